```python
import math
import jax
import jax.numpy as jnp
from jax import lax
import numpy as np

D_MODEL = 1024
BATCH = 16
SEQ = 256
DEPTH = 2
DEC_BATCH = 4
DEC_SEQ = 4096
PAST_LEN = 256

GRID_W = 64
MIX_W = D_MODEL
GROUP_W = MIX_W // 4
EPS = 1e-6
ROPE_BASE = 10000.0
GLA_HEADS = 4
GLA_DK = GROUP_W // GLA_HEADS // 2
GLA_DV = GROUP_W // GLA_HEADS
GLA_LOWRANK = 16
GLA_NORMALIZER = 16.0
GLA_CHUNK = 16
ML_HEADS = 4
ML_DH = GROUP_W // ML_HEADS
ML_CHUNK = 64
NA_HEADS = 4
NA_DH = GROUP_W // NA_HEADS
NA_WIN_R = 8
NA_WIN_C = 16
HY_CH = GROUP_W
HY_ORDER = 2
HY_BANDS = 16
HY_EMB = 1 + 2 * HY_BANDS
HY_FFN = 64
HY_SHORT = 3
HY_DECAY_QUICK = 0.3
HY_DECAY_SLOW = 1.5
HY_TARGET = 1e-2
N_EXPERTS = 32
TOP_K = 4
D_FF = D_MODEL
SWIGLU_LIMIT = 7.0
SWIGLU_ALPHA = 1.702

IN_SIZES = (GLA_HEADS * GLA_DK, GLA_HEADS * GLA_DK, GLA_HEADS * GLA_DV, GLA_HEADS * GLA_DV, GLA_LOWRANK,
            ML_HEADS * ML_DH, ML_HEADS * ML_DH, ML_HEADS * ML_DH, ML_HEADS * ML_DH, 4 * ML_HEADS,
            NA_HEADS * NA_DH, NA_HEADS * NA_DH, NA_HEADS * NA_DH,
            3 * HY_CH)
IN_W = sum(IN_SIZES)

kernel_name = 'hybrid_gla_mlstm_natten_hyena_moe_dit'


def rmsnorm(x, g):
    xf = x.astype(jnp.float32)
    y = xf * lax.rsqrt(jnp.mean(xf * xf, axis=-1, keepdims=True) + EPS)
    return (y * g.astype(jnp.float32)).astype(x.dtype)


def _heads(t, n):
    b, l, _ = t.shape
    return t.reshape(b, l, n, -1).transpose(0, 2, 1, 3)


def _merge(t):
    b, n, l, d = t.shape
    return t.transpose(0, 2, 1, 3).reshape(b, l, n * d)


def axial_rope(x):
    L, dh = x.shape[2], x.shape[3]
    half = dh // 2
    nf = half // 2
    t = jnp.arange(L)
    inv = ROPE_BASE ** (-jnp.arange(nf, dtype=jnp.float32) / nf)

    def rot(xa, pos):
        ang = pos.astype(jnp.float32)[:, None] * inv[None, :]
        cos, sin = jnp.cos(ang), jnp.sin(ang)
        x1, x2 = xa[..., :nf], xa[..., nf:]
        return jnp.concatenate([x1 * cos - x2 * sin, x1 * sin + x2 * cos], axis=-1)

    return jnp.concatenate([rot(x[..., :half], t // GRID_W), rot(x[..., half:], t % GRID_W)], axis=-1)


def gla_chunk_scan(q, k, v, log_a, s0):
    B, H, L, dk = q.shape
    dv = v.shape[-1]
    C = GLA_CHUNK
    N = L // C
    q, k, log_a = (t.reshape(B, H, N, C, dk) for t in (q, k, log_a))
    v = v.reshape(B, H, N, C, dv)
    b = jnp.cumsum(log_a, axis=3)
    b_last = b[:, :, :, -1]
    causal = jnp.tril(jnp.ones((C, C), dtype=bool))
    diff = b[:, :, :, :, None, :] - b[:, :, :, None, :, :]
    decay = jnp.exp(jnp.where(causal[:, :, None], diff, -jnp.inf))
    attn = jnp.einsum('bhnik,bhnjk,bhnijk->bhnij', q, k, decay)
    o_intra = jnp.einsum('bhnij,bhnjv->bhniv', attn, v)
    k_end = k * jnp.exp(b_last[:, :, :, None, :] - b)
    upd = jnp.einsum('bhnck,bhncv->bhnkv', k_end, v)

    def step(s, xs):
        a, u = xs
        return a[..., None] * s + u, s

    s_fin, s_start = lax.scan(step, s0, (jnp.moveaxis(jnp.exp(b_last), 2, 0), jnp.moveaxis(upd, 2, 0)))
    s_start = jnp.moveaxis(s_start, 0, 2)
    o_inter = jnp.einsum('bhnck,bhnkv->bhncv', q * jnp.exp(b), s_start)
    return (o_intra + o_inter).reshape(B, H, L, dv), s_fin


def mlstm_chunk_scan(q, k, v, log_i, log_f, c0, n0, m0):
    B, H, L, dh = q.shape
    C = ML_CHUNK
    N = L // C
    q, k, v = (t.reshape(B, H, N, C, dh) for t in (q, k, v))
    log_i = log_i.reshape(B, H, N, C)
    F = jnp.cumsum(log_f.reshape(B, H, N, C), axis=-1)
    F_last = F[..., -1]
    w_end = F_last[..., None] - F + log_i
    m_loc = jnp.max(w_end, axis=-1)

    def step(carry, xs):
        c, n, m = carry
        fl, we, ml, kc, vc = xs
        m_new = jnp.maximum(fl + m, ml)
        a = jnp.exp(fl + m - m_new)
        wj = jnp.exp(we - m_new[..., None])
        c_new = a[..., None, None] * c + jnp.einsum('bhj,bhjv,bhjk->bhvk', wj, vc, kc)
        n_new = a[..., None] * n + jnp.einsum('bhj,bhjk->bhk', wj, kc)
        return (c_new, n_new, m_new), (c, n, m)

    xs = tuple(jnp.moveaxis(t, 2, 0) for t in (F_last, w_end, m_loc, k, v))
    (c_f, n_f, m_f), (c_s, n_s, m_s) = lax.scan(step, (c0, n0, m0), xs)
    c_s, n_s, m_s = (jnp.moveaxis(t, 0, 2) for t in (c_s, n_s, m_s))
    causal = jnp.tril(jnp.ones((C, C), dtype=bool))
    dmat = jnp.where(causal, F[..., :, None] - F[..., None, :] + log_i[..., None, :], -jnp.inf)
    log_inter = F + m_s[..., None]
    m_t = jnp.maximum(log_inter, jnp.max(dmat, axis=-1))
    s = jnp.einsum('bhntd,bhnjd->bhntj', q, k) * jnp.exp(dmat - m_t[..., None])
    a_t = jnp.exp(log_inter - m_t)
    num = a_t[..., None] * jnp.einsum('bhnvk,bhntk->bhntv', c_s, q) + jnp.einsum('bhntj,bhnjv->bhntv', s, v)
    den = a_t * jnp.einsum('bhnk,bhntk->bhnt', n_s, q) + jnp.sum(s, axis=-1)
    h = num / jnp.maximum(jnp.abs(den), jnp.exp(-m_t))[..., None]
    return h.reshape(B, H, L, dh), (c_f, n_f, m_f)


def natten_latent(q, k, v, k_ctx, v_ctx, rpb):
    B, H, L, dh = q.shape
    rows = L // GRID_W
    wr = min(NA_WIN_R, rows)
    wc = NA_WIN_C
    r = jnp.arange(rows)
    key_rows = jnp.clip(r - wr // 2, 0, rows - wr)[:, None] + jnp.arange(wr)[None, :]
    col = jnp.arange(GRID_W)
    c_start = jnp.clip(col - wc // 2, 0, GRID_W - wc)
    col_mask = (col[None, :] >= c_start[:, None]) & (col[None, :] < c_start[:, None] + wc)
    r_idx = key_rows - r[:, None] + (NA_WIN_R - 1)
    c_idx = jnp.clip(col[None, :] - col[:, None], -(wc - 1), wc - 1) + (NA_WIN_C - 1)
    bias = rpb[:, r_idx[:, None, :, None], c_idx[None, :, None, :]]
    scale = dh ** -0.5
    qg = q.reshape(B, H, rows, GRID_W, dh)
    kg = k.reshape(B, H, rows, GRID_W, dh)[:, :, key_rows]
    vg = v.reshape(B, H, rows, GRID_W, dh)[:, :, key_rows]
    s_loc = jnp.einsum('bhrqd,bhrwkd->bhrqwk', qg, kg) * scale + bias[None]
    s_loc = jnp.where(col_mask[:, None, :], s_loc, -jnp.inf)
    s_ctx = jnp.einsum('bhrqd,bhcd->bhrqc', qg, k_ctx) * scale
    n_loc = wr * GRID_W
    prob = jax.nn.softmax(jnp.concatenate([s_loc.reshape(B, H, rows, GRID_W, n_loc), s_ctx], axis=-1), axis=-1)
    p_loc = prob[..., :n_loc].reshape(B, H, rows, GRID_W, wr, GRID_W)
    out = (jnp.einsum('bhrqwk,bhrwkd->bhrqd', p_loc, vg)
           + jnp.einsum('bhrqc,bhcd->bhrqd', prob[..., n_loc:], v_ctx))
    return out.reshape(B, H, L, dh)


def centred_conv(x, w, b):
    L = x.shape[1]
    pad = HY_SHORT // 2
    xp = jnp.pad(x, ((0, 0), (pad, pad), (0, 0)))
    return sum(xp[:, j:j + L] * w[j] for j in range(HY_SHORT)) + b


def hyena_filters(L, p):
    f32 = jnp.float32
    t = jnp.arange(L, dtype=f32)
    t_unit = t / (L - 1)
    bands = jnp.linspace(1e-4, HY_BANDS - 1, HY_BANDS, dtype=f32)
    ang = (2.0 * math.pi / L) * t[:, None] * bands[None, :]
    feats = jnp.concatenate([t_unit[:, None], jnp.cos(ang), -jnp.sin(ang)], axis=-1)
    freq = p['hy_freq'].astype(f32)
    a = jnp.sin(freq[0] * (feats @ p['hy_w1'].astype(f32) + p['hy_b1'].astype(f32)))
    a = jnp.sin(freq[1] * (a @ p['hy_w2'].astype(f32) + p['hy_b2'].astype(f32)))
    a = (a @ p['hy_w3'].astype(f32) + p['hy_b3'].astype(f32)).reshape(L, 2 * HY_ORDER, HY_CH)
    deltas = jnp.abs(jnp.linspace(math.log(HY_TARGET) / HY_DECAY_SLOW, math.log(HY_TARGET) / HY_DECAY_QUICK,
                                  HY_CH, dtype=f32))
    window = jnp.exp(-t_unit[:, None] * deltas[None, :])
    return a * window[:, None, :]


def bidir_fft_conv(z, h_fwd, h_bwd, d):
    L = z.shape[1]
    filt2l = jnp.concatenate([h_fwd, jnp.zeros_like(h_fwd[:1]), jnp.flip(h_bwd[1:], axis=0)], axis=0)
    zf = jnp.fft.rfft(z, n=2 * L, axis=1)
    hf = jnp.fft.rfft(filt2l, n=2 * L, axis=0)
    y = jnp.fft.irfft(zf * hf[None], n=2 * L, axis=1)[:, :L]
    return y + z * d


def hyena_mixer(u, p):
    f32 = jnp.float32
    L = u.shape[1]
    u = centred_conv(u.astype(f32), p['hy_conv_w'].astype(f32), p['hy_conv_b'].astype(f32))
    x1, x2, z = jnp.split(u, 3, axis=-1)
    filt = hyena_filters(L, p)
    bias = p['hy_bias'].astype(f32)
    z = x1 * bidir_fft_conv(z, filt[:, 0], filt[:, 1], bias[0])
    z = x2 * bidir_fft_conv(z, filt[:, 2], filt[:, 3], bias[1])
    return z


def token_mixing(h, p, ctx):
    f32 = jnp.float32
    B, L, _ = h.shape
    latent = ctx is not None
    splits = np.cumsum(IN_SIZES)[:-1].tolist()
    (gq, gk, gv, gg, glr, mq, mk, mv, mo, mgate, nq, nk, nv, hu) = jnp.split(h @ p['w_in'], splits, axis=-1)

    gq = _heads(gq, GLA_HEADS).astype(f32) * GLA_DK ** -0.5
    gk = _heads(gk, GLA_HEADS).astype(f32)
    gv = _heads(gv, GLA_HEADS).astype(f32)
    glr = glr.astype(f32)
    gla_outs, gla_fin = [], []
    for d in range(2):
        la = jax.nn.log_sigmoid(glr @ p['gla_w_gate'][d].astype(f32) + p['gla_b_gate'][d].astype(f32)) / GLA_NORMALIZER
        seqs = (gq, gk, gv, _heads(la, GLA_HEADS))
        if d == 1:
            seqs = tuple(jnp.flip(a, 2) for a in seqs)
        s0 = ctx['gla'][:, d].astype(f32) if latent else jnp.zeros((B, GLA_HEADS, GLA_DK, GLA_DV), f32)
        o, s_fin = gla_chunk_scan(*seqs, s0)
        gla_outs.append(jnp.flip(o, 2) if d == 1 else o)
        gla_fin.append(s_fin)
    gla_y = _merge(rmsnorm(gla_outs[0] + gla_outs[1], p['gla_norm_g'])) * jax.nn.silu(gg.astype(f32))

    mq = _heads(mq, ML_HEADS).astype(f32)
    mk = _heads(mk, ML_HEADS).astype(f32) * ML_DH ** -0.5
    mv = _heads(mv, ML_HEADS).astype(f32)
    if latent:
        mq, mk = axial_rope(mq), axial_rope(mk)
    gates = jnp.moveaxis(mgate.astype(f32).reshape(B, L, 2, 2, ML_HEADS) + p['ml_b_gate'].astype(f32), 1, -1)
    ml_outs, ml_c, ml_n, ml_m = [], [], [], []
    for d in range(2):
        seqs = (mq, mk, mv, gates[:, d, 0], jax.nn.log_sigmoid(gates[:, d, 1]))
        if d == 1:
            seqs = tuple(jnp.flip(a, 2) for a in seqs)
        if latent:
            init = (ctx['C'][:, d].astype(f32), ctx['n'][:, d].astype(f32), ctx['m'][:, d].astype(f32))
        else:
            init = (jnp.zeros((B, ML_HEADS, ML_DH, ML_DH), f32), jnp.zeros((B, ML_HEADS, ML_DH), f32),
                    jnp.zeros((B, ML_HEADS), f32))
        o, (cf, nf, mf) = mlstm_chunk_scan(*seqs, *init)
        ml_outs.append(jnp.flip(o, 2) if d == 1 else o)
        ml_c.append(cf)
        ml_n.append(nf)
        ml_m.append(mf)
    ml_y = _merge(rmsnorm(ml_outs[0] + ml_outs[1], p['ml_norm_g'][:, None, :])) * jax.nn.sigmoid(mo.astype(f32))

    nq = _heads(nq, NA_HEADS).astype(f32)
    nk = _heads(nk, NA_HEADS).astype(f32)
    nv = _heads(nv, NA_HEADS).astype(f32)
    if latent:
        na = natten_latent(nq, nk, nv, ctx['k'].astype(f32), ctx['v'].astype(f32), p['na_rpb'].astype(f32))
    else:
        s = jnp.einsum('bhqd,bhkd->bhqk', nq, nk) * NA_DH ** -0.5
        na = jnp.einsum('bhqk,bhkd->bhqd', jax.nn.softmax(s, axis=-1), nv)
    na_y = _merge(na)

    hy_y = hyena_mixer(hu, p)

    y = jnp.concatenate([gla_y, ml_y, na_y, hy_y], axis=-1).astype(h.dtype) @ p['w_out']
    if latent:
        return y, None
    return y, (nk, nv, jnp.stack(gla_fin, 1), jnp.stack(ml_c, 1), jnp.stack(ml_n, 1), jnp.stack(ml_m, 1))


def moe_ffn(h, p):
    shp = h.shape
    t = h.reshape(-1, shp[-1])
    logits = (t @ p['router_w'] + p['router_b']).astype(jnp.float32)
    top_val, top_idx = lax.top_k(logits, TOP_K)
    top_w = jax.nn.softmax(top_val, axis=-1)
    gates = jnp.einsum('tk,tke->te', top_w, jax.nn.one_hot(top_idx, N_EXPERTS, dtype=jnp.float32)).astype(h.dtype)
    out = jnp.zeros_like(t)
    for e in range(N_EXPERTS):
        gu = t @ p['w_gu'][e] + p['b_gu'][e]
        g = jnp.minimum(gu[:, :D_FF], SWIGLU_LIMIT)
        u = jnp.clip(gu[:, D_FF:], -SWIGLU_LIMIT, SWIGLU_LIMIT)
        act = g * jax.nn.sigmoid(SWIGLU_ALPHA * g) * (u + 1.0)
        out = out + gates[:, e:e + 1] * (act @ p['w_dn'][e] + p['b_dn'][e])
    return out.reshape(shp)


def trunk_layer(x, cond, p, ctx):
    mod = jax.nn.silu(cond) @ p['w_ada'] + p['b_ada']
    if cond.ndim == 2:
        mod = mod[:, None, :]
    sh1, sc1, g1, sh2, sc2, g2 = jnp.split(mod, 6, axis=-1)
    h = rmsnorm(x, p['norm1_g']) * (1.0 + sc1) + sh1
    mix, ctx_out = token_mixing(h, p, ctx)
    x = x + g1 * mix
    h = rmsnorm(x, p['norm2_g']) * (1.0 + sc2) + sh2
    x = x + g2 * moe_ffn(h, p)
    return x, ctx_out


def setup_inputs(seed: int = 0) -> dict:
    key = jax.random.key(seed)
    ks = iter(jax.random.split(key, 64))
    f32 = jnp.float32

    def nrm(shape, s):
        return jax.random.normal(next(ks), shape, f32) * s

    def gain(shape):
        return 1.0 + nrm(shape, 0.01)

    ml_b_gate = jnp.concatenate([nrm((DEPTH, 2, 1, ML_HEADS), 0.1),
                                 3.0 + 3.0 * jax.random.uniform(next(ks), (DEPTH, 2, 1, ML_HEADS), f32)], axis=2)
    return {
        'x_prompt': nrm((BATCH, SEQ, D_MODEL), 1.0),
        'x_sample': nrm((DEC_BATCH, DEC_SEQ, D_MODEL), 1.0),
        'cache_na_k': nrm((DEC_BATCH, DEPTH, NA_HEADS, PAST_LEN, NA_DH), 1.0),
        'cache_na_v': nrm((DEC_BATCH, DEPTH, NA_HEADS, PAST_LEN, NA_DH), 1.0),
        'state_gla': nrm((DEC_BATCH, DEPTH, 2, GLA_HEADS, GLA_DK, GLA_DV), 0.3),
        'state_mlstm_C': nrm((DEC_BATCH, DEPTH, 2, ML_HEADS, ML_DH, ML_DH), 0.3),
        'state_mlstm_n': nrm((DEC_BATCH, DEPTH, 2, ML_HEADS, ML_DH), 0.3),
        'state_mlstm_m': nrm((DEC_BATCH, DEPTH, 2, ML_HEADS), 1.0),
        'c': nrm((DEC_BATCH, D_MODEL), 1.0),
        'c_ctx': nrm((D_MODEL,), 1.0),
        'w_ada': nrm((DEPTH, D_MODEL, 6 * D_MODEL), 0.5 * D_MODEL ** -0.5),
        'b_ada': nrm((DEPTH, 6 * D_MODEL), 0.02),
        'norm1_g': gain((DEPTH, D_MODEL)),
        'norm2_g': gain((DEPTH, D_MODEL)),
        'w_in': nrm((DEPTH, D_MODEL, IN_W), D_MODEL ** -0.5),
        'w_out': nrm((DEPTH, MIX_W, D_MODEL), MIX_W ** -0.5),
        'gla_w_gate': nrm((DEPTH, 2, GLA_LOWRANK, GLA_HEADS * GLA_DK), GLA_LOWRANK ** -0.5),
        'gla_b_gate': nrm((DEPTH, 2, GLA_HEADS * GLA_DK), 0.1),
        'gla_norm_g': gain((DEPTH, GLA_DV)),
        'ml_b_gate': ml_b_gate,
        'ml_norm_g': gain((DEPTH, ML_HEADS, ML_DH)),
        'na_rpb': nrm((DEPTH, NA_HEADS, 2 * NA_WIN_R - 1, 2 * NA_WIN_C - 1), 0.5),
        'hy_conv_w': nrm((DEPTH, HY_SHORT, 3 * HY_CH), HY_SHORT ** -0.5),
        'hy_conv_b': nrm((DEPTH, 3 * HY_CH), 0.02),
        'hy_w1': nrm((DEPTH, HY_EMB, HY_FFN), HY_EMB ** -0.5),
        'hy_b1': nrm((DEPTH, HY_FFN), 0.1),
        'hy_w2': nrm((DEPTH, HY_FFN, HY_FFN), HY_FFN ** -0.5),
        'hy_b2': nrm((DEPTH, HY_FFN), 0.1),
        'hy_w3': nrm((DEPTH, HY_FFN, 2 * HY_ORDER * HY_CH), 0.05 * HY_FFN ** -0.5),
        'hy_b3': nrm((DEPTH, 2 * HY_ORDER * HY_CH), 0.01),
        'hy_freq': gain((DEPTH, 2, HY_FFN)),
        'hy_bias': nrm((DEPTH, HY_ORDER, HY_CH), 0.5),
        'router_w': nrm((DEPTH, D_MODEL, N_EXPERTS), D_MODEL ** -0.5),
        'router_b': nrm((DEPTH, N_EXPERTS), 0.01),
        'w_gu': nrm((DEPTH, N_EXPERTS, D_MODEL, 2 * D_FF), D_MODEL ** -0.5),
        'b_gu': nrm((DEPTH, N_EXPERTS, 2 * D_FF), 0.01),
        'w_dn': nrm((DEPTH, N_EXPERTS, D_FF, D_MODEL), D_FF ** -0.5),
        'b_dn': nrm((DEPTH, N_EXPERTS, D_MODEL), 0.01),
        'final_norm_g': gain((D_MODEL,)),
    }


def reference(x_prompt, x_sample, cache_na_k, cache_na_v, state_gla, state_mlstm_C, state_mlstm_n, state_mlstm_m,
              c, c_ctx, w_ada, b_ada, norm1_g, norm2_g, w_in, w_out, gla_w_gate, gla_b_gate, gla_norm_g,
              ml_b_gate, ml_norm_g, na_rpb, hy_conv_w, hy_conv_b, hy_w1, hy_b1, hy_w2, hy_b2, hy_w3, hy_b3,
              hy_freq, hy_bias, router_w, router_b, w_gu, b_gu, w_dn, b_dn, final_norm_g):
    xp = x_prompt
    xs = x_sample
    new_k, new_v, new_gla, new_c, new_n, new_m = [], [], [], [], [], []
    for l in range(DEPTH):
        p = {'w_ada': w_ada[l], 'b_ada': b_ada[l], 'norm1_g': norm1_g[l], 'norm2_g': norm2_g[l],
             'w_in': w_in[l], 'w_out': w_out[l], 'gla_w_gate': gla_w_gate[l], 'gla_b_gate': gla_b_gate[l],
             'gla_norm_g': gla_norm_g[l], 'ml_b_gate': ml_b_gate[l], 'ml_norm_g': ml_norm_g[l],
             'na_rpb': na_rpb[l], 'hy_conv_w': hy_conv_w[l], 'hy_conv_b': hy_conv_b[l],
             'hy_w1': hy_w1[l], 'hy_b1': hy_b1[l], 'hy_w2': hy_w2[l], 'hy_b2': hy_b2[l],
             'hy_w3': hy_w3[l], 'hy_b3': hy_b3[l], 'hy_freq': hy_freq[l], 'hy_bias': hy_bias[l],
             'router_w': router_w[l], 'router_b': router_b[l], 'w_gu': w_gu[l], 'b_gu': b_gu[l],
             'w_dn': w_dn[l], 'b_dn': b_dn[l]}
        xp, (k_l, v_l, gla_l, c_l, n_l, m_l) = trunk_layer(xp, c_ctx, p, None)
        new_k.append(k_l)
        new_v.append(v_l)
        new_gla.append(gla_l)
        new_c.append(c_l)
        new_n.append(n_l)
        new_m.append(m_l)
        ctx = {'k': cache_na_k[:, l], 'v': cache_na_v[:, l], 'gla': state_gla[:, l],
               'C': state_mlstm_C[:, l], 'n': state_mlstm_n[:, l], 'm': state_mlstm_m[:, l]}
        xs, _ = trunk_layer(xs, c, p, ctx)
    y_prompt = rmsnorm(xp, final_norm_g)
    y_sample = rmsnorm(xs, final_norm_g)
    return (y_prompt, y_sample, jnp.stack(new_k, 1), jnp.stack(new_v, 1), jnp.stack(new_gla, 1),
            jnp.stack(new_c, 1), jnp.stack(new_n, 1), jnp.stack(new_m, 1))
```

```python
import functools
import math

import numpy as np
import jax
import jax.numpy as jnp
from jax import lax
from jax.experimental import pallas as pl
from jax.experimental.pallas import tpu as pltpu

F32 = jnp.float32
BF16 = jnp.bfloat16
HI = lax.Precision.HIGHEST

LANES = 128
SUBLANES = 8
VMEM_LIMIT = 56 * 1024 * 1024

D = 1024
EPS = 1e-6
N_EXP = 32
TOP_K = 4
D_FF = 1024
SWIGLU_LIMIT = 7.0
SWIGLU_ALPHA = 1.702
GRID_W = 64

TB = 256
ROWS_BS = TB * TOP_K + N_EXP * SUBLANES
CH = SUBLANES
NCH = ROWS_BS // CH
TM = 256

P_HU = 0
P_GQ, P_GK, P_GV, P_GG = 768, 896, 1024, 1280
P_MQ, P_MK, P_MV, P_MO = 1536, 1792, 2048, 2304
P_NQ, P_NK, P_NV = 2560, 2816, 3072
P_AUX = 3328
P_W = 3456


def _cparams(sem=None):
    return pltpu.CompilerParams(dimension_semantics=sem, vmem_limit_bytes=VMEM_LIMIT)


def _ada_kernel(c_ref, w_ref, b_ref, o_ref):
    c = c_ref[...]
    s = c * jax.nn.sigmoid(c)
    o_ref[...] = jnp.dot(s, w_ref[...], precision=HI, preferred_element_type=F32) + b_ref[...]


def ada_mods(cond8, w_ada, b_ada):
    depth = w_ada.shape[0]
    tn = 1536
    return pl.pallas_call(
        _ada_kernel,
        grid=(depth, 6 * D // tn),
        in_specs=[pl.BlockSpec((8, D), lambda l, j: (0, 0)),
                  pl.BlockSpec((None, D, tn), lambda l, j: (l, 0, j)),
                  pl.BlockSpec((None, 1, tn), lambda l, j: (l, 0, j))],
        out_specs=pl.BlockSpec((None, 8, tn), lambda l, j: (l, 0, j)),
        out_shape=jax.ShapeDtypeStruct((depth, 8, 6 * D), F32),
        compiler_params=_cparams(("arbitrary", "arbitrary")),
        name="ada_mods",
    )(cond8, w_ada, b_ada.reshape(depth, 1, 6 * D))


def _rms_mod(x, g, sc, sh):
    ms = jnp.mean(x * x, axis=-1, keepdims=True)
    return (x * lax.rsqrt(ms + EPS) * g) * (1.0 + sc) + sh


def _in_kernel(x_ref, mod_ref, g_ref, w_ref, o_ref):
    h = _rms_mod(x_ref[...], g_ref[...], mod_ref[:, D:2 * D], mod_ref[:, 0:D])
    o_ref[...] = jnp.dot(h.astype(BF16), w_ref[...], preferred_element_type=F32)


def in_proj(x, mods, g, w_bf16, tok_per_mod):
    t = x.shape[0]
    tm = 256
    return pl.pallas_call(
        _in_kernel,
        grid=(t // tm,),
        in_specs=[pl.BlockSpec((tm, D), lambda i: (i, 0)),
                  pl.BlockSpec((None, 1, 6 * D), lambda i: (i * tm // tok_per_mod, 0, 0)),
                  pl.BlockSpec((1, D), lambda i: (0, 0)),
                  pl.BlockSpec((D, P_W), lambda i: (0, 0))],
        out_specs=pl.BlockSpec((tm, P_W), lambda i: (i, 0)),
        out_shape=jax.ShapeDtypeStruct((t, P_W), F32),
        compiler_params=_cparams(("arbitrary",)),
        name="in_proj",
    )(x, mods, g, w_bf16)


def _out_kernel(ya_ref, yb_ref, yc_ref, yd_ref, x_ref, mod_ref, g_ref, w_ref, rw_ref, rb_ref,
                x1_ref, h2_ref, eidx_ref, ew_ref, cnt_ref):
    mix = jnp.dot(ya_ref[...].astype(BF16), w_ref[0:256, :], preferred_element_type=F32)
    mix += jnp.dot(yb_ref[...].astype(BF16), w_ref[256:512, :], preferred_element_type=F32)
    mix += jnp.dot(yc_ref[...].astype(BF16), w_ref[512:768, :], preferred_element_type=F32)
    mix += jnp.dot(yd_ref[...].astype(BF16), w_ref[768:1024, :], preferred_element_type=F32)
    x1 = x_ref[...] + mod_ref[:, 2 * D:3 * D] * mix
    x1_ref[...] = x1
    h2 = _rms_mod(x1, g_ref[...], mod_ref[:, 4 * D:5 * D], mod_ref[:, 3 * D:4 * D])
    h2_ref[...] = h2.astype(BF16)
    lg = lax.dot_general(rw_ref[...], h2, (((1,), (1,)), ((), ())), precision=HI,
                         preferred_element_type=F32) + rb_ref[:, 0:1]
    e_iota = lax.broadcasted_iota(jnp.int32, lg.shape, 0)
    vals, idxs = [], []
    for _ in range(TOP_K):
        m = jnp.max(lg, axis=0, keepdims=True)
        idx = jnp.min(jnp.where(lg == m, e_iota, N_EXP), axis=0, keepdims=True)
        vals.append(m)
        idxs.append(idx)
        lg = jnp.where(e_iota == idx, -jnp.inf, lg)
    ex = [jnp.exp(v - vals[0]) for v in vals]
    den = ex[0] + ex[1] + ex[2] + ex[3]
    eidx_ref[...] = jnp.concatenate(idxs, axis=0)
    ew_ref[...] = jnp.concatenate([e / den for e in ex], axis=0)
    ind = jnp.zeros(lg.shape, F32)
    for idx in idxs:
        ind += (e_iota == idx).astype(F32)
    cnt_ref[...] = jnp.broadcast_to(jnp.sum(ind, axis=1, keepdims=True), (N_EXP, LANES))


def out_proj_route(ys, x, mods, g, w_bf16, rw_t, rb, tok_per_mod):
    t = x.shape[0]
    nb = t // TB
    yspec = pl.BlockSpec((TB, 256), lambda i: (i, 0))
    return pl.pallas_call(
        _out_kernel,
        grid=(nb,),
        in_specs=[yspec, yspec, yspec, yspec,
                  pl.BlockSpec((TB, D), lambda i: (i, 0)),
                  pl.BlockSpec((None, 1, 6 * D), lambda i: (i * TB // tok_per_mod, 0, 0)),
                  pl.BlockSpec((1, D), lambda i: (0, 0)),
                  pl.BlockSpec((D, D), lambda i: (0, 0)),
                  pl.BlockSpec((N_EXP, D), lambda i: (0, 0)),
                  pl.BlockSpec((N_EXP, LANES), lambda i: (0, 0))],
        out_specs=[pl.BlockSpec((TB, D), lambda i: (i, 0)),
                   pl.BlockSpec((TB, D), lambda i: (i, 0)),
                   pl.BlockSpec((None, TOP_K, TB), lambda i: (i, 0, 0)),
                   pl.BlockSpec((None, TOP_K, TB), lambda i: (i, 0, 0)),
                   pl.BlockSpec((None, N_EXP, LANES), lambda i: (i, 0, 0))],
        out_shape=[jax.ShapeDtypeStruct((t, D), F32),
                   jax.ShapeDtypeStruct((t, D), BF16),
                   jax.ShapeDtypeStruct((nb, TOP_K, TB), jnp.int32),
                   jax.ShapeDtypeStruct((nb, TOP_K, TB), F32),
                   jax.ShapeDtypeStruct((nb, N_EXP, LANES), F32)],
        compiler_params=_cparams(("arbitrary",)),
        name="out_proj_route",
    )(*ys, x, mods, g, w_bf16, rw_t, rb)


def moe_tables(cnt, n_tiles):
    nb = cnt.shape[0]
    cnt8 = (cnt + CH - 1) // CH * CH
    ends = jnp.cumsum(cnt8, axis=1)
    off = ends - cnt8
    nchunks = ends[:, -1] // CH
    tot = jnp.sum(cnt8, axis=0)
    totp = (tot + TM - 1) // TM * TM
    eend = jnp.cumsum(totp)
    estart = eend - totp
    gdst = estart[None, :] + jnp.cumsum(cnt8, axis=0) - cnt8
    r = jnp.arange(NCH, dtype=jnp.int32) * CH
    e_of_c = jnp.minimum(jnp.sum((ends[:, None, :] <= r[None, :, None]).astype(jnp.int32), axis=-1), N_EXP - 1)
    g_of_c = jnp.take_along_axis(gdst, e_of_c, axis=1)
    o_of_c = jnp.take_along_axis(off, e_of_c, axis=1)
    gchunk = (g_of_c + r[None, :] - o_of_c) // CH
    nused = eend[-1] // TM
    ti = jnp.arange(n_tiles, dtype=jnp.int32)
    tile_e = jnp.sum((eend[None, :] // TM <= jnp.minimum(ti, nused - 1)[:, None]).astype(jnp.int32), axis=-1)
    tile_e = jnp.minimum(tile_e, N_EXP - 1)
    return (off, gchunk.reshape(-1).astype(jnp.int32), nchunks.astype(jnp.int32),
            tile_e.astype(jnp.int32), nused.reshape(1).astype(jnp.int32))


def _dest_rows(eidx, off_col):
    e_iota = lax.broadcasted_iota(jnp.int32, (N_EXP, TB), 0)
    ohs = [e_iota == eidx[k:k + 1, :] for k in range(TOP_K)]
    ind = jnp.zeros((N_EXP, TB), F32)
    for oh in ohs:
        ind += oh.astype(F32)
    ti = lax.broadcasted_iota(jnp.int32, (TB, TB), 0)
    tj = lax.broadcasted_iota(jnp.int32, (TB, TB), 1)
    upper = (ti <= tj).astype(BF16)
    rank_incl = jnp.dot(ind.astype(BF16), upper, preferred_element_type=F32)
    base = off_col + rank_incl - ind
    return [jnp.sum(jnp.where(oh, base, 0.0), axis=0, keepdims=True).astype(jnp.int32) for oh in ohs]


def _dispatch_kernel(gchunk_ref, nch_ref, h2_ref, eidx_ref, off_ref, zeros_ref, xs_ref, xbs_ref, sem):
    del zeros_ref
    blk = pl.program_id(0)
    dests = _dest_rows(eidx_ref[...], off_ref[:, 0:1])
    p_iota = lax.broadcasted_iota(jnp.int32, (ROWS_BS, TB), 0)
    perm = jnp.zeros((ROWS_BS, TB), F32)
    for d in dests:
        perm += (p_iota == d).astype(F32)
    xbs_ref[...] = jnp.dot(perm.astype(BF16), h2_ref[...], preferred_element_type=F32)
    n = nch_ref[blk]

    def copy(c):
        dst = pl.multiple_of(gchunk_ref[blk * NCH + c] * CH, CH)
        src = pl.multiple_of(c * CH, CH)
        return pltpu.make_async_copy(xbs_ref.at[pl.ds(src, CH), :], xs_ref.at[pl.ds(dst, CH), :], sem)

    def start(c, carry):
        copy(c).start()
        return carry

    def wait(c, carry):
        copy(c).wait()
        return carry

    lax.fori_loop(0, n, start, 0)
    lax.fori_loop(0, n, wait, 0)


def moe_dispatch(h2, eidx, off_b, gchunk, nchunks, rows_alloc):
    nb = h2.shape[0] // TB
    return pl.pallas_call(
        _dispatch_kernel,
        grid_spec=pltpu.PrefetchScalarGridSpec(
            num_scalar_prefetch=2,
            grid=(nb,),
            in_specs=[pl.BlockSpec((TB, D), lambda i, *_: (i, 0)),
                      pl.BlockSpec((None, TOP_K, TB), lambda i, *_: (i, 0, 0)),
                      pl.BlockSpec((None, N_EXP, LANES), lambda i, *_: (i, 0, 0)),
                      pl.BlockSpec(memory_space=pl.ANY)],
            out_specs=pl.BlockSpec(memory_space=pl.ANY),
            scratch_shapes=[pltpu.VMEM((ROWS_BS, D), F32), pltpu.SemaphoreType.DMA(())]),
        out_shape=jax.ShapeDtypeStruct((rows_alloc, D), F32),
        input_output_aliases={5: 0},
        compiler_params=_cparams(("arbitrary",)),
        name="moe_dispatch",
    )(gchunk, nchunks, h2, eidx, off_b, jnp.zeros((rows_alloc, D), F32))


def _expert_kernel(te_ref, nused_ref, x_ref, wgu_ref, bgu_ref, wdn_ref, bdn_ref, y_ref, wgu_bf, wdn_bf):
    i = pl.program_id(0)

    @pl.when(i >= nused_ref[0])
    def _():
        y_ref[...] = jnp.zeros(y_ref.shape, F32)

    @pl.when(i < nused_ref[0])
    def _():
        first = jnp.logical_or(i == 0, te_ref[i] != te_ref[jnp.maximum(i - 1, 0)])

        @pl.when(first)
        def _():
            wgu_bf[...] = wgu_ref[...].astype(BF16)
            wdn_bf[...] = wdn_ref[...].astype(BF16)

        gu = jnp.dot(x_ref[...].astype(BF16), wgu_bf[...], preferred_element_type=F32) + bgu_ref[...]
        g = jnp.minimum(gu[:, 0:D_FF], SWIGLU_LIMIT)
        u = jnp.clip(gu[:, D_FF:2 * D_FF], -SWIGLU_LIMIT, SWIGLU_LIMIT)
        act = g * jax.nn.sigmoid(SWIGLU_ALPHA * g) * (u + 1.0)
        y_ref[...] = jnp.dot(act.astype(BF16), wdn_bf[...], preferred_element_type=F32) + bdn_ref[...]


def moe_experts(xs, tile_e, nused, w_gu, b_gu, w_dn, b_dn):
    n_tiles = xs.shape[0] // TM

    def xmap(i, te, nu):
        return (jnp.minimum(i, nu[0] - 1), 0)

    def wmap(i, te, nu):
        return (te[i], 0, 0)

    return pl.pallas_call(
        _expert_kernel,
        grid_spec=pltpu.PrefetchScalarGridSpec(
            num_scalar_prefetch=2,
            grid=(n_tiles,),
            in_specs=[pl.BlockSpec((TM, D), xmap),
                      pl.BlockSpec((None, D, 2 * D_FF), wmap),
                      pl.BlockSpec((None, 1, 2 * D_FF), wmap),
                      pl.BlockSpec((None, D_FF, D), wmap),
                      pl.BlockSpec((None, 1, D), wmap)],
            out_specs=pl.BlockSpec((TM, D), lambda i, te, nu: (i, 0)),
            scratch_shapes=[pltpu.VMEM((D, 2 * D_FF), BF16), pltpu.VMEM((D_FF, D), BF16)]),
        out_shape=jax.ShapeDtypeStruct((xs.shape[0], D), F32),
        compiler_params=_cparams(("arbitrary",)),
        name="moe_experts",
    )(tile_e, nused, xs, w_gu, b_gu.reshape(N_EXP, 1, 2 * D_FF), w_dn, b_dn.reshape(N_EXP, 1, D))


def _combine_kernel(final, gchunk_ref, nch_ref, eidx_ref, ew_ref, off_ref, x1_ref, mod_ref, fg_ref, ys_ref,
                    o_ref, ybs_ref, sem):
    blk = pl.program_id(0)
    n = nch_ref[blk]

    def copy(c):
        src = pl.multiple_of(gchunk_ref[blk * NCH + c] * CH, CH)
        dst = pl.multiple_of(c * CH, CH)
        return pltpu.make_async_copy(ys_ref.at[pl.ds(src, CH), :], ybs_ref.at[pl.ds(dst, CH), :], sem)

    def start(c, carry):
        copy(c).start()
        return carry

    def wait(c, carry):
        copy(c).wait()
        return carry

    lax.fori_loop(0, n, start, 0)
    dests = _dest_rows(eidx_ref[...], off_ref[:, 0:1])
    ew = ew_ref[...]
    p_iota = lax.broadcasted_iota(jnp.int32, (ROWS_BS, TB), 0)
    perm = jnp.zeros((ROWS_BS, TB), F32)
    gsel = jnp.zeros((ROWS_BS, TB), F32)
    for k, d in enumerate(dests):
        hit = p_iota == d
        perm += hit.astype(F32)
        gsel += jnp.where(hit, ew[k:k + 1, :], 0.0)
    gate_col = jnp.sum(gsel, axis=1, keepdims=True)
    lax.fori_loop(0, n, wait, 0)
    row_iota = lax.broadcasted_iota(jnp.int32, (ROWS_BS, 1), 0)
    yb = jnp.where(row_iota < n * CH, ybs_ref[...], 0.0) * gate_col
    moe = lax.dot_general(perm.astype(BF16), yb.astype(BF16), (((0,), (0,)), ((), ())),
                          preferred_element_type=F32)
    x2 = x1_ref[...] + mod_ref[:, 5 * D:6 * D] * moe
    if final:
        ms = jnp.mean(x2 * x2, axis=-1, keepdims=True)
        x2 = x2 * lax.rsqrt(ms + EPS) * fg_ref[...]
    o_ref[...] = x2


def moe_combine(ys, eidx, ew, off_b, x1, mods, fg, gchunk, nchunks, tok_per_mod, final):
    t = x1.shape[0]
    nb = t // TB
    return pl.pallas_call(
        functools.partial(_combine_kernel, final),
        grid_spec=pltpu.PrefetchScalarGridSpec(
            num_scalar_prefetch=2,
            grid=(nb,),
            in_specs=[pl.BlockSpec((None, TOP_K, TB), lambda i, *_: (i, 0, 0)),
                      pl.BlockSpec((None, TOP_K, TB), lambda i, *_: (i, 0, 0)),
                      pl.BlockSpec((None, N_EXP, LANES), lambda i, *_: (i, 0, 0)),
                      pl.BlockSpec((TB, D), lambda i, *_: (i, 0)),
                      pl.BlockSpec((None, 1, 6 * D), lambda i, *_: (i * TB // tok_per_mod, 0, 0)),
                      pl.BlockSpec((1, D), lambda i, *_: (0, 0)),
                      pl.BlockSpec(memory_space=pl.ANY)],
            out_specs=pl.BlockSpec((TB, D), lambda i, *_: (i, 0)),
            scratch_shapes=[pltpu.VMEM((ROWS_BS, D), F32), pltpu.SemaphoreType.DMA(())]),
        out_shape=jax.ShapeDtypeStruct((t, D), F32),
        compiler_params=_cparams(("arbitrary",)),
        name="moe_combine",
    )(gchunk, nchunks, eidx, ew, off_b, x1, mods, fg, ys)


def moe_layer(h2, eidx, ew, cnt_b, x1, mods, fg, w_gu, b_gu, w_dn, b_dn, tok_per_mod, final):
    t = h2.shape[0]
    nb = t // TB
    max_rows = t * TOP_K + nb * N_EXP * (CH - 1) + N_EXP * (TM - CH)
    n_tiles = (max_rows + TM - 1) // TM
    cnt = cnt_b[:, :, 0].astype(jnp.int32)
    off, gchunk, nchunks, tile_e, nused = moe_tables(cnt, n_tiles)
    off_b = jnp.broadcast_to(off.astype(F32)[:, :, None], (nb, N_EXP, LANES))
    xs = moe_dispatch(h2, eidx, off_b, gchunk, nchunks, n_tiles * TM)
    ys = moe_experts(xs, tile_e, nused, w_gu, b_gu, w_dn, b_dn)
    return moe_combine(ys, eidx, ew, off_b, x1, mods, fg, gchunk, nchunks, tok_per_mod, final)


N_HEADS = 4


def _stack_heads(x, head_w):
    lane_h = lax.broadcasted_iota(jnp.int32, x.shape, 1) // head_w
    return jnp.concatenate([jnp.where(lane_h == h, x, 0.0) for h in range(N_HEADS)], axis=0)


def _unstack_heads(xs, head_w):
    r = xs.shape[0] // N_HEADS
    lane_h = lax.broadcasted_iota(jnp.int32, (r, xs.shape[1]), 1) // head_w
    out = jnp.zeros((r, xs.shape[1]), F32)
    for h in range(N_HEADS):
        out = jnp.where(lane_h == h, xs[h * r:(h + 1) * r, :], out)
    return out


def _block_diag_mask(rows, cols, rw, cw):
    ri = lax.broadcasted_iota(jnp.int32, (rows, cols), 0) // rw
    ci = lax.broadcasted_iota(jnp.int32, (rows, cols), 1) // cw
    return ri == ci


def _head_rmsnorm(o, head_w):
    n = o.shape[1]
    bd = _block_diag_mask(n, n, head_w, head_w).astype(F32)
    ms = jnp.dot(o * o, bd, precision=HI, preferred_element_type=F32) * (1.0 / head_w)
    return o * lax.rsqrt(ms + EPS)


def _nt(a, b, **kw):
    return lax.dot_general(a, b, (((1,), (1,)), ((), ())), preferred_element_type=F32, **kw)


def _tn(a, b, **kw):
    return lax.dot_general(a, b, (((0,), (0,)), ((), ())), preferred_element_type=F32, **kw)


NA_DH = 64
NA_WIN_R = 8
NA_WIN_C = 16
NA_ROWS = 64


def _na_lat_kernel(q_ref, k_ref, v_ref, kc_ref, vc_ref, bias_ref, o_ref):
    r = pl.program_id(1)
    start = jnp.clip(r - NA_WIN_R // 2, 0, NA_ROWS - NA_WIN_R)
    rows = pl.ds(pl.multiple_of(start * GRID_W, GRID_W), NA_WIN_R * GRID_W)
    qs = _stack_heads(q_ref[...] * (NA_DH ** -0.5), NA_DH).astype(BF16)
    kl = k_ref[rows, :].astype(BF16)
    vl = v_ref[rows, :].astype(BF16)
    s_loc = _nt(qs, kl) + bias_ref[...]
    s_ctx = _nt(qs, kc_ref[...].astype(BF16))
    m = jnp.maximum(jnp.max(s_loc, axis=1, keepdims=True), jnp.max(s_ctx, axis=1, keepdims=True))
    p_loc = jnp.exp(s_loc - m)
    p_ctx = jnp.exp(s_ctx - m)
    den = jnp.sum(p_loc, axis=1, keepdims=True) + jnp.sum(p_ctx, axis=1, keepdims=True)
    o = jnp.dot(p_loc.astype(BF16), vl, preferred_element_type=F32)
    o += jnp.dot(p_ctx.astype(BF16), vc_ref[...].astype(BF16), preferred_element_type=F32)
    o_ref[...] = _unstack_heads(o / den, NA_DH)


def _na_bias_table(rpb):
    col = np.arange(GRID_W)
    c_start = np.clip(col - NA_WIN_C // 2, 0, GRID_W - NA_WIN_C)
    col_mask = (col[None, :] >= c_start[:, None]) & (col[None, :] < c_start[:, None] + NA_WIN_C)
    c_idx = np.clip(col[None, :] - col[:, None], -(NA_WIN_C - 1), NA_WIN_C - 1) + (NA_WIN_C - 1)
    tb = jnp.where(col_mask[None, None], rpb[:, :, c_idx], -jnp.inf)
    out = []
    for ri0 in range(NA_WIN_R):
        blk = tb[:, ri0:ri0 + NA_WIN_R]
        out.append(blk.transpose(0, 2, 1, 3).reshape(N_HEADS * GRID_W, NA_WIN_R * GRID_W))
    return jnp.stack(out, 0)


def na_latent(proj, tok0, bs, ls, kc, vc, rpb):
    lc = kc.shape[1]
    bias = _na_bias_table(rpb)
    nrow = ls // GRID_W
    rb0 = tok0 // GRID_W
    sb0 = tok0 // ls
    return pl.pallas_call(
        _na_lat_kernel,
        grid=(bs, nrow),
        in_specs=[pl.BlockSpec((GRID_W, 256), lambda b, r: (rb0 + b * nrow + r, P_NQ // 256)),
                  pl.BlockSpec((ls, 256), lambda b, r: (sb0 + b, P_NK // 256)),
                  pl.BlockSpec((ls, 256), lambda b, r: (sb0 + b, P_NV // 256)),
                  pl.BlockSpec((None, lc, 256), lambda b, r: (b, 0, 0)),
                  pl.BlockSpec((None, lc, 256), lambda b, r: (b, 0, 0)),
                  pl.BlockSpec((None, N_HEADS * GRID_W, NA_WIN_R * GRID_W),
                               lambda b, r: (jnp.clip(r - NA_WIN_R // 2, 0, NA_ROWS - NA_WIN_R) - r + NA_WIN_R - 1, 0, 0))],
        out_specs=pl.BlockSpec((GRID_W, 256), lambda b, r: (b * nrow + r, 0)),
        out_shape=jax.ShapeDtypeStruct((bs * ls, 256), F32),
        compiler_params=_cparams(("arbitrary", "arbitrary")),
        name="na_latent",
    )(proj, proj, proj, kc, vc, bias)


def _na_ctx_kernel(q_ref, k_ref, v_ref, o_ref):
    qs = _stack_heads(q_ref[...] * (NA_DH ** -0.5), NA_DH).astype(BF16)
    s = _nt(qs, k_ref[...].astype(BF16))
    m = jnp.max(s, axis=1, keepdims=True)
    p = jnp.exp(s - m)
    den = jnp.sum(p, axis=1, keepdims=True)
    o = jnp.dot(p.astype(BF16), v_ref[...].astype(BF16), preferred_element_type=F32)
    o_ref[...] = _unstack_heads(o / den, NA_DH)


def na_context(proj, bp, lp):
    return pl.pallas_call(
        _na_ctx_kernel,
        grid=(bp,),
        in_specs=[pl.BlockSpec((lp, 256), lambda b: (b, P_NQ // 256)),
                  pl.BlockSpec((lp, 256), lambda b: (b, P_NK // 256)),
                  pl.BlockSpec((lp, 256), lambda b: (b, P_NV // 256))],
        out_specs=pl.BlockSpec((lp, 256), lambda b: (b, 0)),
        out_shape=jax.ShapeDtypeStruct((bp * lp, 256), F32),
        compiler_params=_cparams(("arbitrary",)),
        name="na_context",
    )(proj, proj, proj)


GLA_DK = 32
GLA_DV = 64
GLA_C = 64
GLA_NORMALIZER = 16.0


def _gla_kernel(has_state, seq, *refs):
    if has_state:
        q_ref, k_ref, v_ref, g_ref, aux_ref, wg_ref, bg_ref, gn_ref, s0_ref, y_ref, acc_ref = refs
    else:
        q_ref, k_ref, v_ref, g_ref, aux_ref, wg_ref, bg_ref, gn_ref, y_ref, sfin_ref, acc_ref = refs
    c_sz = GLA_C
    n_chunks = seq // c_sz
    ti = lax.broadcasted_iota(jnp.int32, (c_sz, c_sz), 0)
    tj = lax.broadcasted_iota(jnp.int32, (c_sz, c_sz), 1)
    ai = lax.broadcasted_iota(jnp.int32, (N_HEADS * c_sz, c_sz), 0) % c_sz
    aj = lax.broadcasted_iota(jnp.int32, (N_HEADS * c_sz, c_sz), 1)
    bd = _block_diag_mask(N_HEADS * GLA_DV, N_HEADS * GLA_DK, GLA_DV, GLA_DK)
    for d in range(2):
        tri = ((tj <= ti) if d == 0 else (tj >= ti)).astype(F32)
        amask = (aj <= ai) if d == 0 else (aj >= ai)
        wg = wg_ref[d]
        bg = bg_ref[d]

        def body(n, st, d=d, tri=tri, amask=amask, wg=wg, bg=bg):
            c = n if d == 0 else n_chunks - 1 - n
            rows = pl.ds(pl.multiple_of(c * c_sz, c_sz), c_sz)
            q = q_ref[rows, :] * (GLA_DK ** -0.5)
            k = k_ref[rows, :]
            v = v_ref[rows, :].astype(BF16)
            la = jax.nn.log_sigmoid(jnp.dot(aux_ref[rows, :], wg, precision=HI, preferred_element_type=F32) + bg)
            la = la * (1.0 / GLA_NORMALIZER)
            b = jnp.dot(tri, la, precision=HI, preferred_element_type=F32)
            btot = b[c_sz - 1:c_sz, :] if d == 0 else b[0:1, :]
            qt = q * jnp.exp(b)
            kt = (k * jnp.exp(-b)).astype(BF16)
            ke = (k * jnp.exp(btot - b)).astype(BF16)
            a = _nt(_stack_heads(qt, GLA_DK).astype(BF16), kt)
            a = jnp.where(amask, a, 0.0).astype(BF16)
            o = _unstack_heads(jnp.dot(a, v, preferred_element_type=F32), GLA_DV)
            o += _nt(qt.astype(BF16), st.astype(BF16))
            if d == 0:
                acc_ref[rows, :] = o
            else:
                acc_ref[rows, :] += o
            upd = _tn(v, ke)
            return st * jnp.exp(btot) + jnp.where(bd, upd, 0.0)

        st0 = s0_ref[d] if has_state else jnp.zeros((N_HEADS * GLA_DV, N_HEADS * GLA_DK), F32)
        st = lax.fori_loop(0, n_chunks, body, st0)
        if not has_state:
            sfin_ref[d] = st
    gn = gn_ref[...]

    def epi(i, carry):
        rows = pl.ds(pl.multiple_of(i * 256, 256), 256)
        g = g_ref[rows, :]
        y_ref[rows, :] = _head_rmsnorm(acc_ref[rows, :], GLA_DV) * gn * (g * jax.nn.sigmoid(g))
        return carry

    lax.fori_loop(0, seq // 256, epi, 0)


def gla_mixer(proj, tok0, nbatch, seq, w_gate, b_gate, norm_g, s0t):
    has_state = s0t is not None
    sb = tok0 // seq
    wg = jnp.zeros((2, LANES, LANES), F32).at[:, 0:w_gate.shape[1], :].set(w_gate)
    in_specs = [pl.BlockSpec((seq, 128), lambda b: (sb + b, P_GQ // 128)),
                pl.BlockSpec((seq, 128), lambda b: (sb + b, P_GK // 128)),
                pl.BlockSpec((seq, 256), lambda b: (sb + b, P_GV // 256)),
                pl.BlockSpec((seq, 256), lambda b: (sb + b, P_GG // 256)),
                pl.BlockSpec((seq, 128), lambda b: (sb + b, P_AUX // 128)),
                pl.BlockSpec((2, LANES, LANES), lambda b: (0, 0, 0)),
                pl.BlockSpec((2, 1, LANES), lambda b: (0, 0, 0)),
                pl.BlockSpec((1, 256), lambda b: (0, 0))]
    args = [proj, proj, proj, proj, proj, wg, b_gate.reshape(2, 1, LANES), jnp.tile(norm_g, N_HEADS).reshape(1, 256)]
    y_spec = pl.BlockSpec((seq, 256), lambda b: (b, 0))
    y_shape = jax.ShapeDtypeStruct((nbatch * seq, 256), F32)
    st_spec = pl.BlockSpec((None, 2, 256, 128), lambda b: (b, 0, 0, 0))
    if has_state:
        in_specs.append(st_spec)
        args.append(s0t)
        out_specs, out_shape = y_spec, y_shape
    else:
        out_specs = [y_spec, st_spec]
        out_shape = [y_shape, jax.ShapeDtypeStruct((nbatch, 2, 256, 128), F32)]
    return pl.pallas_call(
        functools.partial(_gla_kernel, has_state, seq),
        grid=(nbatch,),
        in_specs=in_specs, out_specs=out_specs, out_shape=out_shape,
        scratch_shapes=[pltpu.VMEM((seq, 256), F32)],
        compiler_params=_cparams(("arbitrary",)),
        name="gla_latent" if has_state else "gla_context",
    )(*args)


ML_DH = 64
ML_C = 256
ROPE_BASE = 10000.0
ML_GATE_LANE0 = 16


def _ml_gate_selectors():
    rep = np.zeros((2, 2, LANES, N_HEADS * ML_DH), np.float32)
    sel = np.zeros((2, 8, LANES), np.float32)
    for d in range(2):
        for g in range(2):
            for h in range(N_HEADS):
                lane = ML_GATE_LANE0 + d * 8 + g * 4 + h
                rep[d, g, lane, h * ML_DH:(h + 1) * ML_DH] = 1.0
                sel[d, g * 4 + h, lane] = 1.0
    return jnp.asarray(rep), jnp.asarray(sel)


def _rope_tables(seq):
    nf = ML_DH // 4
    inv = ROPE_BASE ** (-jnp.arange(nf, dtype=F32) / nf)
    t = np.arange(seq)
    j = np.arange(ML_DH)
    pos = np.where(j[None, :] < ML_DH // 2, (t // GRID_W)[:, None], (t % GRID_W)[:, None]).astype(np.float32)
    ang = jnp.asarray(pos) * inv[j % nf][None, :]
    first = (j % (ML_DH // 2)) < nf
    cos = jnp.tile(jnp.cos(ang), (1, N_HEADS))
    sin = jnp.tile(jnp.where(first[None, :], -jnp.sin(ang), jnp.sin(ang)), (1, N_HEADS))
    return cos, sin


def _rope(x, cos, sin_signed):
    nf = ML_DH // 4
    first = (lax.broadcasted_iota(jnp.int32, x.shape, 1) % (ML_DH // 2)) < nf
    partner = jnp.where(first, pltpu.roll(x, x.shape[1] - nf, 1), pltpu.roll(x, nf, 1))
    return x * cos + partner * sin_signed


def _mlstm_kernel(latent, seq, *refs):
    if latent:
        (q_ref, k_ref, v_ref, og_ref, aux_ref, rep_ref, sel_ref, brep_ref, bsel_ref, gn_ref, cos_ref, sin_ref,
         c0_ref, n0_ref, m0_ref, y_ref, acc_ref) = refs
    else:
        (q_ref, k_ref, v_ref, og_ref, aux_ref, rep_ref, sel_ref, brep_ref, bsel_ref, gn_ref,
         y_ref, cf_ref, nf_ref, mf_ref, acc_ref) = refs
    c_sz = min(ML_C, seq)
    n_chunks = seq // c_sz
    hw = N_HEADS * ML_DH
    ti = lax.broadcasted_iota(jnp.int32, (c_sz, c_sz), 0)
    tj = lax.broadcasted_iota(jnp.int32, (c_sz, c_sz), 1)
    bd = _block_diag_mask(hw, hw, ML_DH, ML_DH)
    for d in range(2):
        causal = (tj <= ti) if d == 0 else (tj >= ti)
        tri = causal.astype(F32)
        tri_t = ((ti <= tj) if d == 0 else (ti >= tj)).astype(F32)

        def body(n, carry, d=d, causal=causal, tri=tri, tri_t=tri_t):
            cm, nrow, mrow = carry
            c = n if d == 0 else n_chunks - 1 - n
            rows = pl.ds(pl.multiple_of(c * c_sz, c_sz), c_sz)
            q = q_ref[rows, :]
            k = k_ref[rows, :] * (ML_DH ** -0.5)
            if latent:
                q = _rope(q, cos_ref[rows, :], sin_ref[rows, :])
                k = _rope(k, cos_ref[rows, :], sin_ref[rows, :])
            v = v_ref[rows, :].astype(BF16)
            aux = aux_ref[rows, :]
            li_m = jnp.dot(aux, rep_ref[d, 0], precision=HI, preferred_element_type=F32) + brep_ref[d, 0]
            lf_m = jax.nn.log_sigmoid(jnp.dot(aux, rep_ref[d, 1], precision=HI, preferred_element_type=F32)
                                      + brep_ref[d, 1])
            f_m = jnp.dot(tri, lf_m, precision=HI, preferred_element_type=F32)
            g_t = _nt(sel_ref[d], aux, precision=HI) + bsel_ref[d][:, 0:1]
            li_t = g_t[0:N_HEADS, :]
            f_t = jnp.dot(jax.nn.log_sigmoid(g_t[N_HEADS:2 * N_HEADS, :]), tri_t, precision=HI,
                          preferred_element_type=F32)
            dms, fcols, mcols = [], [], []
            for h in range(N_HEADS):
                fcol = f_m[:, h * ML_DH:h * ML_DH + 1]
                dms.append(jnp.where(causal, fcol - f_t[h:h + 1, :] + li_t[h:h + 1, :], -jnp.inf))
                fcols.append(fcol)
                mcols.append(jnp.broadcast_to(mrow[:, h * ML_DH:h * ML_DH + 1], (c_sz, 1)))
            dm = jnp.concatenate(dms, axis=0)
            log_inter = jnp.concatenate(fcols, axis=0) + jnp.concatenate(mcols, axis=0)
            m_t = jnp.maximum(log_inter, jnp.max(dm, axis=1, keepdims=True))
            qs = _stack_heads(q, ML_DH)
            qsb = qs.astype(BF16)
            s = _nt(qsb, k.astype(BF16)) * jnp.exp(dm - m_t)
            a_t = jnp.exp(log_inter - m_t)
            inter = jnp.dot(qsb, cm.astype(BF16), preferred_element_type=F32)
            num = a_t * inter + jnp.dot(s.astype(BF16), v, preferred_element_type=F32)
            den = a_t * jnp.sum(qs * nrow, axis=1, keepdims=True) + jnp.sum(s, axis=1, keepdims=True)
            hst = num / jnp.maximum(jnp.abs(den), jnp.exp(-m_t))
            hout = _unstack_heads(hst, ML_DH)
            if d == 0:
                acc_ref[rows, :] = hout
            else:
                acc_ref[rows, :] += hout
            f_tot = f_m[c_sz - 1:c_sz, :] if d == 0 else f_m[0:1, :]
            w_end = f_tot - f_m + li_m
            m_new = jnp.maximum(f_tot + mrow, jnp.max(w_end, axis=0, keepdims=True))
            a = jnp.exp(f_tot + mrow - m_new)
            kw = k * jnp.exp(w_end - m_new)
            cm_new = cm * a + jnp.where(bd, _tn(kw.astype(BF16), v), 0.0)
            n_new = nrow * a + jnp.sum(kw, axis=0, keepdims=True)
            return cm_new, n_new, m_new

        if latent:
            init = (c0_ref[d], n0_ref[d], m0_ref[d])
        else:
            init = (jnp.zeros((hw, hw), F32), jnp.zeros((1, hw), F32), jnp.zeros((1, hw), F32))
        cm, nrow, mrow = lax.fori_loop(0, n_chunks, body, init)
        if not latent:
            cf_ref[d] = cm
            nf_ref[d] = nrow
            mf_ref[d] = mrow
    gn = gn_ref[...]

    def epi(i, carry):
        rows = pl.ds(pl.multiple_of(i * 256, 256), 256)
        y_ref[rows, :] = _head_rmsnorm(acc_ref[rows, :], ML_DH) * gn * jax.nn.sigmoid(og_ref[rows, :])
        return carry

    lax.fori_loop(0, seq // 256, epi, 0)


def mlstm_mixer(proj, tok0, nbatch, seq, b_gate, norm_g, state):
    latent = state is not None
    sb = tok0 // seq
    hw = N_HEADS * ML_DH
    rep, sel = _ml_gate_selectors()
    brep = jnp.repeat(b_gate.reshape(2, 2, N_HEADS), ML_DH, axis=-1).reshape(2, 2, 1, hw)
    bsel = jnp.broadcast_to(b_gate.reshape(2, 8, 1), (2, 8, LANES))
    col = lambda c0: pl.BlockSpec((seq, 256), lambda b: (sb + b, c0 // 256), pipeline_mode=pl.Buffered(1))
    full = lambda shape: pl.BlockSpec(shape, lambda b: (0,) * len(shape), pipeline_mode=pl.Buffered(1))
    in_specs = [col(P_MQ), col(P_MK), col(P_MV), col(P_MO),
                pl.BlockSpec((seq, 128), lambda b: (sb + b, P_AUX // 128), pipeline_mode=pl.Buffered(1)),
                full((2, 2, LANES, hw)), full((2, 8, LANES)), full((2, 2, 1, hw)), full((2, 8, LANES)), full((1, hw))]
    args = [proj, proj, proj, proj, proj, rep, sel, brep, bsel, norm_g.reshape(1, hw)]
    y_spec = pl.BlockSpec((seq, 256), lambda b: (b, 0))
    y_shape = jax.ShapeDtypeStruct((nbatch * seq, 256), F32)
    c_spec = pl.BlockSpec((None, 2, hw, hw), lambda b: (b, 0, 0, 0))
    r_spec = pl.BlockSpec((None, 2, 1, hw), lambda b: (b, 0, 0, 0))
    if latent:
        cos, sin = _rope_tables(seq)
        in_specs += [full((seq, hw)), full((seq, hw)), c_spec, r_spec, r_spec]
        args += [cos, sin, *state]
        out_specs, out_shape = y_spec, y_shape
    else:
        out_specs = [y_spec, c_spec, r_spec, r_spec]
        out_shape = [y_shape, jax.ShapeDtypeStruct((nbatch, 2, hw, hw), F32),
                     jax.ShapeDtypeStruct((nbatch, 2, 1, hw), F32), jax.ShapeDtypeStruct((nbatch, 2, 1, hw), F32)]
    return pl.pallas_call(
        functools.partial(_mlstm_kernel, latent, seq),
        grid=(nbatch,),
        in_specs=in_specs, out_specs=out_specs, out_shape=out_shape,
        scratch_shapes=[pltpu.VMEM((seq, 256), F32)],
        compiler_params=_cparams(("arbitrary",)),
        name="mlstm_latent" if latent else "mlstm_context",
    )(*args)


HY_CH = 256
HY_BANDS = 16
HY_EMB = 1 + 2 * HY_BANDS
HY_FFN = 64
FFT_N1 = 64
FFT_N2 = 128


def _hy_filter_kernel(feat_ref, w1_ref, b1_ref, w2_ref, b2_ref, w3_ref, b3_ref, fr_ref, dl_ref, o_ref):
    feats = feat_ref[...]
    a = jnp.sin(fr_ref[0:1, :] * (jnp.dot(feats, w1_ref[...], precision=HI, preferred_element_type=F32) + b1_ref[...]))
    a = jnp.sin(fr_ref[1:2, :] * (jnp.dot(a, w2_ref[...], precision=HI, preferred_element_type=F32) + b2_ref[...]))
    a = jnp.dot(a, w3_ref[...], precision=HI, preferred_element_type=F32) + b3_ref[...]
    o_ref[...] = a * jnp.exp(-feats[:, 0:1] * dl_ref[...])


def hyena_filters(seq, w1, b1, w2, b2, w3, b3, freq):
    t = jnp.arange(seq, dtype=F32)
    t_unit = t / (seq - 1)
    bands = jnp.linspace(1e-4, HY_BANDS - 1, HY_BANDS, dtype=F32)
    ang = (2.0 * math.pi / seq) * t[:, None] * bands[None, :]
    feats = jnp.concatenate([t_unit[:, None], jnp.cos(ang), -jnp.sin(ang),
                             jnp.zeros((seq, LANES - HY_EMB), F32)], axis=-1)
    w1p = jnp.zeros((LANES, HY_FFN), F32).at[0:HY_EMB].set(w1)
    deltas = jnp.abs(jnp.linspace(math.log(1e-2) / 1.5, math.log(1e-2) / 0.3, HY_CH, dtype=F32))
    rb = min(seq, 512)
    full = lambda shape: pl.BlockSpec(shape, lambda i: (0,) * len(shape))
    return pl.pallas_call(
        _hy_filter_kernel,
        grid=(seq // rb,),
        in_specs=[pl.BlockSpec((rb, LANES), lambda i: (i, 0)), full((LANES, HY_FFN)), full((1, HY_FFN)),
                  full((HY_FFN, HY_FFN)), full((1, HY_FFN)), full((HY_FFN, 4 * HY_CH)), full((1, 4 * HY_CH)),
                  full((2, HY_FFN)), full((1, 4 * HY_CH))],
        out_specs=pl.BlockSpec((rb, 4 * HY_CH), lambda i: (i, 0)),
        out_shape=jax.ShapeDtypeStruct((seq, 4 * HY_CH), F32),
        compiler_params=_cparams(("arbitrary",)),
        name="hyena_filters",
    )(feats, w1p, b1.reshape(1, -1), w2, b2.reshape(1, -1), w3, b3.reshape(1, -1), freq,
      jnp.tile(deltas, 4).reshape(1, -1))


def _filt2l(g, order):
    h_fwd = g[:, (2 * order) * HY_CH:(2 * order + 1) * HY_CH]
    h_bwd = g[:, (2 * order + 1) * HY_CH:(2 * order + 2) * HY_CH]
    return jnp.concatenate([h_fwd, jnp.zeros_like(h_fwd[:1]), jnp.flip(h_bwd[1:], axis=0)], axis=0)


def _hy_short_kernel(nblk, u_ref, prev_ref, next_ref, w_ref, b_ref, x1_ref, x2_ref, z_ref):
    i = pl.program_id(1)
    u = u_ref[...]
    rb = u.shape[0]
    row = lax.broadcasted_iota(jnp.int32, u.shape, 0)
    prev_row = jnp.where(i > 0, prev_ref[SUBLANES - 1:SUBLANES, :], 0.0)
    next_row = jnp.where(i < nblk - 1, next_ref[0:1, :], 0.0)
    up = jnp.where(row == 0, prev_row, pltpu.roll(u, 1, 0))
    un = jnp.where(row == rb - 1, next_row, pltpu.roll(u, rb - 1, 0))
    y = up * w_ref[0:1, :] + u * w_ref[1:2, :] + un * w_ref[2:3, :] + b_ref[...]
    x1_ref[...] = y[:, 0:HY_CH]
    x2_ref[...] = y[:, HY_CH:2 * HY_CH]
    z_ref[...] = y[:, 2 * HY_CH:3 * HY_CH]


def hyena_short_conv(proj, tok0, nbatch, seq, w, b):
    rb = min(seq, 512)
    nblk = seq // rb
    r0 = tok0 // rb
    h0 = tok0 // SUBLANES
    hpb = rb // SUBLANES
    last = (tok0 + nbatch * seq) // SUBLANES - 1
    o_spec = pl.BlockSpec((rb, HY_CH), lambda bb, i: (bb * nblk + i, 0))
    o_shape = jax.ShapeDtypeStruct((nbatch * seq, HY_CH), F32)
    return pl.pallas_call(
        functools.partial(_hy_short_kernel, nblk),
        grid=(nbatch, nblk),
        in_specs=[pl.BlockSpec((rb, 3 * HY_CH), lambda bb, i: (r0 + bb * nblk + i, P_HU // (3 * HY_CH))),
                  pl.BlockSpec((SUBLANES, 3 * HY_CH),
                               lambda bb, i: (jnp.maximum(h0 + (bb * nblk + i) * hpb - 1, 0), P_HU // (3 * HY_CH))),
                  pl.BlockSpec((SUBLANES, 3 * HY_CH),
                               lambda bb, i: (jnp.minimum(h0 + (bb * nblk + i + 1) * hpb, last), P_HU // (3 * HY_CH))),
                  pl.BlockSpec((3, 3 * HY_CH), lambda bb, i: (0, 0)),
                  pl.BlockSpec((1, 3 * HY_CH), lambda bb, i: (0, 0))],
        out_specs=[o_spec, o_spec, o_spec],
        out_shape=[o_shape, o_shape, o_shape],
        compiler_params=_cparams(("arbitrary", "arbitrary")),
        name="hyena_short_conv",
    )(proj, proj, proj, w, b.reshape(1, -1))


def _dft_consts_single(seq):
    n = 2 * seq
    k = np.arange(n)[:, None].astype(np.float64)
    m = np.arange(n)[None, :].astype(np.float64)
    ang = 2.0 * np.pi * k * m / n
    fwd = np.concatenate([np.cos(ang), -np.sin(ang)], axis=0)
    inv = np.concatenate([np.cos(ang.T[:seq]), -np.sin(ang.T[:seq])], axis=1) / n
    return (jnp.asarray(fwd, F32), jnp.asarray(fwd[:, :seq], F32), jnp.asarray(inv, F32))


def _cmul(zr, zi, hr, hi):
    return zr * hr - zi * hi, zr * hi + zi * hr


def _hy_spec_single_kernel(f_ref, g_ref, o_ref):
    o_ref[...] = jnp.dot(f_ref[...], g_ref[...], precision=HI, preferred_element_type=F32)


def _hy_conv_single_kernel(x1_ref, x2_ref, z_ref, h_ref, bias_ref, f_ref, i_ref, o_ref):
    n = f_ref.shape[0] // 2
    z = z_ref[...]
    for order, xg_ref in enumerate((x1_ref, x2_ref)):
        zz = jnp.dot(f_ref[...], z, precision=HI, preferred_element_type=F32)
        pr, pi = _cmul(zz[0:n], zz[n:2 * n], h_ref[order, 0:n, :], h_ref[order, n:2 * n, :])
        y = jnp.dot(i_ref[...], jnp.concatenate([pr, pi], axis=0), precision=HI, preferred_element_type=F32)
        z = xg_ref[...] * (y + z * bias_ref[order])
    o_ref[...] = z


def hyena_context(x1, x2, z, g, bias, nbatch, seq):
    n = 2 * seq
    f_full, f_half, inv = _dft_consts_single(seq)
    filt = jnp.stack([_filt2l(g, 0), _filt2l(g, 1)], 0)
    spec = pl.pallas_call(
        _hy_spec_single_kernel,
        grid=(2,),
        in_specs=[pl.BlockSpec((2 * n, n), lambda o: (0, 0)), pl.BlockSpec((None, n, HY_CH), lambda o: (o, 0, 0))],
        out_specs=pl.BlockSpec((None, 2 * n, HY_CH), lambda o: (o, 0, 0)),
        out_shape=jax.ShapeDtypeStruct((2, 2 * n, HY_CH), F32),
        compiler_params=_cparams(("arbitrary",)),
        name="hyena_spec_context",
    )(f_full, filt)
    blk = pl.BlockSpec((seq, HY_CH), lambda b: (b, 0))
    return pl.pallas_call(
        _hy_conv_single_kernel,
        grid=(nbatch,),
        in_specs=[blk, blk, blk, pl.BlockSpec((2, 2 * n, HY_CH), lambda b: (0, 0, 0)),
                  pl.BlockSpec((2, 1, HY_CH), lambda b: (0, 0, 0)),
                  pl.BlockSpec((2 * n, seq), lambda b: (0, 0)), pl.BlockSpec((seq, 2 * n), lambda b: (0, 0))],
        out_specs=blk,
        out_shape=jax.ShapeDtypeStruct((nbatch * seq, HY_CH), F32),
        compiler_params=_cparams(("arbitrary",)),
        name="hyena_conv_context",
    )(x1, x2, z, spec, bias.reshape(2, 1, HY_CH), f_half, inv)


def _dft_consts_two_stage():
    n1, n2 = FFT_N1, FFT_N2
    n = n1 * n2
    a2 = np.arange(n2, dtype=np.float64)[:, None, None]
    k1 = np.arange(n1, dtype=np.float64)[None, :, None]
    a1 = np.arange(n1, dtype=np.float64)[None, None, :]
    th = 2.0 * np.pi * (a1 * k1 / n1 + a2 * k1 / n)
    w1 = np.concatenate([np.cos(th), -np.sin(th)], axis=1)
    tht = np.transpose(th, (0, 2, 1))
    w3 = np.concatenate([np.cos(tht), -np.sin(tht)], axis=2) / n
    ph = 2.0 * np.pi * np.arange(n2, dtype=np.float64)[:, None] * np.arange(n2, dtype=np.float64)[None, :] / n2
    c, s = np.cos(ph), np.sin(ph)
    g2 = np.block([[c, s], [-s, c]])
    g2i = np.block([[c, -s], [s, c]])
    return (jnp.asarray(w1, F32), jnp.asarray(w3, F32), jnp.asarray(g2, F32), jnp.asarray(g2i, F32))


def _bmm_kernel(w_ref, x_ref, o_ref):
    o_ref[...] = lax.dot_general(w_ref[...], x_ref[...], (((2,), (1,)), ((0,), (0,))), precision=HI,
                                 preferred_element_type=F32)


def _batched_stage(w, x):
    nbatch, n2, k, ch = x.shape
    m = w.shape[1]
    sb = 16
    return pl.pallas_call(
        _bmm_kernel,
        grid=(nbatch, n2 // sb),
        in_specs=[pl.BlockSpec((sb, m, k), lambda b, j: (j, 0, 0)),
                  pl.BlockSpec((None, sb, k, ch), lambda b, j: (b, j, 0, 0))],
        out_specs=pl.BlockSpec((None, sb, m, ch), lambda b, j: (b, j, 0, 0)),
        out_shape=jax.ShapeDtypeStruct((nbatch, n2, m, ch), F32),
        compiler_params=_cparams(("arbitrary", "arbitrary")),
        name="hyena_dft_outer_stage",
    )(w, x)


def _mid_kernel(conv, *refs):
    if conv:
        x_ref, g2_ref, h_ref, g2i_ref, o_ref = refs
    else:
        x_ref, g2_ref, o_ref = refs
    y = jnp.dot(g2_ref[...], x_ref[...], precision=HI, preferred_element_type=F32)
    if conv:
        n = y.shape[0] // 2
        pr, pi = _cmul(y[0:n], y[n:2 * n], h_ref[0:n, :], h_ref[n:2 * n, :])
        y = jnp.dot(g2i_ref[...], jnp.concatenate([pr, pi], axis=0), precision=HI, preferred_element_type=F32)
    o_ref[...] = y


def _mid_stage(x, g2, spec=None, g2i=None):
    nbatch, nk1, m, ch = x.shape
    conv = spec is not None
    blk = pl.BlockSpec((None, None, m, ch), lambda b, j: (b, j, 0, 0))
    mat = pl.BlockSpec((m, m), lambda b, j: (0, 0))
    in_specs, args = [blk, mat], [x, g2]
    if conv:
        in_specs += [pl.BlockSpec((None, m, ch), lambda b, j: (j, 0, 0)), mat]
        args += [spec, g2i]
    return pl.pallas_call(
        functools.partial(_mid_kernel, conv),
        grid=(nbatch, nk1),
        in_specs=in_specs, out_specs=blk,
        out_shape=jax.ShapeDtypeStruct(x.shape, F32),
        compiler_params=_cparams(("arbitrary", "arbitrary")),
        name="hyena_dft_inner_stage",
    )(*args)


def _gate_kernel(xg_ref, y_ref, z_ref, b_ref, o_ref):
    o_ref[...] = xg_ref[...] * (y_ref[...] + z_ref[...] * b_ref[...])


def _hy_gate(xg, y, z, bias_row):
    t = xg.shape[0]
    rb = 1024
    blk = pl.BlockSpec((rb, HY_CH), lambda i: (i, 0))
    return pl.pallas_call(
        _gate_kernel,
        grid=(t // rb,),
        in_specs=[blk, blk, blk, pl.BlockSpec((1, HY_CH), lambda i: (0, 0))],
        out_specs=blk,
        out_shape=jax.ShapeDtypeStruct((t, HY_CH), F32),
        compiler_params=_cparams(("arbitrary",)),
        name="hyena_gate",
    )(xg, y, z, bias_row)


def _swap_to_inner(a):
    nbatch, n2, m, ch = a.shape
    return a.reshape(nbatch, n2, 2, m // 2, ch).transpose(0, 3, 2, 1, 4).reshape(nbatch, m // 2, 2 * n2, ch)


def hyena_latent(x1, x2, z, g, bias, nbatch, seq):
    n1, n2 = FFT_N1, FFT_N2
    assert 2 * seq == n1 * n2
    w1, w3, g2, g2i = _dft_consts_two_stage()
    half = seq // n2
    filt = jnp.stack([_filt2l(g, 0), _filt2l(g, 1)], 0)
    ft = filt.reshape(2, n1, n2, HY_CH).transpose(0, 2, 1, 3)
    spec = _mid_stage(_swap_to_inner(_batched_stage(w1, ft)), g2)
    w1h = w1[:, :, 0:half]
    w3h = w3[:, 0:half, :]
    cur = z
    for order, xg in enumerate((x1, x2)):
        zt = cur.reshape(nbatch, half, n2, HY_CH).transpose(0, 2, 1, 3)
        a = _swap_to_inner(_batched_stage(w1h, zt))
        q = _swap_to_inner(_mid_stage(a, g2, spec[order], g2i))
        y = _batched_stage(w3h, q).transpose(0, 2, 1, 3).reshape(nbatch * seq, HY_CH)
        cur = _hy_gate(xg, y, cur, bias[order].reshape(1, HY_CH))
    return cur


def _permute_w_in(w):
    sizes = (128, 128, 256, 256, 16, 256, 256, 256, 256, 16, 256, 256, 256, 768)
    offs = np.cumsum((0,) + sizes)
    seg = lambda j: w[:, offs[j]:offs[j + 1]]
    order = (13, 0, 1, 2, 3, 5, 6, 7, 8, 10, 11, 12, 4, 9)
    pad = jnp.zeros((w.shape[0], P_W - P_AUX - 32), w.dtype)
    return jnp.concatenate([seg(j) for j in order] + [pad], axis=1)


def kernel(x_prompt, x_sample, cache_na_k, cache_na_v, state_gla, state_mlstm_C, state_mlstm_n, state_mlstm_m, c, c_ctx, w_ada, b_ada, norm1_g, norm2_g, w_in, w_out, gla_w_gate, gla_b_gate, gla_norm_g, ml_b_gate, ml_norm_g, na_rpb, hy_conv_w, hy_conv_b, hy_w1, hy_b1, hy_w2, hy_b2, hy_w3, hy_b3, hy_freq, hy_bias, router_w, router_b, w_gu, b_gu, w_dn, b_dn, final_norm_g):
    depth = w_ada.shape[0]
    bp, lp, _ = x_prompt.shape
    bs, ls, _ = x_sample.shape
    tp = bp * lp
    assert tp == ls, "modulation rows are selected per block of DEC_SEQ tokens"
    x = jnp.concatenate([x_prompt.reshape(tp, D), x_sample.reshape(bs * ls, D)], axis=0)
    cond8 = jnp.concatenate([c_ctx[None, :], c, jnp.zeros((8 - 1 - bs, D), F32)], axis=0)
    mods_all = ada_mods(cond8, w_ada, b_ada)
    fg = final_norm_g.reshape(1, D)
    eye_h = jnp.eye(N_HEADS, dtype=F32)
    lc = cache_na_k.shape[3]
    new_k, new_v, new_gla, new_c, new_n, new_m = [], [], [], [], [], []
    for l in range(depth):
        mods = mods_all[l].reshape(8, 1, 6 * D)
        proj = in_proj(x, mods, norm1_g[l].reshape(1, D), _permute_w_in(w_in[l]).astype(BF16), ls)
        gla_c, gla_fin = gla_mixer(proj, 0, bp, lp, gla_w_gate[l], gla_b_gate[l], gla_norm_g[l], None)
        s0t = jnp.einsum('bdhkv,hg->bdhvgk', state_gla[:, l], eye_h).reshape(bs, 2, 256, 128)
        gla_s = gla_mixer(proj, tp, bs, ls, gla_w_gate[l], gla_b_gate[l], gla_norm_g[l], s0t)
        ml_c, cf, nf, mf = mlstm_mixer(proj, 0, bp, lp, ml_b_gate[l], ml_norm_g[l], None)
        c0 = jnp.einsum('bdhvk,hg->bdhkgv', state_mlstm_C[:, l], eye_h).reshape(bs, 2, 256, 256)
        n0 = state_mlstm_n[:, l].reshape(bs, 2, 1, 256)
        m0 = jnp.repeat(state_mlstm_m[:, l], ML_DH, axis=-1).reshape(bs, 2, 1, 256)
        ml_s = mlstm_mixer(proj, tp, bs, ls, ml_b_gate[l], ml_norm_g[l], (c0, n0, m0))
        na_c = na_context(proj, bp, lp)
        kct = cache_na_k[:, l].transpose(0, 2, 1, 3).reshape(bs, lc, 256)
        vct = cache_na_v[:, l].transpose(0, 2, 1, 3).reshape(bs, lc, 256)
        na_s = na_latent(proj, tp, bs, ls, kct, vct, na_rpb[l])
        hy_args = (hy_w1[l], hy_b1[l], hy_w2[l], hy_b2[l], hy_w3[l], hy_b3[l], hy_freq[l])
        hy_c = hyena_context(*hyena_short_conv(proj, 0, bp, lp, hy_conv_w[l], hy_conv_b[l]),
                             hyena_filters(lp, *hy_args), hy_bias[l], bp, lp)
        hy_s = hyena_latent(*hyena_short_conv(proj, tp, bs, ls, hy_conv_w[l], hy_conv_b[l]),
                            hyena_filters(ls, *hy_args), hy_bias[l], bs, ls)
        ys = [jnp.concatenate(pair, axis=0) for pair in ((gla_c, gla_s), (ml_c, ml_s), (na_c, na_s), (hy_c, hy_s))]
        x1, h2, eidx, ew, cnt_b = out_proj_route(
            ys, x, mods, norm2_g[l].reshape(1, D), w_out[l].astype(BF16), router_w[l].T,
            jnp.broadcast_to(router_b[l][:, None], (N_EXP, LANES)), ls)
        x = moe_layer(h2, eidx, ew, cnt_b, x1, mods, fg, w_gu[l], b_gu[l], w_dn[l], b_dn[l], ls, l == depth - 1)
        new_k.append(proj[:tp, P_NK:P_NK + 256].reshape(bp, lp, N_HEADS, NA_DH).transpose(0, 2, 1, 3))
        new_v.append(proj[:tp, P_NV:P_NV + 256].reshape(bp, lp, N_HEADS, NA_DH).transpose(0, 2, 1, 3))
        new_gla.append(jnp.einsum('bdhvhk->bdhkv', gla_fin.reshape(bp, 2, N_HEADS, GLA_DV, N_HEADS, GLA_DK)))
        new_c.append(jnp.einsum('bdhkhv->bdhvk', cf.reshape(bp, 2, N_HEADS, ML_DH, N_HEADS, ML_DH)))
        new_n.append(nf.reshape(bp, 2, N_HEADS, ML_DH))
        new_m.append(mf.reshape(bp, 2, N_HEADS, ML_DH)[..., 0])
    y_prompt = x[:tp].reshape(bp, lp, D)
    y_sample = x[tp:].reshape(bs, ls, D)
    return (y_prompt, y_sample, jnp.stack(new_k, 1), jnp.stack(new_v, 1), jnp.stack(new_gla, 1),
            jnp.stack(new_c, 1), jnp.stack(new_n, 1), jnp.stack(new_m, 1))
```

```python
import functools
import math

import numpy as np
import jax
import jax.numpy as jnp
from jax import lax
from jax.experimental import pallas as pl
from jax.experimental.pallas import tpu as pltpu

F32 = jnp.float32
BF16 = jnp.bfloat16
HI = lax.Precision.HIGHEST

LANES = 128
SUBLANES = 8
VMEM_LIMIT = 56 * 1024 * 1024

D = 1024
EPS = 1e-6
N_EXP = 32
TOP_K = 4
D_FF = 1024
SWIGLU_LIMIT = 7.0
SWIGLU_ALPHA = 1.702
GRID_W = 64

TB = 256
ROWS_BS = TB * TOP_K + N_EXP * SUBLANES
CH = SUBLANES
NCH = ROWS_BS // CH
TM = 512

P_HU = 0
P_GQ, P_GK, P_GV, P_GG = 768, 896, 1024, 1280
P_MQ, P_MK, P_MV, P_MO = 1536, 1792, 2048, 2304
P_NQ, P_NK, P_NV = 2560, 2816, 3072
P_AUX = 3328
P_W = 3456


def _cparams(sem=None):
    return pltpu.CompilerParams(dimension_semantics=sem, vmem_limit_bytes=VMEM_LIMIT)


def _ada_kernel(c_ref, w_ref, b_ref, o_ref):
    c = c_ref[...]
    s = c * jax.nn.sigmoid(c)
    o_ref[...] = jnp.dot(s, w_ref[...], precision=HI, preferred_element_type=F32) + b_ref[...]


def ada_mods(cond8, w_ada, b_ada):
    depth = w_ada.shape[0]
    tn = 1536
    return pl.pallas_call(
        _ada_kernel,
        grid=(depth, 6 * D // tn),
        in_specs=[pl.BlockSpec((8, D), lambda l, j: (0, 0)),
                  pl.BlockSpec((None, D, tn), lambda l, j: (l, 0, j)),
                  pl.BlockSpec((None, 1, tn), lambda l, j: (l, 0, j))],
        out_specs=pl.BlockSpec((None, 8, tn), lambda l, j: (l, 0, j)),
        out_shape=jax.ShapeDtypeStruct((depth, 8, 6 * D), F32),
        compiler_params=_cparams(("arbitrary", "arbitrary")),
        name="ada_mods",
    )(cond8, w_ada, b_ada.reshape(depth, 1, 6 * D))


def _rms_mod(x, g, sc, sh):
    ms = jnp.mean(x * x, axis=-1, keepdims=True)
    return (x * lax.rsqrt(ms + EPS) * g) * (1.0 + sc) + sh


def _in_kernel(x_ref, mod_ref, g_ref, w_ref, o_ref):
    h = _rms_mod(x_ref[...], g_ref[...], mod_ref[:, D:2 * D], mod_ref[:, 0:D])
    o_ref[...] = jnp.dot(h.astype(BF16), w_ref[...], preferred_element_type=F32)


def in_proj(x, mods, g, w_bf16, tok_per_mod):
    t = x.shape[0]
    tm = 256
    return pl.pallas_call(
        _in_kernel,
        grid=(t // tm,),
        in_specs=[pl.BlockSpec((tm, D), lambda i: (i, 0)),
                  pl.BlockSpec((None, 1, 6 * D), lambda i: (i * tm // tok_per_mod, 0, 0)),
                  pl.BlockSpec((1, D), lambda i: (0, 0)),
                  pl.BlockSpec((D, P_W), lambda i: (0, 0))],
        out_specs=pl.BlockSpec((tm, P_W), lambda i: (i, 0)),
        out_shape=jax.ShapeDtypeStruct((t, P_W), F32),
        compiler_params=_cparams(("arbitrary",)),
        name="in_proj",
    )(x, mods, g, w_bf16)


def _out_kernel(ya_ref, yb_ref, yc_ref, yd_ref, x_ref, mod_ref, g_ref, w_ref, rw_ref, rb_ref,
                x1_ref, h2_ref, eidx_ref, ew_ref, cnt_ref):
    mix = jnp.dot(ya_ref[...].astype(BF16), w_ref[0:256, :], preferred_element_type=F32)
    mix += jnp.dot(yb_ref[...].astype(BF16), w_ref[256:512, :], preferred_element_type=F32)
    mix += jnp.dot(yc_ref[...].astype(BF16), w_ref[512:768, :], preferred_element_type=F32)
    mix += jnp.dot(yd_ref[...].astype(BF16), w_ref[768:1024, :], preferred_element_type=F32)
    x1 = x_ref[...] + mod_ref[:, 2 * D:3 * D] * mix
    x1_ref[...] = x1
    h2 = _rms_mod(x1, g_ref[...], mod_ref[:, 4 * D:5 * D], mod_ref[:, 3 * D:4 * D])
    h2_ref[...] = h2.astype(BF16)
    lg = lax.dot_general(rw_ref[...], h2, (((1,), (1,)), ((), ())), precision=HI,
                         preferred_element_type=F32) + rb_ref[:, 0:1]
    e_iota = lax.broadcasted_iota(jnp.int32, lg.shape, 0)
    vals, idxs = [], []
    for _ in range(TOP_K):
        m = jnp.max(lg, axis=0, keepdims=True)
        idx = jnp.min(jnp.where(lg == m, e_iota, N_EXP), axis=0, keepdims=True)
        vals.append(m)
        idxs.append(idx)
        lg = jnp.where(e_iota == idx, -jnp.inf, lg)
    ex = [jnp.exp(v - vals[0]) for v in vals]
    den = ex[0] + ex[1] + ex[2] + ex[3]
    eidx_ref[...] = jnp.concatenate(idxs, axis=0)
    ew_ref[...] = jnp.concatenate([e / den for e in ex], axis=0)
    ind = jnp.zeros(lg.shape, F32)
    for idx in idxs:
        ind += (e_iota == idx).astype(F32)
    cnt_ref[...] = jnp.broadcast_to(jnp.sum(ind, axis=1, keepdims=True), (N_EXP, LANES))


def out_proj_route(ys, x, mods, g, w_bf16, rw_t, rb, tok_per_mod):
    t = x.shape[0]
    nb = t // TB
    yspec = pl.BlockSpec((TB, 256), lambda i: (i, 0))
    return pl.pallas_call(
        _out_kernel,
        grid=(nb,),
        in_specs=[yspec, yspec, yspec, yspec,
                  pl.BlockSpec((TB, D), lambda i: (i, 0)),
                  pl.BlockSpec((None, 1, 6 * D), lambda i: (i * TB // tok_per_mod, 0, 0)),
                  pl.BlockSpec((1, D), lambda i: (0, 0)),
                  pl.BlockSpec((D, D), lambda i: (0, 0)),
                  pl.BlockSpec((N_EXP, D), lambda i: (0, 0)),
                  pl.BlockSpec((N_EXP, LANES), lambda i: (0, 0))],
        out_specs=[pl.BlockSpec((TB, D), lambda i: (i, 0)),
                   pl.BlockSpec((TB, D), lambda i: (i, 0)),
                   pl.BlockSpec((None, TOP_K, TB), lambda i: (i, 0, 0)),
                   pl.BlockSpec((None, TOP_K, TB), lambda i: (i, 0, 0)),
                   pl.BlockSpec((None, N_EXP, LANES), lambda i: (i, 0, 0))],
        out_shape=[jax.ShapeDtypeStruct((t, D), F32),
                   jax.ShapeDtypeStruct((t, D), BF16),
                   jax.ShapeDtypeStruct((nb, TOP_K, TB), jnp.int32),
                   jax.ShapeDtypeStruct((nb, TOP_K, TB), F32),
                   jax.ShapeDtypeStruct((nb, N_EXP, LANES), F32)],
        compiler_params=_cparams(("arbitrary",)),
        name="out_proj_route",
    )(*ys, x, mods, g, w_bf16, rw_t, rb)


def moe_tables(cnt, n_tiles):
    nb = cnt.shape[0]
    cnt8 = (cnt + CH - 1) // CH * CH
    ends = jnp.cumsum(cnt8, axis=1)
    off = ends - cnt8
    nchunks = ends[:, -1] // CH
    tot = jnp.sum(cnt8, axis=0)
    totp = (tot + TM - 1) // TM * TM
    eend = jnp.cumsum(totp)
    estart = eend - totp
    gdst = estart[None, :] + jnp.cumsum(cnt8, axis=0) - cnt8
    r = jnp.arange(NCH, dtype=jnp.int32) * CH
    e_of_c = jnp.minimum(jnp.sum((ends[:, None, :] <= r[None, :, None]).astype(jnp.int32), axis=-1), N_EXP - 1)
    pick = e_of_c[:, :, None] == jnp.arange(N_EXP, dtype=jnp.int32)[None, None, :]
    g_of_c = jnp.sum(jnp.where(pick, gdst[:, None, :], 0), axis=-1)
    o_of_c = jnp.sum(jnp.where(pick, off[:, None, :], 0), axis=-1)
    gchunk = (g_of_c + r[None, :] - o_of_c) // CH
    nused = eend[-1] // TM
    ti = jnp.arange(n_tiles, dtype=jnp.int32)
    tile_e = jnp.sum((eend[None, :] // TM <= jnp.minimum(ti, nused - 1)[:, None]).astype(jnp.int32), axis=-1)
    tile_e = jnp.minimum(tile_e, N_EXP - 1)
    return (off, gchunk.reshape(-1).astype(jnp.int32), nchunks.astype(jnp.int32),
            tile_e.astype(jnp.int32), nused.reshape(1).astype(jnp.int32))


def _dest_rows(eidx, off_col):
    e_iota = lax.broadcasted_iota(jnp.int32, (N_EXP, TB), 0)
    ohs = [e_iota == eidx[k:k + 1, :] for k in range(TOP_K)]
    ind = jnp.zeros((N_EXP, TB), F32)
    for oh in ohs:
        ind += oh.astype(F32)
    ti = lax.broadcasted_iota(jnp.int32, (TB, TB), 0)
    tj = lax.broadcasted_iota(jnp.int32, (TB, TB), 1)
    upper = (ti <= tj).astype(BF16)
    rank_incl = jnp.dot(ind.astype(BF16), upper, preferred_element_type=F32)
    base = off_col + rank_incl - ind
    return [jnp.sum(jnp.where(oh, base, 0.0), axis=0, keepdims=True).astype(jnp.int32) for oh in ohs]


def _dispatch_kernel(gchunk_ref, nch_ref, h2_ref, eidx_ref, off_ref, zeros_ref, xs_ref, xbs_ref, sem):
    del zeros_ref
    blk = pl.program_id(0)
    dests = _dest_rows(eidx_ref[...], off_ref[:, 0:1])
    p_iota = lax.broadcasted_iota(jnp.int32, (ROWS_BS, TB), 0)
    perm = jnp.zeros((ROWS_BS, TB), F32)
    for d in dests:
        perm += (p_iota == d).astype(F32)
    slot = blk % 2
    xbs_ref[slot] = jnp.dot(perm.astype(BF16), h2_ref[...], preferred_element_type=F32)

    def copy(b, c):
        dst = pl.multiple_of(gchunk_ref[b * NCH + c] * CH, CH)
        src = pl.multiple_of(c * CH, CH)
        return pltpu.make_async_copy(xbs_ref.at[b % 2, pl.ds(src, CH), :], xs_ref.at[pl.ds(dst, CH), :],
                                     sem.at[b % 2])

    def start(c, carry):
        copy(blk, c).start()
        return carry

    lax.fori_loop(0, nch_ref[blk], start, 0)

    def wait_block(b):
        def wait(c, carry):
            copy(b, c).wait()
            return carry
        lax.fori_loop(0, nch_ref[b], wait, 0)

    @pl.when(blk > 0)
    def _():
        wait_block(blk - 1)

    @pl.when(blk == pl.num_programs(0) - 1)
    def _():
        wait_block(blk)


def moe_dispatch(h2, eidx, off_b, gchunk, nchunks, rows_alloc):
    nb = h2.shape[0] // TB
    return pl.pallas_call(
        _dispatch_kernel,
        grid_spec=pltpu.PrefetchScalarGridSpec(
            num_scalar_prefetch=2,
            grid=(nb,),
            in_specs=[pl.BlockSpec((TB, D), lambda i, *_: (i, 0)),
                      pl.BlockSpec((None, TOP_K, TB), lambda i, *_: (i, 0, 0)),
                      pl.BlockSpec((None, N_EXP, LANES), lambda i, *_: (i, 0, 0)),
                      pl.BlockSpec(memory_space=pl.ANY)],
            out_specs=pl.BlockSpec(memory_space=pl.ANY),
            scratch_shapes=[pltpu.VMEM((2, ROWS_BS, D), F32), pltpu.SemaphoreType.DMA((2,))]),
        out_shape=jax.ShapeDtypeStruct((rows_alloc, D), F32),
        input_output_aliases={5: 0},
        compiler_params=_cparams(("arbitrary",)),
        name="moe_dispatch",
    )(gchunk, nchunks, h2, eidx, off_b, jnp.zeros((rows_alloc, D), F32))


def _expert_kernel(te_ref, nused_ref, x_ref, wgu_ref, bgu_ref, wdn_ref, bdn_ref, y_ref, wgu_bf, wdn_bf):
    i = pl.program_id(0)

    @pl.when(i >= nused_ref[0])
    def _():
        y_ref[...] = jnp.zeros(y_ref.shape, F32)

    @pl.when(i < nused_ref[0])
    def _():
        first = jnp.logical_or(i == 0, te_ref[i] != te_ref[jnp.maximum(i - 1, 0)])

        @pl.when(first)
        def _():
            wgu_bf[...] = wgu_ref[...].astype(BF16)
            wdn_bf[...] = wdn_ref[...].astype(BF16)

        gu = jnp.dot(x_ref[...].astype(BF16), wgu_bf[...], preferred_element_type=F32) + bgu_ref[...]
        g = jnp.minimum(gu[:, 0:D_FF], SWIGLU_LIMIT)
        u = jnp.clip(gu[:, D_FF:2 * D_FF], -SWIGLU_LIMIT, SWIGLU_LIMIT)
        act = g * jax.nn.sigmoid(SWIGLU_ALPHA * g) * (u + 1.0)
        y_ref[...] = jnp.dot(act.astype(BF16), wdn_bf[...], preferred_element_type=F32) + bdn_ref[...]


def moe_experts(xs, tile_e, nused, layer, w_gu, b_gu, w_dn, b_dn):
    n_tiles = xs.shape[0] // TM
    depth = w_gu.shape[0]

    def xmap(i, te, nu):
        return (jnp.minimum(i, jnp.maximum(nu[0] - 1, 0)), 0)

    def wmap(i, te, nu):
        return (layer, te[i], 0, 0)

    return pl.pallas_call(
        _expert_kernel,
        grid_spec=pltpu.PrefetchScalarGridSpec(
            num_scalar_prefetch=2,
            grid=(n_tiles,),
            in_specs=[pl.BlockSpec((TM, D), xmap),
                      pl.BlockSpec((None, None, D, 2 * D_FF), wmap),
                      pl.BlockSpec((None, None, 1, 2 * D_FF), wmap),
                      pl.BlockSpec((None, None, D_FF, D), wmap),
                      pl.BlockSpec((None, None, 1, D), wmap)],
            out_specs=pl.BlockSpec((TM, D), lambda i, te, nu: (i, 0)),
            scratch_shapes=[pltpu.VMEM((D, 2 * D_FF), BF16), pltpu.VMEM((D_FF, D), BF16)]),
        out_shape=jax.ShapeDtypeStruct((xs.shape[0], D), F32),
        compiler_params=_cparams(("arbitrary",)),
        name="moe_experts",
    )(tile_e, nused, xs, w_gu, b_gu.reshape(depth, N_EXP, 1, 2 * D_FF), w_dn, b_dn.reshape(depth, N_EXP, 1, D))


def _combine_kernel(final, gchunk_ref, nch_ref, eidx_ref, ew_ref, off_ref, x1_ref, mod_ref, fg_ref, ys_ref,
                    o_ref, ybs_ref, sem):
    blk = pl.program_id(0)
    n = nch_ref[blk]
    slot = blk % 2

    def copy(b, c):
        src = pl.multiple_of(gchunk_ref[b * NCH + c] * CH, CH)
        dst = pl.multiple_of(c * CH, CH)
        return pltpu.make_async_copy(ys_ref.at[pl.ds(src, CH), :], ybs_ref.at[b % 2, pl.ds(dst, CH), :],
                                     sem.at[b % 2])

    def fetch_block(b):
        def start(c, carry):
            copy(b, c).start()
            return carry
        lax.fori_loop(0, nch_ref[b], start, 0)

    def wait(c, carry):
        copy(blk, c).wait()
        return carry

    @pl.when(blk == 0)
    def _():
        fetch_block(blk)

    @pl.when(blk + 1 < pl.num_programs(0))
    def _():
        fetch_block(blk + 1)

    dests = _dest_rows(eidx_ref[...], off_ref[:, 0:1])
    ew = ew_ref[...]
    p_iota = lax.broadcasted_iota(jnp.int32, (ROWS_BS, TB), 0)
    perm = jnp.zeros((ROWS_BS, TB), F32)
    gsel = jnp.zeros((ROWS_BS, TB), F32)
    for k, d in enumerate(dests):
        hit = p_iota == d
        perm += hit.astype(F32)
        gsel += jnp.where(hit, ew[k:k + 1, :], 0.0)
    gate_col = jnp.sum(gsel, axis=1, keepdims=True)
    lax.fori_loop(0, n, wait, 0)
    row_iota = lax.broadcasted_iota(jnp.int32, (ROWS_BS, 1), 0)
    yb = jnp.where(row_iota < n * CH, ybs_ref[slot], 0.0) * gate_col
    moe = lax.dot_general(perm.astype(BF16), yb.astype(BF16), (((0,), (0,)), ((), ())),
                          preferred_element_type=F32)
    x2 = x1_ref[...] + mod_ref[:, 5 * D:6 * D] * moe
    if final:
        ms = jnp.mean(x2 * x2, axis=-1, keepdims=True)
        x2 = x2 * lax.rsqrt(ms + EPS) * fg_ref[...]
    o_ref[...] = x2


def moe_combine(ys, eidx, ew, off_b, x1, mods, fg, gchunk, nchunks, tok_per_mod, final):
    t = x1.shape[0]
    nb = t // TB
    return pl.pallas_call(
        functools.partial(_combine_kernel, final),
        grid_spec=pltpu.PrefetchScalarGridSpec(
            num_scalar_prefetch=2,
            grid=(nb,),
            in_specs=[pl.BlockSpec((None, TOP_K, TB), lambda i, *_: (i, 0, 0)),
                      pl.BlockSpec((None, TOP_K, TB), lambda i, *_: (i, 0, 0)),
                      pl.BlockSpec((None, N_EXP, LANES), lambda i, *_: (i, 0, 0)),
                      pl.BlockSpec((TB, D), lambda i, *_: (i, 0)),
                      pl.BlockSpec((None, 1, 6 * D), lambda i, *_: (i * TB // tok_per_mod, 0, 0)),
                      pl.BlockSpec((1, D), lambda i, *_: (0, 0)),
                      pl.BlockSpec(memory_space=pl.ANY)],
            out_specs=pl.BlockSpec((TB, D), lambda i, *_: (i, 0)),
            scratch_shapes=[pltpu.VMEM((2, ROWS_BS, D), F32), pltpu.SemaphoreType.DMA((2,))]),
        out_shape=jax.ShapeDtypeStruct((t, D), F32),
        compiler_params=_cparams(("arbitrary",)),
        name="moe_combine",
    )(gchunk, nchunks, eidx, ew, off_b, x1, mods, fg, ys)


def moe_layer(h2, eidx, ew, cnt_b, x1, mods, fg, layer, w_gu, b_gu, w_dn, b_dn, tok_per_mod, final):
    t = h2.shape[0]
    nb = t // TB
    max_rows = t * TOP_K + nb * N_EXP * (CH - 1) + N_EXP * (TM - CH)
    n_tiles = (max_rows + TM - 1) // TM
    cnt = cnt_b[:, :, 0].astype(jnp.int32)
    off, gchunk, nchunks, tile_e, nused = moe_tables(cnt, n_tiles)
    off_b = jnp.broadcast_to(off.astype(F32)[:, :, None], (nb, N_EXP, LANES))
    xs = moe_dispatch(h2, eidx, off_b, gchunk, nchunks, n_tiles * TM)
    ys = moe_experts(xs, tile_e, nused, layer, w_gu, b_gu, w_dn, b_dn)
    return moe_combine(ys, eidx, ew, off_b, x1, mods, fg, gchunk, nchunks, tok_per_mod, final)


N_HEADS = 4


def _stack_heads(x, head_w):
    lane_h = lax.broadcasted_iota(jnp.int32, x.shape, 1) // head_w
    return jnp.concatenate([jnp.where(lane_h == h, x, 0.0) for h in range(N_HEADS)], axis=0)


def _unstack_heads(xs, head_w):
    r = xs.shape[0] // N_HEADS
    lane_h = lax.broadcasted_iota(jnp.int32, (r, xs.shape[1]), 1) // head_w
    out = jnp.zeros((r, xs.shape[1]), F32)
    for h in range(N_HEADS):
        out = jnp.where(lane_h == h, xs[h * r:(h + 1) * r, :], out)
    return out


def _block_diag_mask(rows, cols, rw, cw):
    ri = lax.broadcasted_iota(jnp.int32, (rows, cols), 0) // rw
    ci = lax.broadcasted_iota(jnp.int32, (rows, cols), 1) // cw
    return ri == ci


def _head_rmsnorm(o, head_w):
    n = o.shape[1]
    bd = _block_diag_mask(n, n, head_w, head_w).astype(F32)
    ms = jnp.dot(o * o, bd, precision=HI, preferred_element_type=F32) * (1.0 / head_w)
    return o * lax.rsqrt(ms + EPS)


def _nt(a, b, **kw):
    return lax.dot_general(a, b, (((1,), (1,)), ((), ())), preferred_element_type=F32, **kw)


def _tn(a, b, **kw):
    return lax.dot_general(a, b, (((0,), (0,)), ((), ())), preferred_element_type=F32, **kw)


NA_DH = 64
NA_WIN_R = 8
NA_WIN_C = 16
NA_ROWS = 64


def _na_lat_kernel(q_ref, k_ref, v_ref, kc_ref, vc_ref, bias_ref, o_ref):
    r = pl.program_id(1)
    start = jnp.clip(r - NA_WIN_R // 2, 0, NA_ROWS - NA_WIN_R)
    rows = pl.ds(pl.multiple_of(start * GRID_W, GRID_W), NA_WIN_R * GRID_W)
    qs = _stack_heads(q_ref[...] * (NA_DH ** -0.5), NA_DH).astype(BF16)
    kl = k_ref[rows, :].astype(BF16)
    vl = v_ref[rows, :].astype(BF16)
    s_loc = _nt(qs, kl) + bias_ref[...]
    s_ctx = _nt(qs, kc_ref[...].astype(BF16))
    m = jnp.maximum(jnp.max(s_loc, axis=1, keepdims=True), jnp.max(s_ctx, axis=1, keepdims=True))
    p_loc = jnp.exp(s_loc - m)
    p_ctx = jnp.exp(s_ctx - m)
    den = jnp.sum(p_loc, axis=1, keepdims=True) + jnp.sum(p_ctx, axis=1, keepdims=True)
    o = jnp.dot(p_loc.astype(BF16), vl, preferred_element_type=F32)
    o += jnp.dot(p_ctx.astype(BF16), vc_ref[...].astype(BF16), preferred_element_type=F32)
    o_ref[...] = _unstack_heads(o / den, NA_DH)


def _na_bias_table(rpb):
    col = np.arange(GRID_W)
    c_start = np.clip(col - NA_WIN_C // 2, 0, GRID_W - NA_WIN_C)
    col_mask = (col[None, :] >= c_start[:, None]) & (col[None, :] < c_start[:, None] + NA_WIN_C)
    c_idx = np.clip(col[None, :] - col[:, None], -(NA_WIN_C - 1), NA_WIN_C - 1) + (NA_WIN_C - 1)
    onehot = jnp.asarray(c_idx[None, :, :] == np.arange(2 * NA_WIN_C - 1)[:, None, None], F32)
    tb = jnp.einsum('hrc,cqk->hrqk', rpb, onehot, precision=HI)
    tb = jnp.where(col_mask[None, None], tb, -jnp.inf)
    out = []
    for ri0 in range(NA_WIN_R):
        blk = tb[:, ri0:ri0 + NA_WIN_R]
        out.append(blk.transpose(0, 2, 1, 3).reshape(N_HEADS * GRID_W, NA_WIN_R * GRID_W))
    return jnp.stack(out, 0)


def na_latent(proj, tok0, bs, ls, kc, vc, rpb):
    lc = kc.shape[1]
    bias = _na_bias_table(rpb)
    nrow = ls // GRID_W
    rb0 = tok0 // GRID_W
    sb0 = tok0 // ls
    return pl.pallas_call(
        _na_lat_kernel,
        grid=(bs, nrow),
        in_specs=[pl.BlockSpec((GRID_W, 256), lambda b, r: (rb0 + b * nrow + r, P_NQ // 256)),
                  pl.BlockSpec((ls, 256), lambda b, r: (sb0 + b, P_NK // 256)),
                  pl.BlockSpec((ls, 256), lambda b, r: (sb0 + b, P_NV // 256)),
                  pl.BlockSpec((None, lc, 256), lambda b, r: (b, 0, 0)),
                  pl.BlockSpec((None, lc, 256), lambda b, r: (b, 0, 0)),
                  pl.BlockSpec((None, N_HEADS * GRID_W, NA_WIN_R * GRID_W),
                               lambda b, r: (jnp.clip(r - NA_WIN_R // 2, 0, NA_ROWS - NA_WIN_R) - r + NA_WIN_R - 1, 0, 0))],
        out_specs=pl.BlockSpec((GRID_W, 256), lambda b, r: (b * nrow + r, 0)),
        out_shape=jax.ShapeDtypeStruct((bs * ls, 256), F32),
        compiler_params=_cparams(("arbitrary", "arbitrary")),
        name="na_latent",
    )(proj, proj, proj, kc, vc, bias)


def _na_ctx_kernel(q_ref, k_ref, v_ref, o_ref):
    qs = _stack_heads(q_ref[...] * (NA_DH ** -0.5), NA_DH).astype(BF16)
    s = _nt(qs, k_ref[...].astype(BF16))
    m = jnp.max(s, axis=1, keepdims=True)
    p = jnp.exp(s - m)
    den = jnp.sum(p, axis=1, keepdims=True)
    o = jnp.dot(p.astype(BF16), v_ref[...].astype(BF16), preferred_element_type=F32)
    o_ref[...] = _unstack_heads(o / den, NA_DH)


def na_context(proj, bp, lp):
    return pl.pallas_call(
        _na_ctx_kernel,
        grid=(bp,),
        in_specs=[pl.BlockSpec((lp, 256), lambda b: (b, P_NQ // 256)),
                  pl.BlockSpec((lp, 256), lambda b: (b, P_NK // 256)),
                  pl.BlockSpec((lp, 256), lambda b: (b, P_NV // 256))],
        out_specs=pl.BlockSpec((lp, 256), lambda b: (b, 0)),
        out_shape=jax.ShapeDtypeStruct((bp * lp, 256), F32),
        compiler_params=_cparams(("arbitrary",)),
        name="na_context",
    )(proj, proj, proj)


GLA_DK = 32
GLA_DV = 64
GLA_C = 64
GLA_NORMALIZER = 16.0


def _gla_kernel(has_state, seq, *refs):
    if has_state:
        q_ref, k_ref, v_ref, g_ref, aux_ref, wg_ref, bg_ref, gn_ref, s0_ref, y_ref, acc_ref = refs
    else:
        q_ref, k_ref, v_ref, g_ref, aux_ref, wg_ref, bg_ref, gn_ref, y_ref, sfin_ref, acc_ref = refs
    c_sz = GLA_C
    n_chunks = seq // c_sz
    ti = lax.broadcasted_iota(jnp.int32, (c_sz, c_sz), 0)
    tj = lax.broadcasted_iota(jnp.int32, (c_sz, c_sz), 1)
    ai = lax.broadcasted_iota(jnp.int32, (N_HEADS * c_sz, c_sz), 0) % c_sz
    aj = lax.broadcasted_iota(jnp.int32, (N_HEADS * c_sz, c_sz), 1)
    bd = _block_diag_mask(N_HEADS * GLA_DV, N_HEADS * GLA_DK, GLA_DV, GLA_DK)
    def step(d, c, st):
        tri = ((tj <= ti) if d == 0 else (tj >= ti)).astype(F32)
        amask = (aj <= ai) if d == 0 else (aj >= ai)
        rows = pl.ds(pl.multiple_of(c * c_sz, c_sz), c_sz)
        q = q_ref[rows, :] * (GLA_DK ** -0.5)
        k = k_ref[rows, :]
        v = v_ref[rows, :].astype(BF16)
        la = jax.nn.log_sigmoid(jnp.dot(aux_ref[rows, :], wg_ref[d], precision=HI, preferred_element_type=F32)
                                + bg_ref[d])
        la = la * (1.0 / GLA_NORMALIZER)
        b = jnp.dot(tri, la, precision=HI, preferred_element_type=F32)
        btot = b[c_sz - 1:c_sz, :] if d == 0 else b[0:1, :]
        qt = q * jnp.exp(b)
        kt = (k * jnp.exp(-b)).astype(BF16)
        ke = (k * jnp.exp(btot - b)).astype(BF16)
        a = _nt(_stack_heads(qt, GLA_DK).astype(BF16), kt)
        a = jnp.where(amask, a, 0.0).astype(BF16)
        o = _unstack_heads(jnp.dot(a, v, preferred_element_type=F32), GLA_DV)
        o += _nt(qt.astype(BF16), st.astype(BF16))
        acc_ref[rows, :] += o
        upd = _tn(v, ke)
        return st * jnp.exp(btot) + jnp.where(bd, upd, 0.0)

    def body(n, sts):
        return step(0, n, sts[0]), step(1, n_chunks - 1 - n, sts[1])

    acc_ref[...] = jnp.zeros(acc_ref.shape, F32)
    if has_state:
        st0 = (s0_ref[0], s0_ref[1])
    else:
        st0 = (jnp.zeros((N_HEADS * GLA_DV, N_HEADS * GLA_DK), F32),) * 2
    st_f, st_b = lax.fori_loop(0, n_chunks, body, st0)
    if not has_state:
        sfin_ref[0] = st_f
        sfin_ref[1] = st_b
    gn = gn_ref[...]

    def epi(i, carry):
        rows = pl.ds(pl.multiple_of(i * 256, 256), 256)
        g = g_ref[rows, :]
        y_ref[rows, :] = _head_rmsnorm(acc_ref[rows, :], GLA_DV) * gn * (g * jax.nn.sigmoid(g))
        return carry

    lax.fori_loop(0, seq // 256, epi, 0)


def gla_mixer(proj, tok0, nbatch, seq, w_gate, b_gate, norm_g, s0t):
    has_state = s0t is not None
    sb = tok0 // seq
    wg = jnp.zeros((2, LANES, LANES), F32).at[:, 0:w_gate.shape[1], :].set(w_gate)
    in_specs = [pl.BlockSpec((seq, 128), lambda b: (sb + b, P_GQ // 128)),
                pl.BlockSpec((seq, 128), lambda b: (sb + b, P_GK // 128)),
                pl.BlockSpec((seq, 256), lambda b: (sb + b, P_GV // 256)),
                pl.BlockSpec((seq, 256), lambda b: (sb + b, P_GG // 256)),
                pl.BlockSpec((seq, 128), lambda b: (sb + b, P_AUX // 128)),
                pl.BlockSpec((2, LANES, LANES), lambda b: (0, 0, 0)),
                pl.BlockSpec((2, 1, LANES), lambda b: (0, 0, 0)),
                pl.BlockSpec((1, 256), lambda b: (0, 0))]
    args = [proj, proj, proj, proj, proj, wg, b_gate.reshape(2, 1, LANES), jnp.tile(norm_g, N_HEADS).reshape(1, 256)]
    y_spec = pl.BlockSpec((seq, 256), lambda b: (b, 0))
    y_shape = jax.ShapeDtypeStruct((nbatch * seq, 256), F32)
    st_spec = pl.BlockSpec((None, 2, 256, 128), lambda b: (b, 0, 0, 0))
    if has_state:
        in_specs.append(st_spec)
        args.append(s0t)
        out_specs, out_shape = y_spec, y_shape
    else:
        out_specs = [y_spec, st_spec]
        out_shape = [y_shape, jax.ShapeDtypeStruct((nbatch, 2, 256, 128), F32)]
    return pl.pallas_call(
        functools.partial(_gla_kernel, has_state, seq),
        grid=(nbatch,),
        in_specs=in_specs, out_specs=out_specs, out_shape=out_shape,
        scratch_shapes=[pltpu.VMEM((seq, 256), F32)],
        compiler_params=_cparams(("arbitrary",)),
        name="gla_latent" if has_state else "gla_context",
    )(*args)


ML_DH = 64
ML_C = 256
ROPE_BASE = 10000.0
ML_GATE_LANE0 = 16


def _ml_gate_selectors():
    rep = np.zeros((2, 2, LANES, N_HEADS * ML_DH), np.float32)
    sel = np.zeros((2, 8, LANES), np.float32)
    for d in range(2):
        for g in range(2):
            for h in range(N_HEADS):
                lane = ML_GATE_LANE0 + d * 8 + g * 4 + h
                rep[d, g, lane, h * ML_DH:(h + 1) * ML_DH] = 1.0
                sel[d, g * 4 + h, lane] = 1.0
    return jnp.asarray(rep), jnp.asarray(sel)


def _rope_tables(seq):
    nf = ML_DH // 4
    inv = ROPE_BASE ** (-jnp.arange(nf, dtype=F32) / nf)
    t = np.arange(seq)
    j = np.arange(ML_DH)
    pos = np.where(j[None, :] < ML_DH // 2, (t // GRID_W)[:, None], (t % GRID_W)[:, None]).astype(np.float32)
    ang = jnp.asarray(pos) * inv[j % nf][None, :]
    first = (j % (ML_DH // 2)) < nf
    cos = jnp.tile(jnp.cos(ang), (1, N_HEADS))
    sin = jnp.tile(jnp.where(first[None, :], -jnp.sin(ang), jnp.sin(ang)), (1, N_HEADS))
    return cos, sin


def _rope(x, cos, sin_signed):
    nf = ML_DH // 4
    first = (lax.broadcasted_iota(jnp.int32, x.shape, 1) % (ML_DH // 2)) < nf
    partner = jnp.where(first, pltpu.roll(x, x.shape[1] - nf, 1), pltpu.roll(x, nf, 1))
    return x * cos + partner * sin_signed


def _mlstm_kernel(latent, seq, *refs):
    if latent:
        (q_ref, k_ref, v_ref, og_ref, aux_ref, rep_ref, sel_ref, brep_ref, bsel_ref, gn_ref, cos_ref, sin_ref,
         c0_ref, n0_ref, m0_ref, y_ref, acc_ref) = refs
    else:
        (q_ref, k_ref, v_ref, og_ref, aux_ref, rep_ref, sel_ref, brep_ref, bsel_ref, gn_ref,
         y_ref, cf_ref, nf_ref, mf_ref, acc_ref) = refs
    c_sz = min(ML_C, seq)
    n_chunks = seq // c_sz
    hw = N_HEADS * ML_DH
    ti = lax.broadcasted_iota(jnp.int32, (c_sz, c_sz), 0)
    tj = lax.broadcasted_iota(jnp.int32, (c_sz, c_sz), 1)
    bd = _block_diag_mask(hw, hw, ML_DH, ML_DH)
    for d in range(2):
        causal = (tj <= ti) if d == 0 else (tj >= ti)
        tri = causal.astype(F32)
        tri_t = ((ti <= tj) if d == 0 else (ti >= tj)).astype(F32)

        def body(n, carry, d=d, causal=causal, tri=tri, tri_t=tri_t):
            cm, nrow, mrow = carry
            c = n if d == 0 else n_chunks - 1 - n
            rows = pl.ds(pl.multiple_of(c * c_sz, c_sz), c_sz)
            q = q_ref[rows, :]
            k = k_ref[rows, :] * (ML_DH ** -0.5)
            if latent:
                q = _rope(q, cos_ref[rows, :], sin_ref[rows, :])
                k = _rope(k, cos_ref[rows, :], sin_ref[rows, :])
            v = v_ref[rows, :].astype(BF16)
            aux = aux_ref[rows, :]
            li_m = jnp.dot(aux, rep_ref[d, 0], precision=HI, preferred_element_type=F32) + brep_ref[d, 0]
            lf_m = jax.nn.log_sigmoid(jnp.dot(aux, rep_ref[d, 1], precision=HI, preferred_element_type=F32)
                                      + brep_ref[d, 1])
            f_m = jnp.dot(tri, lf_m, precision=HI, preferred_element_type=F32)
            g_t = _nt(sel_ref[d], aux, precision=HI) + bsel_ref[d][:, 0:1]
            li_t = g_t[0:N_HEADS, :]
            f_t = jnp.dot(jax.nn.log_sigmoid(g_t[N_HEADS:2 * N_HEADS, :]), tri_t, precision=HI,
                          preferred_element_type=F32)
            dms, fcols, mcols = [], [], []
            for h in range(N_HEADS):
                fcol = f_m[:, h * ML_DH:h * ML_DH + 1]
                dms.append(jnp.where(causal, fcol - f_t[h:h + 1, :] + li_t[h:h + 1, :], -jnp.inf))
                fcols.append(fcol)
                mcols.append(jnp.broadcast_to(mrow[:, h * ML_DH:h * ML_DH + 1], (c_sz, 1)))
            dm = jnp.concatenate(dms, axis=0)
            log_inter = jnp.concatenate(fcols, axis=0) + jnp.concatenate(mcols, axis=0)
            m_t = jnp.maximum(log_inter, jnp.max(dm, axis=1, keepdims=True))
            qs = _stack_heads(q, ML_DH)
            qsb = qs.astype(BF16)
            s = _nt(qsb, k.astype(BF16)) * jnp.exp(dm - m_t)
            a_t = jnp.exp(log_inter - m_t)
            inter = jnp.dot(qsb, cm.astype(BF16), preferred_element_type=F32)
            num = a_t * inter + jnp.dot(s.astype(BF16), v, preferred_element_type=F32)
            den = a_t * jnp.sum(qs * nrow, axis=1, keepdims=True) + jnp.sum(s, axis=1, keepdims=True)
            hst = num / jnp.maximum(jnp.abs(den), jnp.exp(-m_t))
            hout = _unstack_heads(hst, ML_DH)
            if d == 0:
                acc_ref[rows, :] = hout
            else:
                acc_ref[rows, :] += hout
            f_tot = f_m[c_sz - 1:c_sz, :] if d == 0 else f_m[0:1, :]
            w_end = f_tot - f_m + li_m
            m_new = jnp.maximum(f_tot + mrow, jnp.max(w_end, axis=0, keepdims=True))
            a = jnp.exp(f_tot + mrow - m_new)
            kw = k * jnp.exp(w_end - m_new)
            cm_new = cm * a + jnp.where(bd, _tn(kw.astype(BF16), v), 0.0)
            n_new = nrow * a + jnp.sum(kw, axis=0, keepdims=True)
            return cm_new, n_new, m_new

        if latent:
            init = (c0_ref[d], n0_ref[d], m0_ref[d])
        else:
            init = (jnp.zeros((hw, hw), F32), jnp.zeros((1, hw), F32), jnp.zeros((1, hw), F32))
        cm, nrow, mrow = lax.fori_loop(0, n_chunks, body, init)
        if not latent:
            cf_ref[d] = cm
            nf_ref[d] = nrow
            mf_ref[d] = mrow
    gn = gn_ref[...]

    def epi(i, carry):
        rows = pl.ds(pl.multiple_of(i * 256, 256), 256)
        y_ref[rows, :] = _head_rmsnorm(acc_ref[rows, :], ML_DH) * gn * jax.nn.sigmoid(og_ref[rows, :])
        return carry

    lax.fori_loop(0, seq // 256, epi, 0)


def mlstm_mixer(proj, tok0, nbatch, seq, b_gate, norm_g, state):
    latent = state is not None
    sb = tok0 // seq
    hw = N_HEADS * ML_DH
    rep, sel = _ml_gate_selectors()
    brep = jnp.repeat(b_gate.reshape(2, 2, N_HEADS), ML_DH, axis=-1).reshape(2, 2, 1, hw)
    bsel = jnp.broadcast_to(b_gate.reshape(2, 8, 1), (2, 8, LANES))
    col = lambda c0: pl.BlockSpec((seq, 256), lambda b: (sb + b, c0 // 256), pipeline_mode=pl.Buffered(1))
    full = lambda shape: pl.BlockSpec(shape, lambda b: (0,) * len(shape), pipeline_mode=pl.Buffered(1))
    in_specs = [col(P_MQ), col(P_MK), col(P_MV), col(P_MO),
                pl.BlockSpec((seq, 128), lambda b: (sb + b, P_AUX // 128), pipeline_mode=pl.Buffered(1)),
                full((2, 2, LANES, hw)), full((2, 8, LANES)), full((2, 2, 1, hw)), full((2, 8, LANES)), full((1, hw))]
    args = [proj, proj, proj, proj, proj, rep, sel, brep, bsel, norm_g.reshape(1, hw)]
    y_spec = pl.BlockSpec((seq, 256), lambda b: (b, 0))
    y_shape = jax.ShapeDtypeStruct((nbatch * seq, 256), F32)
    c_spec = pl.BlockSpec((None, 2, hw, hw), lambda b: (b, 0, 0, 0))
    r_spec = pl.BlockSpec((None, 2, 1, hw), lambda b: (b, 0, 0, 0))
    if latent:
        cos, sin = _rope_tables(seq)
        in_specs += [full((seq, hw)), full((seq, hw)), c_spec, r_spec, r_spec]
        args += [cos, sin, *state]
        out_specs, out_shape = y_spec, y_shape
    else:
        out_specs = [y_spec, c_spec, r_spec, r_spec]
        out_shape = [y_shape, jax.ShapeDtypeStruct((nbatch, 2, hw, hw), F32),
                     jax.ShapeDtypeStruct((nbatch, 2, 1, hw), F32), jax.ShapeDtypeStruct((nbatch, 2, 1, hw), F32)]
    return pl.pallas_call(
        functools.partial(_mlstm_kernel, latent, seq),
        grid=(nbatch,),
        in_specs=in_specs, out_specs=out_specs, out_shape=out_shape,
        scratch_shapes=[pltpu.VMEM((seq, 256), F32)],
        compiler_params=_cparams(("arbitrary",)),
        name="mlstm_latent" if latent else "mlstm_context",
    )(*args)


HY_CH = 256
HY_BANDS = 16
HY_EMB = 1 + 2 * HY_BANDS
HY_FFN = 64
FFT_N1 = 64
FFT_N2 = 128


def _hy_filter_kernel(feat_ref, w1_ref, b1_ref, w2_ref, b2_ref, w3_ref, b3_ref, fr_ref, dl_ref, o_ref):
    feats = feat_ref[...]
    a = jnp.sin(fr_ref[0:1, :] * (jnp.dot(feats, w1_ref[...], precision=HI, preferred_element_type=F32) + b1_ref[...]))
    a = jnp.sin(fr_ref[1:2, :] * (jnp.dot(a, w2_ref[...], precision=HI, preferred_element_type=F32) + b2_ref[...]))
    a = jnp.dot(a, w3_ref[...], precision=HI, preferred_element_type=F32) + b3_ref[...]
    o_ref[...] = a * jnp.exp(-feats[:, 0:1] * dl_ref[...])


def hyena_filters(seq, w1, b1, w2, b2, w3, b3, freq):
    t = jnp.arange(seq, dtype=F32)
    t_unit = t / (seq - 1)
    bands = jnp.linspace(1e-4, HY_BANDS - 1, HY_BANDS, dtype=F32)
    ang = (2.0 * math.pi / seq) * t[:, None] * bands[None, :]
    feats = jnp.concatenate([t_unit[:, None], jnp.cos(ang), -jnp.sin(ang),
                             jnp.zeros((seq, LANES - HY_EMB), F32)], axis=-1)
    w1p = jnp.zeros((LANES, HY_FFN), F32).at[0:HY_EMB].set(w1)
    deltas = jnp.abs(jnp.linspace(math.log(1e-2) / 1.5, math.log(1e-2) / 0.3, HY_CH, dtype=F32))
    rb = min(seq, 512)
    full = lambda shape: pl.BlockSpec(shape, lambda i: (0,) * len(shape))
    return pl.pallas_call(
        _hy_filter_kernel,
        grid=(seq // rb,),
        in_specs=[pl.BlockSpec((rb, LANES), lambda i: (i, 0)), full((LANES, HY_FFN)), full((1, HY_FFN)),
                  full((HY_FFN, HY_FFN)), full((1, HY_FFN)), full((HY_FFN, 4 * HY_CH)), full((1, 4 * HY_CH)),
                  full((2, HY_FFN)), full((1, 4 * HY_CH))],
        out_specs=pl.BlockSpec((rb, 4 * HY_CH), lambda i: (i, 0)),
        out_shape=jax.ShapeDtypeStruct((seq, 4 * HY_CH), F32),
        compiler_params=_cparams(("arbitrary",)),
        name="hyena_filters",
    )(feats, w1p, b1.reshape(1, -1), w2, b2.reshape(1, -1), w3, b3.reshape(1, -1), freq,
      jnp.tile(deltas, 4).reshape(1, -1))


def _filt2l(g, order):
    h_fwd = g[:, (2 * order) * HY_CH:(2 * order + 1) * HY_CH]
    h_bwd = g[:, (2 * order + 1) * HY_CH:(2 * order + 2) * HY_CH]
    return jnp.concatenate([h_fwd, jnp.zeros_like(h_fwd[:1]), jnp.flip(h_bwd[1:], axis=0)], axis=0)


def _hy_short_kernel(nblk, u_ref, prev_ref, next_ref, w_ref, b_ref, x1_ref, x2_ref, z_ref):
    i = pl.program_id(1)
    u = u_ref[...]
    rb = u.shape[0]
    row = lax.broadcasted_iota(jnp.int32, u.shape, 0)
    prev_row = jnp.where(i > 0, prev_ref[SUBLANES - 1:SUBLANES, :], 0.0)
    next_row = jnp.where(i < nblk - 1, next_ref[0:1, :], 0.0)
    up = jnp.where(row == 0, prev_row, pltpu.roll(u, 1, 0))
    un = jnp.where(row == rb - 1, next_row, pltpu.roll(u, rb - 1, 0))
    y = up * w_ref[0:1, :] + u * w_ref[1:2, :] + un * w_ref[2:3, :] + b_ref[...]
    x1_ref[...] = y[:, 0:HY_CH]
    x2_ref[...] = y[:, HY_CH:2 * HY_CH]
    z_ref[...] = y[:, 2 * HY_CH:3 * HY_CH]


def hyena_short_conv(proj, tok0, nbatch, seq, w, b):
    rb = min(seq, 512)
    nblk = seq // rb
    r0 = tok0 // rb
    h0 = tok0 // SUBLANES
    hpb = rb // SUBLANES
    last = (tok0 + nbatch * seq) // SUBLANES - 1
    o_spec = pl.BlockSpec((rb, HY_CH), lambda bb, i: (bb * nblk + i, 0))
    o_shape = jax.ShapeDtypeStruct((nbatch * seq, HY_CH), F32)
    return pl.pallas_call(
        functools.partial(_hy_short_kernel, nblk),
        grid=(nbatch, nblk),
        in_specs=[pl.BlockSpec((rb, 3 * HY_CH), lambda bb, i: (r0 + bb * nblk + i, P_HU // (3 * HY_CH))),
                  pl.BlockSpec((SUBLANES, 3 * HY_CH),
                               lambda bb, i: (jnp.maximum(h0 + (bb * nblk + i) * hpb - 1, 0), P_HU // (3 * HY_CH))),
                  pl.BlockSpec((SUBLANES, 3 * HY_CH),
                               lambda bb, i: (jnp.minimum(h0 + (bb * nblk + i + 1) * hpb, last), P_HU // (3 * HY_CH))),
                  pl.BlockSpec((3, 3 * HY_CH), lambda bb, i: (0, 0)),
                  pl.BlockSpec((1, 3 * HY_CH), lambda bb, i: (0, 0))],
        out_specs=[o_spec, o_spec, o_spec],
        out_shape=[o_shape, o_shape, o_shape],
        compiler_params=_cparams(("arbitrary", "arbitrary")),
        name="hyena_short_conv",
    )(proj, proj, proj, w, b.reshape(1, -1))


def _dft_consts_single(seq):
    n = 2 * seq
    k = np.arange(n)[:, None].astype(np.float64)
    m = np.arange(n)[None, :].astype(np.float64)
    ang = 2.0 * np.pi * k * m / n
    fwd = np.concatenate([np.cos(ang), -np.sin(ang)], axis=0)
    inv = np.concatenate([np.cos(ang.T[:seq]), -np.sin(ang.T[:seq])], axis=1) / n
    return (jnp.asarray(fwd, F32), jnp.asarray(fwd[:, :seq], F32), jnp.asarray(inv, F32))


def _cmul(zr, zi, hr, hi):
    return zr * hr - zi * hi, zr * hi + zi * hr


def _split_bf16(a):
    hi = a.astype(BF16)
    return hi, (a - hi.astype(F32)).astype(BF16)


def _dot3(a, b, dims=None):
    if dims is None:
        dims = (((a.ndim - 1,), (0,)), ((), ()))
    a_hi, a_lo = _split_bf16(a)
    b_hi, b_lo = _split_bf16(b)
    dg = functools.partial(lax.dot_general, dimension_numbers=dims, preferred_element_type=F32)
    return dg(a_hi, b_hi) + dg(a_lo, b_hi) + dg(a_hi, b_lo)


def _hy_spec_single_kernel(f_ref, g_ref, o_ref):
    o_ref[...] = _dot3(f_ref[...], g_ref[...])


def _hy_conv_single_kernel(x1_ref, x2_ref, z_ref, h_ref, bias_ref, f_ref, i_ref, o_ref):
    n = f_ref.shape[0] // 2
    z = z_ref[...]
    for order, xg_ref in enumerate((x1_ref, x2_ref)):
        zz = _dot3(f_ref[...], z)
        pr, pi = _cmul(zz[0:n], zz[n:2 * n], h_ref[order, 0:n, :], h_ref[order, n:2 * n, :])
        y = _dot3(i_ref[...], jnp.concatenate([pr, pi], axis=0))
        z = xg_ref[...] * (y + z * bias_ref[order])
    o_ref[...] = z


def hyena_context(x1, x2, z, g, bias, nbatch, seq):
    n = 2 * seq
    f_full, f_half, inv = _dft_consts_single(seq)
    filt = jnp.stack([_filt2l(g, 0), _filt2l(g, 1)], 0)
    spec = pl.pallas_call(
        _hy_spec_single_kernel,
        grid=(2,),
        in_specs=[pl.BlockSpec((2 * n, n), lambda o: (0, 0)), pl.BlockSpec((None, n, HY_CH), lambda o: (o, 0, 0))],
        out_specs=pl.BlockSpec((None, 2 * n, HY_CH), lambda o: (o, 0, 0)),
        out_shape=jax.ShapeDtypeStruct((2, 2 * n, HY_CH), F32),
        compiler_params=_cparams(("arbitrary",)),
        name="hyena_spec_context",
    )(f_full, filt)
    blk = pl.BlockSpec((seq, HY_CH), lambda b: (b, 0))
    return pl.pallas_call(
        _hy_conv_single_kernel,
        grid=(nbatch,),
        in_specs=[blk, blk, blk, pl.BlockSpec((2, 2 * n, HY_CH), lambda b: (0, 0, 0)),
                  pl.BlockSpec((2, 1, HY_CH), lambda b: (0, 0, 0)),
                  pl.BlockSpec((2 * n, seq), lambda b: (0, 0)), pl.BlockSpec((seq, 2 * n), lambda b: (0, 0))],
        out_specs=blk,
        out_shape=jax.ShapeDtypeStruct((nbatch * seq, HY_CH), F32),
        compiler_params=_cparams(("arbitrary",)),
        name="hyena_conv_context",
    )(x1, x2, z, spec, bias.reshape(2, 1, HY_CH), f_half, inv)


def _dft_consts_two_stage():
    n1, n2 = FFT_N1, FFT_N2
    n = n1 * n2
    a2 = np.arange(n2, dtype=np.float64)[:, None, None]
    k1 = np.arange(n1, dtype=np.float64)[None, :, None]
    a1 = np.arange(n1, dtype=np.float64)[None, None, :]
    th = 2.0 * np.pi * (a1 * k1 / n1 + a2 * k1 / n)
    w1 = np.concatenate([np.cos(th), -np.sin(th)], axis=1)
    tht = np.transpose(th, (0, 2, 1))
    w3 = np.concatenate([np.cos(tht), -np.sin(tht)], axis=2) / n
    ph = 2.0 * np.pi * np.arange(n2, dtype=np.float64)[:, None] * np.arange(n2, dtype=np.float64)[None, :] / n2
    c, s = np.cos(ph), np.sin(ph)
    g2 = np.block([[c, s], [-s, c]])
    g2i = np.block([[c, -s], [s, c]])
    return (jnp.asarray(w1, F32), jnp.asarray(w3, F32), jnp.asarray(g2, F32), jnp.asarray(g2i, F32))


def _bmm_kernel(w_ref, x_ref, o_ref):
    o_ref[...] = _dot3(w_ref[...], x_ref[...], (((2,), (1,)), ((0,), (0,))))


def _batched_stage(w, x):
    nbatch, n2, k, ch = x.shape
    m = w.shape[1]
    sb = 16
    return pl.pallas_call(
        _bmm_kernel,
        grid=(nbatch, n2 // sb),
        in_specs=[pl.BlockSpec((sb, m, k), lambda b, j: (j, 0, 0)),
                  pl.BlockSpec((None, sb, k, ch), lambda b, j: (b, j, 0, 0))],
        out_specs=pl.BlockSpec((None, sb, m, ch), lambda b, j: (b, j, 0, 0)),
        out_shape=jax.ShapeDtypeStruct((nbatch, n2, m, ch), F32),
        compiler_params=_cparams(("arbitrary", "arbitrary")),
        name="hyena_dft_outer_stage",
    )(w, x)


def _mid_kernel(conv, *refs):
    if conv:
        x_ref, g2_ref, h_ref, g2i_ref, o_ref = refs
    else:
        x_ref, g2_ref, o_ref = refs
    y = _dot3(g2_ref[...], x_ref[...])
    if conv:
        n = y.shape[0] // 2
        pr, pi = _cmul(y[0:n], y[n:2 * n], h_ref[0:n, :], h_ref[n:2 * n, :])
        y = _dot3(g2i_ref[...], jnp.concatenate([pr, pi], axis=0))
    o_ref[...] = y


def _mid_stage(x, g2, spec=None, g2i=None):
    nbatch, nk1, m, ch = x.shape
    conv = spec is not None
    blk = pl.BlockSpec((None, None, m, ch), lambda b, j: (b, j, 0, 0))
    mat = pl.BlockSpec((m, m), lambda b, j: (0, 0))
    in_specs, args = [blk, mat], [x, g2]
    if conv:
        in_specs += [pl.BlockSpec((None, m, ch), lambda b, j: (j, 0, 0)), mat]
        args += [spec, g2i]
    return pl.pallas_call(
        functools.partial(_mid_kernel, conv),
        grid=(nbatch, nk1),
        in_specs=in_specs, out_specs=blk,
        out_shape=jax.ShapeDtypeStruct(x.shape, F32),
        compiler_params=_cparams(("arbitrary", "arbitrary")),
        name="hyena_dft_inner_stage",
    )(*args)


def _gate_kernel(xg_ref, y_ref, z_ref, b_ref, o_ref):
    o_ref[...] = xg_ref[...] * (y_ref[...] + z_ref[...] * b_ref[...])


def _hy_gate(xg, y, z, bias_row):
    t = xg.shape[0]
    rb = 1024
    blk = pl.BlockSpec((rb, HY_CH), lambda i: (i, 0))
    return pl.pallas_call(
        _gate_kernel,
        grid=(t // rb,),
        in_specs=[blk, blk, blk, pl.BlockSpec((1, HY_CH), lambda i: (0, 0))],
        out_specs=blk,
        out_shape=jax.ShapeDtypeStruct((t, HY_CH), F32),
        compiler_params=_cparams(("arbitrary",)),
        name="hyena_gate",
    )(xg, y, z, bias_row)


def _swap_to_inner(a):
    nbatch, n2, m, ch = a.shape
    return a.reshape(nbatch, n2, 2, m // 2, ch).transpose(0, 3, 2, 1, 4).reshape(nbatch, m // 2, 2 * n2, ch)


def hyena_latent(x1, x2, z, g, bias, nbatch, seq):
    n1, n2 = FFT_N1, FFT_N2
    assert 2 * seq == n1 * n2
    w1, w3, g2, g2i = _dft_consts_two_stage()
    half = seq // n2
    filt = jnp.stack([_filt2l(g, 0), _filt2l(g, 1)], 0)
    ft = filt.reshape(2, n1, n2, HY_CH).transpose(0, 2, 1, 3)
    spec = _mid_stage(_swap_to_inner(_batched_stage(w1, ft)), g2)
    w1h = w1[:, :, 0:half]
    w3h = w3[:, 0:half, :]
    cur = z
    for order, xg in enumerate((x1, x2)):
        zt = cur.reshape(nbatch, half, n2, HY_CH).transpose(0, 2, 1, 3)
        a = _swap_to_inner(_batched_stage(w1h, zt))
        q = _swap_to_inner(_mid_stage(a, g2, spec[order], g2i))
        y = _batched_stage(w3h, q).transpose(0, 2, 1, 3).reshape(nbatch * seq, HY_CH)
        cur = _hy_gate(xg, y, cur, bias[order].reshape(1, HY_CH))
    return cur


def _permute_w_in(w):
    sizes = (128, 128, 256, 256, 16, 256, 256, 256, 256, 16, 256, 256, 256, 768)
    offs = np.cumsum((0,) + sizes)
    seg = lambda j: w[:, offs[j]:offs[j + 1]]
    order = (13, 0, 1, 2, 3, 5, 6, 7, 8, 10, 11, 12, 4, 9)
    pad = jnp.zeros((w.shape[0], P_W - P_AUX - 32), w.dtype)
    return jnp.concatenate([seg(j) for j in order] + [pad], axis=1)


def kernel(x_prompt, x_sample, cache_na_k, cache_na_v, state_gla, state_mlstm_C, state_mlstm_n, state_mlstm_m, c, c_ctx, w_ada, b_ada, norm1_g, norm2_g, w_in, w_out, gla_w_gate, gla_b_gate, gla_norm_g, ml_b_gate, ml_norm_g, na_rpb, hy_conv_w, hy_conv_b, hy_w1, hy_b1, hy_w2, hy_b2, hy_w3, hy_b3, hy_freq, hy_bias, router_w, router_b, w_gu, b_gu, w_dn, b_dn, final_norm_g):
    depth = w_ada.shape[0]
    bp, lp, _ = x_prompt.shape
    bs, ls, _ = x_sample.shape
    tp = bp * lp
    assert tp == ls, "modulation rows are selected per block of DEC_SEQ tokens"
    x = jnp.concatenate([x_prompt.reshape(tp, D), x_sample.reshape(bs * ls, D)], axis=0)
    cond8 = jnp.concatenate([c_ctx[None, :], c, jnp.zeros((8 - 1 - bs, D), F32)], axis=0)
    mods_all = ada_mods(cond8, w_ada, b_ada)
    fg = final_norm_g.reshape(1, D)
    eye_h = jnp.eye(N_HEADS, dtype=F32)
    lc = cache_na_k.shape[3]
    new_k, new_v, new_gla, new_c, new_n, new_m = [], [], [], [], [], []
    for l in range(depth):
        mods = mods_all[l].reshape(8, 1, 6 * D)
        proj = in_proj(x, mods, norm1_g[l].reshape(1, D), _permute_w_in(w_in[l]).astype(BF16), ls)
        gla_c, gla_fin = gla_mixer(proj, 0, bp, lp, gla_w_gate[l], gla_b_gate[l], gla_norm_g[l], None)
        s0t = jnp.einsum('bdhkv,hg->bdhvgk', state_gla[:, l], eye_h).reshape(bs, 2, 256, 128)
        gla_s = gla_mixer(proj, tp, bs, ls, gla_w_gate[l], gla_b_gate[l], gla_norm_g[l], s0t)
        ml_c, cf, nf, mf = mlstm_mixer(proj, 0, bp, lp, ml_b_gate[l], ml_norm_g[l], None)
        c0 = jnp.einsum('bdhvk,hg->bdhkgv', state_mlstm_C[:, l], eye_h).reshape(bs, 2, 256, 256)
        n0 = state_mlstm_n[:, l].reshape(bs, 2, 1, 256)
        m0 = jnp.repeat(state_mlstm_m[:, l], ML_DH, axis=-1).reshape(bs, 2, 1, 256)
        ml_s = mlstm_mixer(proj, tp, bs, ls, ml_b_gate[l], ml_norm_g[l], (c0, n0, m0))
        na_c = na_context(proj, bp, lp)
        kct = cache_na_k[:, l].transpose(0, 2, 1, 3).reshape(bs, lc, 256)
        vct = cache_na_v[:, l].transpose(0, 2, 1, 3).reshape(bs, lc, 256)
        na_s = na_latent(proj, tp, bs, ls, kct, vct, na_rpb[l])
        hy_args = (hy_w1[l], hy_b1[l], hy_w2[l], hy_b2[l], hy_w3[l], hy_b3[l], hy_freq[l])
        hy_c = hyena_context(*hyena_short_conv(proj, 0, bp, lp, hy_conv_w[l], hy_conv_b[l]),
                             hyena_filters(lp, *hy_args), hy_bias[l], bp, lp)
        hy_s = hyena_latent(*hyena_short_conv(proj, tp, bs, ls, hy_conv_w[l], hy_conv_b[l]),
                            hyena_filters(ls, *hy_args), hy_bias[l], bs, ls)
        ys = [jnp.concatenate(pair, axis=0) for pair in ((gla_c, gla_s), (ml_c, ml_s), (na_c, na_s), (hy_c, hy_s))]
        x1, h2, eidx, ew, cnt_b = out_proj_route(
            ys, x, mods, norm2_g[l].reshape(1, D), w_out[l].astype(BF16), router_w[l].T,
            jnp.broadcast_to(router_b[l][:, None], (N_EXP, LANES)), ls)
        x = moe_layer(h2, eidx, ew, cnt_b, x1, mods, fg, l, w_gu, b_gu, w_dn, b_dn, ls, l == depth - 1)
        new_k.append(proj[:tp, P_NK:P_NK + 256].reshape(bp, lp, N_HEADS, NA_DH).transpose(0, 2, 1, 3))
        new_v.append(proj[:tp, P_NV:P_NV + 256].reshape(bp, lp, N_HEADS, NA_DH).transpose(0, 2, 1, 3))
        gla_blocks = [gla_fin[:, :, h * GLA_DV:(h + 1) * GLA_DV, h * GLA_DK:(h + 1) * GLA_DK] for h in range(N_HEADS)]
        new_gla.append(jnp.stack(gla_blocks, 2).transpose(0, 1, 2, 4, 3))
        c_blocks = [cf[:, :, h * ML_DH:(h + 1) * ML_DH, h * ML_DH:(h + 1) * ML_DH] for h in range(N_HEADS)]
        new_c.append(jnp.stack(c_blocks, 2).transpose(0, 1, 2, 4, 3))
        new_n.append(nf.reshape(bp, 2, N_HEADS, ML_DH))
        new_m.append(mf.reshape(bp, 2, N_HEADS, ML_DH)[..., 0])
    y_prompt = x[:tp].reshape(bp, lp, D)
    y_sample = x[tp:].reshape(bs, ls, D)
    return (y_prompt, y_sample, jnp.stack(new_k, 1), jnp.stack(new_v, 1), jnp.stack(new_gla, 1),
            jnp.stack(new_c, 1), jnp.stack(new_n, 1), jnp.stack(new_m, 1))
```

```python
import functools
import math

import numpy as np
import jax
import jax.numpy as jnp
from jax import lax
from jax.experimental import pallas as pl
from jax.experimental.pallas import tpu as pltpu

F32 = jnp.float32
BF16 = jnp.bfloat16
HI = lax.Precision.HIGHEST

LANES = 128
SUBLANES = 8
VMEM_LIMIT = 56 * 1024 * 1024

D = 1024
EPS = 1e-6
N_EXP = 32
TOP_K = 4
D_FF = 1024
SWIGLU_LIMIT = 7.0
SWIGLU_ALPHA = 1.702
GRID_W = 64

TB = 256
ROWS_BS = TB * TOP_K + N_EXP * SUBLANES
CH = SUBLANES
NCH = ROWS_BS // CH
TM = 512

P_HU = 0
P_GQ, P_GK, P_GV, P_GG = 768, 896, 1024, 1280
P_MQ, P_MK, P_MV, P_MO = 1536, 1792, 2048, 2304
P_NQ, P_NK, P_NV = 2560, 2816, 3072
P_AUX = 3328
P_W = 3456


def _cparams(sem=None):
    return pltpu.CompilerParams(dimension_semantics=sem, vmem_limit_bytes=VMEM_LIMIT)


def _ada_kernel(c_ref, w_ref, b_ref, o_ref):
    c = c_ref[...]
    s = c * jax.nn.sigmoid(c)
    o_ref[...] = jnp.dot(s, w_ref[...], precision=HI, preferred_element_type=F32) + b_ref[...]


def ada_mods(cond8, w_ada, b_ada):
    depth = w_ada.shape[0]
    tn = 1536
    return pl.pallas_call(
        _ada_kernel,
        grid=(depth, 6 * D // tn),
        in_specs=[pl.BlockSpec((8, D), lambda l, j: (0, 0)),
                  pl.BlockSpec((None, D, tn), lambda l, j: (l, 0, j)),
                  pl.BlockSpec((None, 1, tn), lambda l, j: (l, 0, j))],
        out_specs=pl.BlockSpec((None, 8, tn), lambda l, j: (l, 0, j)),
        out_shape=jax.ShapeDtypeStruct((depth, 8, 6 * D), F32),
        compiler_params=_cparams(("arbitrary", "arbitrary")),
        name="ada_mods",
    )(cond8, w_ada, b_ada.reshape(depth, 1, 6 * D))


def _rms_mod(x, g, sc, sh):
    ms = jnp.mean(x * x, axis=-1, keepdims=True)
    return (x * lax.rsqrt(ms + EPS) * g) * (1.0 + sc) + sh


def _in_kernel(x_ref, mod_ref, g_ref, w_ref, o_ref):
    h = _rms_mod(x_ref[...], g_ref[...], mod_ref[:, D:2 * D], mod_ref[:, 0:D])
    o_ref[...] = jnp.dot(h.astype(BF16), w_ref[...], preferred_element_type=F32)


def in_proj(x, mods, g, w_bf16, tok_per_mod):
    t = x.shape[0]
    tm = 256
    return pl.pallas_call(
        _in_kernel,
        grid=(t // tm,),
        in_specs=[pl.BlockSpec((tm, D), lambda i: (i, 0)),
                  pl.BlockSpec((None, 1, 6 * D), lambda i: (i * tm // tok_per_mod, 0, 0)),
                  pl.BlockSpec((1, D), lambda i: (0, 0)),
                  pl.BlockSpec((D, P_W), lambda i: (0, 0))],
        out_specs=pl.BlockSpec((tm, P_W), lambda i: (i, 0)),
        out_shape=jax.ShapeDtypeStruct((t, P_W), F32),
        compiler_params=_cparams(("arbitrary",)),
        name="in_proj",
    )(x, mods, g, w_bf16)


def _out_kernel(ya_ref, yb_ref, yc_ref, yd_ref, x_ref, mod_ref, g_ref, w_ref, rw_ref, rb_ref,
                x1_ref, h2_ref, eidx_ref, ew_ref, cnt_ref):
    mix = jnp.dot(ya_ref[...].astype(BF16), w_ref[0:256, :], preferred_element_type=F32)
    mix += jnp.dot(yb_ref[...].astype(BF16), w_ref[256:512, :], preferred_element_type=F32)
    mix += jnp.dot(yc_ref[...].astype(BF16), w_ref[512:768, :], preferred_element_type=F32)
    mix += jnp.dot(yd_ref[...].astype(BF16), w_ref[768:1024, :], preferred_element_type=F32)
    x1 = x_ref[...] + mod_ref[:, 2 * D:3 * D] * mix
    x1_ref[...] = x1
    h2 = _rms_mod(x1, g_ref[...], mod_ref[:, 4 * D:5 * D], mod_ref[:, 3 * D:4 * D])
    h2_ref[...] = h2.astype(BF16)
    lg = lax.dot_general(rw_ref[...], h2, (((1,), (1,)), ((), ())), precision=HI,
                         preferred_element_type=F32) + rb_ref[:, 0:1]
    e_iota = lax.broadcasted_iota(jnp.int32, lg.shape, 0)
    vals, idxs = [], []
    for _ in range(TOP_K):
        m = jnp.max(lg, axis=0, keepdims=True)
        idx = jnp.min(jnp.where(lg == m, e_iota, N_EXP), axis=0, keepdims=True)
        vals.append(m)
        idxs.append(idx)
        lg = jnp.where(e_iota == idx, -jnp.inf, lg)
    ex = [jnp.exp(v - vals[0]) for v in vals]
    den = ex[0] + ex[1] + ex[2] + ex[3]
    eidx_ref[...] = jnp.concatenate(idxs, axis=0)
    ew_ref[...] = jnp.concatenate([e / den for e in ex], axis=0)
    ind = jnp.zeros(lg.shape, F32)
    for idx in idxs:
        ind += (e_iota == idx).astype(F32)
    cnt_ref[...] = jnp.broadcast_to(jnp.sum(ind, axis=1, keepdims=True), (N_EXP, LANES))


def out_proj_route(ys, x, mods, g, w_bf16, rw_t, rb, tok_per_mod):
    t = x.shape[0]
    nb = t // TB
    yspec = pl.BlockSpec((TB, 256), lambda i: (i, 0))
    return pl.pallas_call(
        _out_kernel,
        grid=(nb,),
        in_specs=[yspec, yspec, yspec, yspec,
                  pl.BlockSpec((TB, D), lambda i: (i, 0)),
                  pl.BlockSpec((None, 1, 6 * D), lambda i: (i * TB // tok_per_mod, 0, 0)),
                  pl.BlockSpec((1, D), lambda i: (0, 0)),
                  pl.BlockSpec((D, D), lambda i: (0, 0)),
                  pl.BlockSpec((N_EXP, D), lambda i: (0, 0)),
                  pl.BlockSpec((N_EXP, LANES), lambda i: (0, 0))],
        out_specs=[pl.BlockSpec((TB, D), lambda i: (i, 0)),
                   pl.BlockSpec((TB, D), lambda i: (i, 0)),
                   pl.BlockSpec((None, TOP_K, TB), lambda i: (i, 0, 0)),
                   pl.BlockSpec((None, TOP_K, TB), lambda i: (i, 0, 0)),
                   pl.BlockSpec((None, N_EXP, LANES), lambda i: (i, 0, 0))],
        out_shape=[jax.ShapeDtypeStruct((t, D), F32),
                   jax.ShapeDtypeStruct((t, D), BF16),
                   jax.ShapeDtypeStruct((nb, TOP_K, TB), jnp.int32),
                   jax.ShapeDtypeStruct((nb, TOP_K, TB), F32),
                   jax.ShapeDtypeStruct((nb, N_EXP, LANES), F32)],
        compiler_params=_cparams(("arbitrary",)),
        name="out_proj_route",
    )(*ys, x, mods, g, w_bf16, rw_t, rb)


def moe_tables(cnt, n_tiles):
    nb = cnt.shape[0]
    cnt8 = (cnt + CH - 1) // CH * CH
    ends = jnp.cumsum(cnt8, axis=1)
    off = ends - cnt8
    nchunks = ends[:, -1] // CH
    tot = jnp.sum(cnt8, axis=0)
    totp = (tot + TM - 1) // TM * TM
    eend = jnp.cumsum(totp)
    estart = eend - totp
    gdst = estart[None, :] + jnp.cumsum(cnt8, axis=0) - cnt8
    r = jnp.arange(NCH, dtype=jnp.int32) * CH
    e_of_c = jnp.minimum(jnp.sum((ends[:, None, :] <= r[None, :, None]).astype(jnp.int32), axis=-1), N_EXP - 1)
    pick = e_of_c[:, :, None] == jnp.arange(N_EXP, dtype=jnp.int32)[None, None, :]
    g_of_c = jnp.sum(jnp.where(pick, gdst[:, None, :], 0), axis=-1)
    o_of_c = jnp.sum(jnp.where(pick, off[:, None, :], 0), axis=-1)
    gchunk = (g_of_c + r[None, :] - o_of_c) // CH
    nused = eend[-1] // TM
    ti = jnp.arange(n_tiles, dtype=jnp.int32)
    tile_e = jnp.sum((eend[None, :] // TM <= jnp.minimum(ti, nused - 1)[:, None]).astype(jnp.int32), axis=-1)
    tile_e = jnp.minimum(tile_e, N_EXP - 1)
    has = totp > 0
    ei = jnp.arange(N_EXP, dtype=jnp.int32)
    later = has[None, :] & (ei[None, :] > ei[:, None])
    next_of = jnp.min(jnp.where(later, ei[None, :], N_EXP), axis=1)
    next_of = jnp.where(next_of == N_EXP, -1, next_of)
    ordinal = jnp.cumsum(has.astype(jnp.int32)) - 1
    tile_next = jnp.sum(jnp.where(tile_e[:, None] == ei[None, :], next_of[None, :], 0), axis=1)
    tile_slot = jnp.sum(jnp.where(tile_e[:, None] == ei[None, :], ordinal[None, :], 0), axis=1) % 2
    i32 = lambda a: a.astype(jnp.int32)
    return dict(off=off, gchunk=i32(gchunk.reshape(-1)), nchunks=i32(nchunks), tile_e=i32(tile_e),
                tile_next=i32(tile_next), tile_slot=i32(tile_slot), nused=i32(nused.reshape(1)),
                pad_lo=i32((estart + tot) // CH), pad_hi=i32(eend // CH))


def _dest_rows(eidx, off_col):
    e_iota = lax.broadcasted_iota(jnp.int32, (N_EXP, TB), 0)
    ohs = [e_iota == eidx[k:k + 1, :] for k in range(TOP_K)]
    ind = jnp.zeros((N_EXP, TB), F32)
    for oh in ohs:
        ind += oh.astype(F32)
    ti = lax.broadcasted_iota(jnp.int32, (TB, TB), 0)
    tj = lax.broadcasted_iota(jnp.int32, (TB, TB), 1)
    upper = (ti <= tj).astype(BF16)
    rank_incl = jnp.dot(ind.astype(BF16), upper, preferred_element_type=F32)
    base = off_col + rank_incl - ind
    return [jnp.sum(jnp.where(oh, base, 0.0), axis=0, keepdims=True).astype(jnp.int32) for oh in ohs]


def _dispatch_kernel(gchunk_ref, nch_ref, plo_ref, phi_ref, nused_ref, h2_ref, eidx_ref, off_ref, xs_ref,
                     xbs_ref, zbuf_ref, sem, zsem):
    blk = pl.program_id(0)
    last = pl.num_programs(0) - 1
    n_tiles = xs_ref.shape[0] // TM

    def zero_fill(start):
        def pad_chunk(c, carry):
            cp = pltpu.make_async_copy(zbuf_ref.at[pl.ds(0, CH), :],
                                       xs_ref.at[pl.ds(pl.multiple_of(c * CH, CH), CH), :], zsem)
            cp.start() if start else cp.wait()
            return carry

        def per_expert(e, carry):
            lax.fori_loop(plo_ref[e], phi_ref[e], pad_chunk, 0)
            return carry

        def tail_tile(t, carry):
            cp = pltpu.make_async_copy(zbuf_ref, xs_ref.at[pl.ds(pl.multiple_of(t * TM, TM), TM), :], zsem)
            cp.start() if start else cp.wait()
            return carry

        lax.fori_loop(0, N_EXP, per_expert, 0)
        lax.fori_loop(nused_ref[0], n_tiles, tail_tile, 0)

    @pl.when(blk == 0)
    def _():
        zbuf_ref[...] = jnp.zeros(zbuf_ref.shape, F32)
        zero_fill(True)

    dests = _dest_rows(eidx_ref[...], off_ref[:, 0:1])
    p_iota = lax.broadcasted_iota(jnp.int32, (ROWS_BS, TB), 0)
    perm = jnp.zeros((ROWS_BS, TB), F32)
    for d in dests:
        perm += (p_iota == d).astype(F32)
    slot = blk % 2
    xbs_ref[slot] = jnp.dot(perm.astype(BF16), h2_ref[...], preferred_element_type=F32)

    def copy(b, c):
        dst = pl.multiple_of(gchunk_ref[b * NCH + c] * CH, CH)
        src = pl.multiple_of(c * CH, CH)
        return pltpu.make_async_copy(xbs_ref.at[b % 2, pl.ds(src, CH), :], xs_ref.at[pl.ds(dst, CH), :],
                                     sem.at[b % 2])

    def start(c, carry):
        copy(blk, c).start()
        return carry

    lax.fori_loop(0, nch_ref[blk], start, 0)

    def wait_block(b):
        def wait(c, carry):
            copy(b, c).wait()
            return carry
        lax.fori_loop(0, nch_ref[b], wait, 0)

    @pl.when(blk > 0)
    def _():
        wait_block(blk - 1)

    @pl.when(blk == last)
    def _():
        wait_block(blk)
        zero_fill(False)


def moe_dispatch(h2, eidx, off_b, tb, rows_alloc):
    nb = h2.shape[0] // TB
    return pl.pallas_call(
        _dispatch_kernel,
        grid_spec=pltpu.PrefetchScalarGridSpec(
            num_scalar_prefetch=5,
            grid=(nb,),
            in_specs=[pl.BlockSpec((TB, D), lambda i, *_: (i, 0)),
                      pl.BlockSpec((None, TOP_K, TB), lambda i, *_: (i, 0, 0)),
                      pl.BlockSpec((None, N_EXP, LANES), lambda i, *_: (i, 0, 0))],
            out_specs=pl.BlockSpec(memory_space=pl.ANY),
            scratch_shapes=[pltpu.VMEM((2, ROWS_BS, D), F32), pltpu.VMEM((TM, D), F32),
                            pltpu.SemaphoreType.DMA((2,)), pltpu.SemaphoreType.DMA(())]),
        out_shape=jax.ShapeDtypeStruct((rows_alloc, D), F32),
        compiler_params=_cparams(("arbitrary",)),
        name="moe_dispatch",
    )(tb["gchunk"], tb["nchunks"], tb["pad_lo"], tb["pad_hi"], tb["nused"], h2, eidx, off_b)


def _expert_kernel(layer, te_ref, tnext_ref, tslot_ref, nused_ref, x_ref, bgu_ref, bdn_ref, wgu_hbm, wdn_hbm,
                   y_ref, wgu_f32, wdn_f32, wgu_bf, wdn_bf, sem):
    i = pl.program_id(0)

    def fetch(e, slot, start):
        for k, (src, dst) in enumerate(((wgu_hbm, wgu_f32), (wdn_hbm, wdn_f32))):
            cp = pltpu.make_async_copy(src.at[layer, e], dst.at[slot], sem.at[slot, k])
            cp.start() if start else cp.wait()

    @pl.when(i >= nused_ref[0])
    def _():
        y_ref[...] = jnp.zeros(y_ref.shape, F32)

    @pl.when(i < nused_ref[0])
    def _():
        e = te_ref[i]
        slot = tslot_ref[i]
        first = jnp.logical_or(i == 0, e != te_ref[jnp.maximum(i - 1, 0)])

        @pl.when(i == 0)
        def _():
            fetch(e, slot, True)

        @pl.when(first)
        def _():
            fetch(e, slot, False)
            nxt = tnext_ref[i]

            @pl.when(nxt >= 0)
            def _():
                fetch(nxt, 1 - slot, True)

            wgu_bf[...] = wgu_f32[slot].astype(BF16)
            wdn_bf[...] = wdn_f32[slot].astype(BF16)

        gu = jnp.dot(x_ref[...].astype(BF16), wgu_bf[...], preferred_element_type=F32) + bgu_ref[...]
        g = jnp.minimum(gu[:, 0:D_FF], SWIGLU_LIMIT)
        u = jnp.clip(gu[:, D_FF:2 * D_FF], -SWIGLU_LIMIT, SWIGLU_LIMIT)
        act = g * jax.nn.sigmoid(SWIGLU_ALPHA * g) * (u + 1.0)
        y_ref[...] = jnp.dot(act.astype(BF16), wdn_bf[...], preferred_element_type=F32) + bdn_ref[...]


def moe_experts(xs, tb, layer, w_gu, b_gu, w_dn, b_dn):
    n_tiles = xs.shape[0] // TM
    depth = w_gu.shape[0]

    def xmap(i, te, tn, ts, nu):
        return (jnp.minimum(i, jnp.maximum(nu[0] - 1, 0)), 0)

    def bmap(i, te, tn, ts, nu):
        return (layer, te[i], 0, 0)

    return pl.pallas_call(
        functools.partial(_expert_kernel, layer),
        grid_spec=pltpu.PrefetchScalarGridSpec(
            num_scalar_prefetch=4,
            grid=(n_tiles,),
            in_specs=[pl.BlockSpec((TM, D), xmap),
                      pl.BlockSpec((None, None, 1, 2 * D_FF), bmap),
                      pl.BlockSpec((None, None, 1, D), bmap),
                      pl.BlockSpec(memory_space=pl.ANY),
                      pl.BlockSpec(memory_space=pl.ANY)],
            out_specs=pl.BlockSpec((TM, D), lambda i, *_: (i, 0)),
            scratch_shapes=[pltpu.VMEM((2, D, 2 * D_FF), F32), pltpu.VMEM((2, D_FF, D), F32),
                            pltpu.VMEM((D, 2 * D_FF), BF16), pltpu.VMEM((D_FF, D), BF16),
                            pltpu.SemaphoreType.DMA((2, 2))]),
        out_shape=jax.ShapeDtypeStruct((xs.shape[0], D), F32),
        compiler_params=_cparams(("arbitrary",)),
        name="moe_experts",
    )(tb["tile_e"], tb["tile_next"], tb["tile_slot"], tb["nused"], xs,
      b_gu.reshape(depth, N_EXP, 1, 2 * D_FF), b_dn.reshape(depth, N_EXP, 1, D), w_gu, w_dn)


def _combine_kernel(final, gchunk_ref, nch_ref, eidx_ref, ew_ref, off_ref, x1_ref, mod_ref, fg_ref, ys_ref,
                    o_ref, ybs_ref, sem):
    blk = pl.program_id(0)
    n = nch_ref[blk]
    slot = blk % 2

    def copy(b, c):
        src = pl.multiple_of(gchunk_ref[b * NCH + c] * CH, CH)
        dst = pl.multiple_of(c * CH, CH)
        return pltpu.make_async_copy(ys_ref.at[pl.ds(src, CH), :], ybs_ref.at[b % 2, pl.ds(dst, CH), :],
                                     sem.at[b % 2])

    def fetch_block(b):
        def start(c, carry):
            copy(b, c).start()
            return carry
        lax.fori_loop(0, nch_ref[b], start, 0)

    def wait(c, carry):
        copy(blk, c).wait()
        return carry

    @pl.when(blk == 0)
    def _():
        fetch_block(blk)

    @pl.when(blk + 1 < pl.num_programs(0))
    def _():
        fetch_block(blk + 1)

    dests = _dest_rows(eidx_ref[...], off_ref[:, 0:1])
    ew = ew_ref[...]
    p_iota = lax.broadcasted_iota(jnp.int32, (ROWS_BS, TB), 0)
    perm = jnp.zeros((ROWS_BS, TB), F32)
    gsel = jnp.zeros((ROWS_BS, TB), F32)
    for k, d in enumerate(dests):
        hit = p_iota == d
        perm += hit.astype(F32)
        gsel += jnp.where(hit, ew[k:k + 1, :], 0.0)
    gate_col = jnp.sum(gsel, axis=1, keepdims=True)
    lax.fori_loop(0, n, wait, 0)
    row_iota = lax.broadcasted_iota(jnp.int32, (ROWS_BS, 1), 0)
    yb = jnp.where(row_iota < n * CH, ybs_ref[slot], 0.0) * gate_col
    moe = lax.dot_general(perm.astype(BF16), yb.astype(BF16), (((0,), (0,)), ((), ())),
                          preferred_element_type=F32)
    x2 = x1_ref[...] + mod_ref[:, 5 * D:6 * D] * moe
    if final:
        ms = jnp.mean(x2 * x2, axis=-1, keepdims=True)
        x2 = x2 * lax.rsqrt(ms + EPS) * fg_ref[...]
    o_ref[...] = x2


def moe_combine(ys, eidx, ew, off_b, x1, mods, fg, gchunk, nchunks, tok_per_mod, final):
    t = x1.shape[0]
    nb = t // TB
    return pl.pallas_call(
        functools.partial(_combine_kernel, final),
        grid_spec=pltpu.PrefetchScalarGridSpec(
            num_scalar_prefetch=2,
            grid=(nb,),
            in_specs=[pl.BlockSpec((None, TOP_K, TB), lambda i, *_: (i, 0, 0)),
                      pl.BlockSpec((None, TOP_K, TB), lambda i, *_: (i, 0, 0)),
                      pl.BlockSpec((None, N_EXP, LANES), lambda i, *_: (i, 0, 0)),
                      pl.BlockSpec((TB, D), lambda i, *_: (i, 0)),
                      pl.BlockSpec((None, 1, 6 * D), lambda i, *_: (i * TB // tok_per_mod, 0, 0)),
                      pl.BlockSpec((1, D), lambda i, *_: (0, 0)),
                      pl.BlockSpec(memory_space=pl.ANY)],
            out_specs=pl.BlockSpec((TB, D), lambda i, *_: (i, 0)),
            scratch_shapes=[pltpu.VMEM((2, ROWS_BS, D), F32), pltpu.SemaphoreType.DMA((2,))]),
        out_shape=jax.ShapeDtypeStruct((t, D), F32),
        compiler_params=_cparams(("arbitrary",)),
        name="moe_combine",
    )(gchunk, nchunks, eidx, ew, off_b, x1, mods, fg, ys)


def moe_layer(h2, eidx, ew, cnt_b, x1, mods, fg, layer, w_gu, b_gu, w_dn, b_dn, tok_per_mod, final):
    t = h2.shape[0]
    nb = t // TB
    max_rows = t * TOP_K + nb * N_EXP * (CH - 1) + N_EXP * (TM - CH)
    n_tiles = (max_rows + TM - 1) // TM
    cnt = cnt_b[:, :, 0].astype(jnp.int32)
    tb = moe_tables(cnt, n_tiles)
    off_b = jnp.broadcast_to(tb["off"].astype(F32)[:, :, None], (nb, N_EXP, LANES))
    xs = moe_dispatch(h2, eidx, off_b, tb, n_tiles * TM)
    ys = moe_experts(xs, tb, layer, w_gu, b_gu, w_dn, b_dn)
    return moe_combine(ys, eidx, ew, off_b, x1, mods, fg, tb["gchunk"], tb["nchunks"], tok_per_mod, final)


N_HEADS = 4


def _stack_heads(x, head_w):
    lane_h = lax.broadcasted_iota(jnp.int32, x.shape, 1) // head_w
    return jnp.concatenate([jnp.where(lane_h == h, x, 0.0) for h in range(N_HEADS)], axis=0)


def _unstack_heads(xs, head_w):
    r = xs.shape[0] // N_HEADS
    lane_h = lax.broadcasted_iota(jnp.int32, (r, xs.shape[1]), 1) // head_w
    out = jnp.zeros((r, xs.shape[1]), F32)
    for h in range(N_HEADS):
        out = jnp.where(lane_h == h, xs[h * r:(h + 1) * r, :], out)
    return out


def _block_diag_mask(rows, cols, rw, cw):
    ri = lax.broadcasted_iota(jnp.int32, (rows, cols), 0) // rw
    ci = lax.broadcasted_iota(jnp.int32, (rows, cols), 1) // cw
    return ri == ci


def _head_rmsnorm(o, head_w):
    n = o.shape[1]
    bd = _block_diag_mask(n, n, head_w, head_w).astype(F32)
    ms = jnp.dot(o * o, bd, precision=HI, preferred_element_type=F32) * (1.0 / head_w)
    return o * lax.rsqrt(ms + EPS)


def _nt(a, b, **kw):
    return lax.dot_general(a, b, (((1,), (1,)), ((), ())), preferred_element_type=F32, **kw)


def _tn(a, b, **kw):
    return lax.dot_general(a, b, (((0,), (0,)), ((), ())), preferred_element_type=F32, **kw)


NA_DH = 64
NA_WIN_R = 8
NA_WIN_C = 16
NA_ROWS = 64


def _na_lat_kernel(q_ref, k_ref, v_ref, kc_ref, vc_ref, bias_ref, o_ref):
    r = pl.program_id(1)
    start = jnp.clip(r - NA_WIN_R // 2, 0, NA_ROWS - NA_WIN_R)
    rows = pl.ds(pl.multiple_of(start * GRID_W, GRID_W), NA_WIN_R * GRID_W)
    qs = _stack_heads(q_ref[...] * (NA_DH ** -0.5), NA_DH).astype(BF16)
    kl = k_ref[rows, :].astype(BF16)
    vl = v_ref[rows, :].astype(BF16)
    s_loc = _nt(qs, kl) + bias_ref[...]
    s_ctx = _nt(qs, kc_ref[...].astype(BF16))
    m = jnp.maximum(jnp.max(s_loc, axis=1, keepdims=True), jnp.max(s_ctx, axis=1, keepdims=True))
    p_loc = jnp.exp(s_loc - m)
    p_ctx = jnp.exp(s_ctx - m)
    den = jnp.sum(p_loc, axis=1, keepdims=True) + jnp.sum(p_ctx, axis=1, keepdims=True)
    o = jnp.dot(p_loc.astype(BF16), vl, preferred_element_type=F32)
    o += jnp.dot(p_ctx.astype(BF16), vc_ref[...].astype(BF16), preferred_element_type=F32)
    o_ref[...] = _unstack_heads(o / den, NA_DH)


def _na_bias_table(rpb):
    col = np.arange(GRID_W)
    c_start = np.clip(col - NA_WIN_C // 2, 0, GRID_W - NA_WIN_C)
    col_mask = (col[None, :] >= c_start[:, None]) & (col[None, :] < c_start[:, None] + NA_WIN_C)
    c_idx = np.clip(col[None, :] - col[:, None], -(NA_WIN_C - 1), NA_WIN_C - 1) + (NA_WIN_C - 1)
    onehot = jnp.asarray(c_idx[None, :, :] == np.arange(2 * NA_WIN_C - 1)[:, None, None], F32)
    tb = jnp.einsum('hrc,cqk->hrqk', rpb, onehot, precision=HI)
    tb = jnp.where(col_mask[None, None], tb, -jnp.inf)
    out = []
    for ri0 in range(NA_WIN_R):
        blk = tb[:, ri0:ri0 + NA_WIN_R]
        out.append(blk.transpose(0, 2, 1, 3).reshape(N_HEADS * GRID_W, NA_WIN_R * GRID_W))
    return jnp.stack(out, 0)


def na_latent(proj, tok0, bs, ls, kc, vc, rpb):
    lc = kc.shape[1]
    bias = _na_bias_table(rpb)
    nrow = ls // GRID_W
    rb0 = tok0 // GRID_W
    sb0 = tok0 // ls
    return pl.pallas_call(
        _na_lat_kernel,
        grid=(bs, nrow),
        in_specs=[pl.BlockSpec((GRID_W, 256), lambda b, r: (rb0 + b * nrow + r, P_NQ // 256)),
                  pl.BlockSpec((ls, 256), lambda b, r: (sb0 + b, P_NK // 256)),
                  pl.BlockSpec((ls, 256), lambda b, r: (sb0 + b, P_NV // 256)),
                  pl.BlockSpec((None, lc, 256), lambda b, r: (b, 0, 0)),
                  pl.BlockSpec((None, lc, 256), lambda b, r: (b, 0, 0)),
                  pl.BlockSpec((None, N_HEADS * GRID_W, NA_WIN_R * GRID_W),
                               lambda b, r: (jnp.clip(r - NA_WIN_R // 2, 0, NA_ROWS - NA_WIN_R) - r + NA_WIN_R - 1, 0, 0))],
        out_specs=pl.BlockSpec((GRID_W, 256), lambda b, r: (b * nrow + r, 0)),
        out_shape=jax.ShapeDtypeStruct((bs * ls, 256), F32),
        compiler_params=_cparams(("arbitrary", "arbitrary")),
        name="na_latent",
    )(proj, proj, proj, kc, vc, bias)


def _na_ctx_kernel(q_ref, k_ref, v_ref, o_ref):
    qs = _stack_heads(q_ref[...] * (NA_DH ** -0.5), NA_DH).astype(BF16)
    s = _nt(qs, k_ref[...].astype(BF16))
    m = jnp.max(s, axis=1, keepdims=True)
    p = jnp.exp(s - m)
    den = jnp.sum(p, axis=1, keepdims=True)
    o = jnp.dot(p.astype(BF16), v_ref[...].astype(BF16), preferred_element_type=F32)
    o_ref[...] = _unstack_heads(o / den, NA_DH)


def na_context(proj, bp, lp):
    return pl.pallas_call(
        _na_ctx_kernel,
        grid=(bp,),
        in_specs=[pl.BlockSpec((lp, 256), lambda b: (b, P_NQ // 256)),
                  pl.BlockSpec((lp, 256), lambda b: (b, P_NK // 256)),
                  pl.BlockSpec((lp, 256), lambda b: (b, P_NV // 256))],
        out_specs=pl.BlockSpec((lp, 256), lambda b: (b, 0)),
        out_shape=jax.ShapeDtypeStruct((bp * lp, 256), F32),
        compiler_params=_cparams(("arbitrary",)),
        name="na_context",
    )(proj, proj, proj)


GLA_DK = 32
GLA_DV = 64
GLA_C = 64
GLA_NORMALIZER = 16.0


def _gla_kernel(has_state, seq, *refs):
    if has_state:
        q_ref, k_ref, v_ref, g_ref, aux_ref, wg_ref, bg_ref, gn_ref, s0_ref, y_ref, acc_ref = refs
    else:
        q_ref, k_ref, v_ref, g_ref, aux_ref, wg_ref, bg_ref, gn_ref, y_ref, sfin_ref, acc_ref = refs
    c_sz = GLA_C
    n_chunks = seq // c_sz
    ti = lax.broadcasted_iota(jnp.int32, (c_sz, c_sz), 0)
    tj = lax.broadcasted_iota(jnp.int32, (c_sz, c_sz), 1)
    ai = lax.broadcasted_iota(jnp.int32, (N_HEADS * c_sz, c_sz), 0) % c_sz
    aj = lax.broadcasted_iota(jnp.int32, (N_HEADS * c_sz, c_sz), 1)
    bd = _block_diag_mask(N_HEADS * GLA_DV, N_HEADS * GLA_DK, GLA_DV, GLA_DK)
    def step(d, c, st):
        tri = ((tj <= ti) if d == 0 else (tj >= ti)).astype(F32)
        amask = (aj <= ai) if d == 0 else (aj >= ai)
        rows = pl.ds(pl.multiple_of(c * c_sz, c_sz), c_sz)
        q = q_ref[rows, :] * (GLA_DK ** -0.5)
        k = k_ref[rows, :]
        v = v_ref[rows, :].astype(BF16)
        la = jax.nn.log_sigmoid(jnp.dot(aux_ref[rows, :], wg_ref[d], precision=HI, preferred_element_type=F32)
                                + bg_ref[d])
        la = la * (1.0 / GLA_NORMALIZER)
        b = jnp.dot(tri, la, precision=HI, preferred_element_type=F32)
        btot = b[c_sz - 1:c_sz, :] if d == 0 else b[0:1, :]
        qt = q * jnp.exp(b)
        kt = (k * jnp.exp(-b)).astype(BF16)
        ke = (k * jnp.exp(btot - b)).astype(BF16)
        a = _nt(_stack_heads(qt, GLA_DK).astype(BF16), kt)
        a = jnp.where(amask, a, 0.0).astype(BF16)
        o = _unstack_heads(jnp.dot(a, v, preferred_element_type=F32), GLA_DV)
        o += _nt(qt.astype(BF16), st.astype(BF16))
        acc_ref[rows, :] += o
        upd = _tn(v, ke)
        return st * jnp.exp(btot) + jnp.where(bd, upd, 0.0)

    def body(n, sts):
        return step(0, n, sts[0]), step(1, n_chunks - 1 - n, sts[1])

    acc_ref[...] = jnp.zeros(acc_ref.shape, F32)
    if has_state:
        st0 = (s0_ref[0], s0_ref[1])
    else:
        st0 = (jnp.zeros((N_HEADS * GLA_DV, N_HEADS * GLA_DK), F32),) * 2
    st_f, st_b = lax.fori_loop(0, n_chunks, body, st0)
    if not has_state:
        sfin_ref[0] = st_f
        sfin_ref[1] = st_b
    gn = gn_ref[...]

    def epi(i, carry):
        rows = pl.ds(pl.multiple_of(i * 256, 256), 256)
        g = g_ref[rows, :]
        y_ref[rows, :] = _head_rmsnorm(acc_ref[rows, :], GLA_DV) * gn * (g * jax.nn.sigmoid(g))
        return carry

    lax.fori_loop(0, seq // 256, epi, 0)


def gla_mixer(proj, tok0, nbatch, seq, w_gate, b_gate, norm_g, s0t):
    has_state = s0t is not None
    sb = tok0 // seq
    wg = jnp.zeros((2, LANES, LANES), F32).at[:, 0:w_gate.shape[1], :].set(w_gate)
    in_specs = [pl.BlockSpec((seq, 128), lambda b: (sb + b, P_GQ // 128)),
                pl.BlockSpec((seq, 128), lambda b: (sb + b, P_GK // 128)),
                pl.BlockSpec((seq, 256), lambda b: (sb + b, P_GV // 256)),
                pl.BlockSpec((seq, 256), lambda b: (sb + b, P_GG // 256)),
                pl.BlockSpec((seq, 128), lambda b: (sb + b, P_AUX // 128)),
                pl.BlockSpec((2, LANES, LANES), lambda b: (0, 0, 0)),
                pl.BlockSpec((2, 1, LANES), lambda b: (0, 0, 0)),
                pl.BlockSpec((1, 256), lambda b: (0, 0))]
    args = [proj, proj, proj, proj, proj, wg, b_gate.reshape(2, 1, LANES), jnp.tile(norm_g, N_HEADS).reshape(1, 256)]
    y_spec = pl.BlockSpec((seq, 256), lambda b: (b, 0))
    y_shape = jax.ShapeDtypeStruct((nbatch * seq, 256), F32)
    st_spec = pl.BlockSpec((None, 2, 256, 128), lambda b: (b, 0, 0, 0))
    if has_state:
        in_specs.append(st_spec)
        args.append(s0t)
        out_specs, out_shape = y_spec, y_shape
    else:
        out_specs = [y_spec, st_spec]
        out_shape = [y_shape, jax.ShapeDtypeStruct((nbatch, 2, 256, 128), F32)]
    return pl.pallas_call(
        functools.partial(_gla_kernel, has_state, seq),
        grid=(nbatch,),
        in_specs=in_specs, out_specs=out_specs, out_shape=out_shape,
        scratch_shapes=[pltpu.VMEM((seq, 256), F32)],
        compiler_params=_cparams(("arbitrary",)),
        name="gla_latent" if has_state else "gla_context",
    )(*args)


ML_DH = 64
ML_C = 128
ROPE_BASE = 10000.0
ML_GATE_LANE0 = 16


def _ml_gate_selectors():
    rep = np.zeros((2, 2, LANES, N_HEADS * ML_DH), np.float32)
    sel = np.zeros((2, 8, LANES), np.float32)
    for d in range(2):
        for g in range(2):
            for h in range(N_HEADS):
                lane = ML_GATE_LANE0 + d * 8 + g * 4 + h
                rep[d, g, lane, h * ML_DH:(h + 1) * ML_DH] = 1.0
                sel[d, g * 4 + h, lane] = 1.0
    return jnp.asarray(rep), jnp.asarray(sel)


def _rope_tables(seq):
    nf = ML_DH // 4
    inv = ROPE_BASE ** (-jnp.arange(nf, dtype=F32) / nf)
    t = np.arange(seq)
    j = np.arange(ML_DH)
    pos = np.where(j[None, :] < ML_DH // 2, (t // GRID_W)[:, None], (t % GRID_W)[:, None]).astype(np.float32)
    ang = jnp.asarray(pos) * inv[j % nf][None, :]
    first = (j % (ML_DH // 2)) < nf
    cos = jnp.tile(jnp.cos(ang), (1, N_HEADS))
    sin = jnp.tile(jnp.where(first[None, :], -jnp.sin(ang), jnp.sin(ang)), (1, N_HEADS))
    return cos, sin


def _rope(x, cos, sin_signed):
    nf = ML_DH // 4
    first = (lax.broadcasted_iota(jnp.int32, x.shape, 1) % (ML_DH // 2)) < nf
    partner = jnp.where(first, pltpu.roll(x, x.shape[1] - nf, 1), pltpu.roll(x, nf, 1))
    return x * cos + partner * sin_signed


def _mlstm_kernel(latent, seq, *refs):
    if latent:
        (q_ref, k_ref, v_ref, og_ref, aux_ref, rep_ref, sel_ref, brep_ref, bsel_ref, gn_ref, cos_ref, sin_ref,
         c0_ref, n0_ref, m0_ref, y_ref, acc_ref) = refs
    else:
        (q_ref, k_ref, v_ref, og_ref, aux_ref, rep_ref, sel_ref, brep_ref, bsel_ref, gn_ref,
         y_ref, cf_ref, nf_ref, mf_ref, acc_ref) = refs
    c_sz = min(ML_C, seq)
    n_chunks = seq // c_sz
    hw = N_HEADS * ML_DH
    ti = lax.broadcasted_iota(jnp.int32, (c_sz, c_sz), 0)
    tj = lax.broadcasted_iota(jnp.int32, (c_sz, c_sz), 1)
    bd = _block_diag_mask(hw, hw, ML_DH, ML_DH)
    for d in range(2):
        causal = (tj <= ti) if d == 0 else (tj >= ti)
        tri = causal.astype(F32)
        tri_t = ((ti <= tj) if d == 0 else (ti >= tj)).astype(F32)

        def body(n, carry, d=d, causal=causal, tri=tri, tri_t=tri_t):
            cm, nrow, mrow = carry
            c = n if d == 0 else n_chunks - 1 - n
            rows = pl.ds(pl.multiple_of(c * c_sz, c_sz), c_sz)
            q = q_ref[rows, :]
            k = k_ref[rows, :] * (ML_DH ** -0.5)
            if latent:
                q = _rope(q, cos_ref[rows, :], sin_ref[rows, :])
                k = _rope(k, cos_ref[rows, :], sin_ref[rows, :])
            v = v_ref[rows, :].astype(BF16)
            aux = aux_ref[rows, :]
            li_m = jnp.dot(aux, rep_ref[d, 0], precision=HI, preferred_element_type=F32) + brep_ref[d, 0]
            lf_m = jax.nn.log_sigmoid(jnp.dot(aux, rep_ref[d, 1], precision=HI, preferred_element_type=F32)
                                      + brep_ref[d, 1])
            f_m = jnp.dot(tri, lf_m, precision=HI, preferred_element_type=F32)
            g_t = _nt(sel_ref[d], aux, precision=HI) + bsel_ref[d][:, 0:1]
            li_t = g_t[0:N_HEADS, :]
            f_t = jnp.dot(jax.nn.log_sigmoid(g_t[N_HEADS:2 * N_HEADS, :]), tri_t, precision=HI,
                          preferred_element_type=F32)
            dms, fcols, mcols = [], [], []
            for h in range(N_HEADS):
                fcol = f_m[:, h * ML_DH:h * ML_DH + 1]
                dms.append(jnp.where(causal, fcol - f_t[h:h + 1, :] + li_t[h:h + 1, :], -jnp.inf))
                fcols.append(fcol)
                mcols.append(jnp.broadcast_to(mrow[:, h * ML_DH:h * ML_DH + 1], (c_sz, 1)))
            dm = jnp.concatenate(dms, axis=0)
            log_inter = jnp.concatenate(fcols, axis=0) + jnp.concatenate(mcols, axis=0)
            m_t = jnp.maximum(log_inter, jnp.max(dm, axis=1, keepdims=True))
            qs = _stack_heads(q, ML_DH)
            qsb = qs.astype(BF16)
            s = _nt(qsb, k.astype(BF16)) * jnp.exp(dm - m_t)
            a_t = jnp.exp(log_inter - m_t)
            inter = jnp.dot(qsb, cm.astype(BF16), preferred_element_type=F32)
            num = a_t * inter + jnp.dot(s.astype(BF16), v, preferred_element_type=F32)
            den = a_t * jnp.sum(qs * nrow, axis=1, keepdims=True) + jnp.sum(s, axis=1, keepdims=True)
            hst = num / jnp.maximum(jnp.abs(den), jnp.exp(-m_t))
            hout = _unstack_heads(hst, ML_DH)
            if d == 0:
                acc_ref[rows, :] = hout
            else:
                acc_ref[rows, :] += hout
            f_tot = f_m[c_sz - 1:c_sz, :] if d == 0 else f_m[0:1, :]
            w_end = f_tot - f_m + li_m
            m_new = jnp.maximum(f_tot + mrow, jnp.max(w_end, axis=0, keepdims=True))
            a = jnp.exp(f_tot + mrow - m_new)
            kw = k * jnp.exp(w_end - m_new)
            cm_new = cm * a + jnp.where(bd, _tn(kw.astype(BF16), v), 0.0)
            n_new = nrow * a + jnp.sum(kw, axis=0, keepdims=True)
            return cm_new, n_new, m_new

        if latent:
            init = (c0_ref[d], n0_ref[d], m0_ref[d])
        else:
            init = (jnp.zeros((hw, hw), F32), jnp.zeros((1, hw), F32), jnp.zeros((1, hw), F32))
        cm, nrow, mrow = lax.fori_loop(0, n_chunks, body, init)
        if not latent:
            cf_ref[d] = cm
            nf_ref[d] = nrow
            mf_ref[d] = mrow
    gn = gn_ref[...]

    def epi(i, carry):
        rows = pl.ds(pl.multiple_of(i * 256, 256), 256)
        y_ref[rows, :] = _head_rmsnorm(acc_ref[rows, :], ML_DH) * gn * jax.nn.sigmoid(og_ref[rows, :])
        return carry

    lax.fori_loop(0, seq // 256, epi, 0)


def mlstm_mixer(proj, tok0, nbatch, seq, b_gate, norm_g, state):
    latent = state is not None
    sb = tok0 // seq
    hw = N_HEADS * ML_DH
    rep, sel = _ml_gate_selectors()
    brep = jnp.repeat(b_gate.reshape(2, 2, N_HEADS), ML_DH, axis=-1).reshape(2, 2, 1, hw)
    bsel = jnp.broadcast_to(b_gate.reshape(2, 8, 1), (2, 8, LANES))
    col = lambda c0: pl.BlockSpec((seq, 256), lambda b: (sb + b, c0 // 256), pipeline_mode=pl.Buffered(1))
    full = lambda shape: pl.BlockSpec(shape, lambda b: (0,) * len(shape), pipeline_mode=pl.Buffered(1))
    in_specs = [col(P_MQ), col(P_MK), col(P_MV), col(P_MO),
                pl.BlockSpec((seq, 128), lambda b: (sb + b, P_AUX // 128), pipeline_mode=pl.Buffered(1)),
                full((2, 2, LANES, hw)), full((2, 8, LANES)), full((2, 2, 1, hw)), full((2, 8, LANES)), full((1, hw))]
    args = [proj, proj, proj, proj, proj, rep, sel, brep, bsel, norm_g.reshape(1, hw)]
    y_spec = pl.BlockSpec((seq, 256), lambda b: (b, 0))
    y_shape = jax.ShapeDtypeStruct((nbatch * seq, 256), F32)
    c_spec = pl.BlockSpec((None, 2, hw, hw), lambda b: (b, 0, 0, 0))
    r_spec = pl.BlockSpec((None, 2, 1, hw), lambda b: (b, 0, 0, 0))
    if latent:
        cos, sin = _rope_tables(seq)
        in_specs += [full((seq, hw)), full((seq, hw)), c_spec, r_spec, r_spec]
        args += [cos, sin, *state]
        out_specs, out_shape = y_spec, y_shape
    else:
        out_specs = [y_spec, c_spec, r_spec, r_spec]
        out_shape = [y_shape, jax.ShapeDtypeStruct((nbatch, 2, hw, hw), F32),
                     jax.ShapeDtypeStruct((nbatch, 2, 1, hw), F32), jax.ShapeDtypeStruct((nbatch, 2, 1, hw), F32)]
    return pl.pallas_call(
        functools.partial(_mlstm_kernel, latent, seq),
        grid=(nbatch,),
        in_specs=in_specs, out_specs=out_specs, out_shape=out_shape,
        scratch_shapes=[pltpu.VMEM((seq, 256), F32)],
        compiler_params=_cparams(("arbitrary",)),
        name="mlstm_latent" if latent else "mlstm_context",
    )(*args)


HY_CH = 256
HY_BANDS = 16
HY_EMB = 1 + 2 * HY_BANDS
HY_FFN = 64
FFT_N1 = 64
FFT_N2 = 128


def _hy_filter_kernel(feat_ref, w1_ref, b1_ref, w2_ref, b2_ref, w3_ref, b3_ref, fr_ref, dl_ref, o_ref):
    feats = feat_ref[...]
    a = jnp.sin(fr_ref[0:1, :] * (jnp.dot(feats, w1_ref[...], precision=HI, preferred_element_type=F32) + b1_ref[...]))
    a = jnp.sin(fr_ref[1:2, :] * (jnp.dot(a, w2_ref[...], precision=HI, preferred_element_type=F32) + b2_ref[...]))
    a = jnp.dot(a, w3_ref[...], precision=HI, preferred_element_type=F32) + b3_ref[...]
    o_ref[...] = a * jnp.exp(-feats[:, 0:1] * dl_ref[...])


def hyena_filters(seq, w1, b1, w2, b2, w3, b3, freq):
    t = jnp.arange(seq, dtype=F32)
    t_unit = t / (seq - 1)
    bands = jnp.linspace(1e-4, HY_BANDS - 1, HY_BANDS, dtype=F32)
    ang = (2.0 * math.pi / seq) * t[:, None] * bands[None, :]
    feats = jnp.concatenate([t_unit[:, None], jnp.cos(ang), -jnp.sin(ang),
                             jnp.zeros((seq, LANES - HY_EMB), F32)], axis=-1)
    w1p = jnp.zeros((LANES, HY_FFN), F32).at[0:HY_EMB].set(w1)
    deltas = jnp.abs(jnp.linspace(math.log(1e-2) / 1.5, math.log(1e-2) / 0.3, HY_CH, dtype=F32))
    rb = min(seq, 512)
    full = lambda shape: pl.BlockSpec(shape, lambda i: (0,) * len(shape))
    return pl.pallas_call(
        _hy_filter_kernel,
        grid=(seq // rb,),
        in_specs=[pl.BlockSpec((rb, LANES), lambda i: (i, 0)), full((LANES, HY_FFN)), full((1, HY_FFN)),
                  full((HY_FFN, HY_FFN)), full((1, HY_FFN)), full((HY_FFN, 4 * HY_CH)), full((1, 4 * HY_CH)),
                  full((2, HY_FFN)), full((1, 4 * HY_CH))],
        out_specs=pl.BlockSpec((rb, 4 * HY_CH), lambda i: (i, 0)),
        out_shape=jax.ShapeDtypeStruct((seq, 4 * HY_CH), F32),
        compiler_params=_cparams(("arbitrary",)),
        name="hyena_filters",
    )(feats, w1p, b1.reshape(1, -1), w2, b2.reshape(1, -1), w3, b3.reshape(1, -1), freq,
      jnp.tile(deltas, 4).reshape(1, -1))


def _filt2l(g, order):
    h_fwd = g[:, (2 * order) * HY_CH:(2 * order + 1) * HY_CH]
    h_bwd = g[:, (2 * order + 1) * HY_CH:(2 * order + 2) * HY_CH]
    return jnp.concatenate([h_fwd, jnp.zeros_like(h_fwd[:1]), jnp.flip(h_bwd[1:], axis=0)], axis=0)


def _hy_short_kernel(nblk, slab, u_ref, prev_ref, next_ref, w_ref, b_ref, x1_ref, x2_ref, z_ref):
    i = pl.program_id(1)
    u = u_ref[...]
    rb = u.shape[0]
    row = lax.broadcasted_iota(jnp.int32, u.shape, 0)
    prev_row = jnp.where(i > 0, prev_ref[SUBLANES - 1:SUBLANES, :], 0.0)
    next_row = jnp.where(i < nblk - 1, next_ref[0:1, :], 0.0)
    up = jnp.where(row == 0, prev_row, pltpu.roll(u, 1, 0))
    un = jnp.where(row == rb - 1, next_row, pltpu.roll(u, rb - 1, 0))
    y = up * w_ref[0:1, :] + u * w_ref[1:2, :] + un * w_ref[2:3, :] + b_ref[...]
    for k, o_ref in enumerate((x1_ref, x2_ref, z_ref)):
        if slab:
            for a in range(rb // slab):
                o_ref[:, a, :] = y[a * slab:(a + 1) * slab, k * HY_CH:(k + 1) * HY_CH]
        else:
            o_ref[...] = y[:, k * HY_CH:(k + 1) * HY_CH]


def hyena_short_conv(proj, tok0, nbatch, seq, w, b, slab=0):
    rb = min(seq, 1024 if slab else 512)
    nblk = seq // rb
    r0 = tok0 // rb
    h0 = tok0 // SUBLANES
    hpb = rb // SUBLANES
    last = (tok0 + nbatch * seq) // SUBLANES - 1
    if slab:
        o_spec = pl.BlockSpec((None, slab, rb // slab, HY_CH), lambda bb, i: (bb, 0, i, 0))
        o_shape = jax.ShapeDtypeStruct((nbatch, slab, seq // slab, HY_CH), F32)
    else:
        o_spec = pl.BlockSpec((rb, HY_CH), lambda bb, i: (bb * nblk + i, 0))
        o_shape = jax.ShapeDtypeStruct((nbatch * seq, HY_CH), F32)
    return pl.pallas_call(
        functools.partial(_hy_short_kernel, nblk, slab),
        grid=(nbatch, nblk),
        in_specs=[pl.BlockSpec((rb, 3 * HY_CH), lambda bb, i: (r0 + bb * nblk + i, P_HU // (3 * HY_CH))),
                  pl.BlockSpec((SUBLANES, 3 * HY_CH),
                               lambda bb, i: (jnp.maximum(h0 + (bb * nblk + i) * hpb - 1, 0), P_HU // (3 * HY_CH))),
                  pl.BlockSpec((SUBLANES, 3 * HY_CH),
                               lambda bb, i: (jnp.minimum(h0 + (bb * nblk + i + 1) * hpb, last), P_HU // (3 * HY_CH))),
                  pl.BlockSpec((3, 3 * HY_CH), lambda bb, i: (0, 0)),
                  pl.BlockSpec((1, 3 * HY_CH), lambda bb, i: (0, 0))],
        out_specs=[o_spec, o_spec, o_spec],
        out_shape=[o_shape, o_shape, o_shape],
        compiler_params=_cparams(("arbitrary", "arbitrary")),
        name="hyena_short_conv",
    )(proj, proj, proj, w, b.reshape(1, -1))


def _dft_consts_single(seq):
    n = 2 * seq
    k = np.arange(n)[:, None].astype(np.float64)
    m = np.arange(n)[None, :].astype(np.float64)
    ang = 2.0 * np.pi * k * m / n
    fwd = np.concatenate([np.cos(ang), -np.sin(ang)], axis=0)
    inv = np.concatenate([np.cos(ang.T[:seq]), -np.sin(ang.T[:seq])], axis=1) / n
    return (jnp.asarray(fwd, F32), jnp.asarray(fwd[:, :seq], F32), jnp.asarray(inv, F32))


def _cmul(zr, zi, hr, hi):
    return zr * hr - zi * hi, zr * hi + zi * hr


def _split_bf16(a):
    hi = a.astype(BF16)
    return hi, (a - hi.astype(F32)).astype(BF16)


def _dot3(a, b, dims=None):
    if dims is None:
        dims = (((a.ndim - 1,), (0,)), ((), ()))
    a_hi, a_lo = _split_bf16(a)
    b_hi, b_lo = _split_bf16(b)
    dg = functools.partial(lax.dot_general, dimension_numbers=dims, preferred_element_type=F32)
    return dg(a_hi, b_hi) + dg(a_lo, b_hi) + dg(a_hi, b_lo)


def _hy_spec_single_kernel(f_ref, g_ref, o_ref):
    o_ref[...] = _dot3(f_ref[...], g_ref[...])


def _hy_conv_single_kernel(x1_ref, x2_ref, z_ref, h_ref, bias_ref, f_ref, i_ref, o_ref):
    n = f_ref.shape[0] // 2
    z = z_ref[...]
    for order, xg_ref in enumerate((x1_ref, x2_ref)):
        zz = _dot3(f_ref[...], z)
        pr, pi = _cmul(zz[0:n], zz[n:2 * n], h_ref[order, 0:n, :], h_ref[order, n:2 * n, :])
        y = _dot3(i_ref[...], jnp.concatenate([pr, pi], axis=0))
        z = xg_ref[...] * (y + z * bias_ref[order])
    o_ref[...] = z


def hyena_context(x1, x2, z, g, bias, nbatch, seq):
    n = 2 * seq
    f_full, f_half, inv = _dft_consts_single(seq)
    filt = jnp.stack([_filt2l(g, 0), _filt2l(g, 1)], 0)
    spec = pl.pallas_call(
        _hy_spec_single_kernel,
        grid=(2,),
        in_specs=[pl.BlockSpec((2 * n, n), lambda o: (0, 0)), pl.BlockSpec((None, n, HY_CH), lambda o: (o, 0, 0))],
        out_specs=pl.BlockSpec((None, 2 * n, HY_CH), lambda o: (o, 0, 0)),
        out_shape=jax.ShapeDtypeStruct((2, 2 * n, HY_CH), F32),
        compiler_params=_cparams(("arbitrary",)),
        name="hyena_spec_context",
    )(f_full, filt)
    blk = pl.BlockSpec((seq, HY_CH), lambda b: (b, 0))
    return pl.pallas_call(
        _hy_conv_single_kernel,
        grid=(nbatch,),
        in_specs=[blk, blk, blk, pl.BlockSpec((2, 2 * n, HY_CH), lambda b: (0, 0, 0)),
                  pl.BlockSpec((2, 1, HY_CH), lambda b: (0, 0, 0)),
                  pl.BlockSpec((2 * n, seq), lambda b: (0, 0)), pl.BlockSpec((seq, 2 * n), lambda b: (0, 0))],
        out_specs=blk,
        out_shape=jax.ShapeDtypeStruct((nbatch * seq, HY_CH), F32),
        compiler_params=_cparams(("arbitrary",)),
        name="hyena_conv_context",
    )(x1, x2, z, spec, bias.reshape(2, 1, HY_CH), f_half, inv)


def _dft_consts_two_stage():
    n1, n2 = FFT_N1, FFT_N2
    n = n1 * n2
    a2 = np.arange(n2, dtype=np.float64)[:, None, None]
    k1 = np.arange(n1, dtype=np.float64)[None, :, None]
    a1 = np.arange(n1, dtype=np.float64)[None, None, :]
    th = 2.0 * np.pi * (a1 * k1 / n1 + a2 * k1 / n)
    w1 = np.concatenate([np.cos(th), -np.sin(th)], axis=1)
    tht = np.transpose(th, (0, 2, 1))
    w3 = np.concatenate([np.cos(tht), -np.sin(tht)], axis=2) / n
    ph = 2.0 * np.pi * np.arange(n2, dtype=np.float64)[:, None] * np.arange(n2, dtype=np.float64)[None, :] / n2
    c, s = np.cos(ph), np.sin(ph)
    g2 = np.block([[c, s], [-s, c]])
    g2i = np.block([[c, -s], [s, c]])
    return (jnp.asarray(w1, F32), jnp.asarray(w3, F32), jnp.asarray(g2, F32), jnp.asarray(g2i, F32))


FFT_SB = 16
FFT_KB = 8


def _dot3c(w_hi, w_lo, x):
    x_hi, x_lo = _split_bf16(x)
    return (jnp.dot(w_hi, x_hi, preferred_element_type=F32) + jnp.dot(w_lo, x_hi, preferred_element_type=F32)
            + jnp.dot(w_hi, x_lo, preferred_element_type=F32))


def _outer_fwd_kernel(x_ref, wh_ref, wl_ref, o_ref):
    for jj in range(FFT_SB):
        y = _dot3c(wh_ref[jj], wl_ref[jj], x_ref[jj])
        o_ref[jj, 0] = y[0:FFT_N1]
        o_ref[jj, 1] = y[FFT_N1:2 * FFT_N1]


def _outer_fwd(w, xs):
    nbatch, n2, n1, ch = xs.shape
    wspec = pl.BlockSpec((FFT_SB, 2 * FFT_N1, n1), lambda b, j: (j, 0, 0))
    return pl.pallas_call(
        _outer_fwd_kernel,
        grid=(nbatch, n2 // FFT_SB),
        in_specs=[pl.BlockSpec((None, FFT_SB, n1, ch), lambda b, j: (b, j, 0, 0)), wspec, wspec],
        out_specs=pl.BlockSpec((None, FFT_SB, 2, FFT_N1, ch), lambda b, j: (b, j, 0, 0, 0)),
        out_shape=jax.ShapeDtypeStruct((nbatch, n2, 2, FFT_N1, ch), F32),
        compiler_params=_cparams(("arbitrary", "arbitrary")),
        name="hyena_dft_outer_fwd",
    )(xs, *_split_bf16(w))


def _inner_kernel(conv, *refs):
    if conv:
        a_ref, gh_ref, gl_ref, h_ref, ih_ref, il_ref, o_ref = refs
    else:
        a_ref, gh_ref, gl_ref, o_ref = refs
    n = FFT_N2
    for kk in range(FFT_KB):
        x = jnp.concatenate([a_ref[:, 0, kk, :], a_ref[:, 1, kk, :]], axis=0)
        y = _dot3c(gh_ref[...], gl_ref[...], x)
        if conv:
            pr, pi = _cmul(y[0:n], y[n:2 * n], h_ref[kk, 0:n, :], h_ref[kk, n:2 * n, :])
            q = _dot3c(ih_ref[...], il_ref[...], jnp.concatenate([pr, pi], axis=0))
            o_ref[kk, 0] = q[0:n]
            o_ref[kk, 1] = q[n:2 * n]
        else:
            o_ref[kk] = y


def _inner_stage(a5, g2, spec=None, g2i=None):
    nbatch, n2, _, n1, ch = a5.shape
    conv = spec is not None
    mat = pl.BlockSpec((2 * n2, 2 * n2), lambda b, j: (0, 0))
    in_specs = [pl.BlockSpec((None, n2, 2, FFT_KB, ch), lambda b, j: (b, 0, 0, j, 0)), mat, mat]
    args = [a5, *_split_bf16(g2)]
    if conv:
        in_specs += [pl.BlockSpec((FFT_KB, 2 * n2, ch), lambda b, j: (j, 0, 0)), mat, mat]
        args += [spec, *_split_bf16(g2i)]
        out_spec = pl.BlockSpec((None, FFT_KB, 2, n2, ch), lambda b, j: (b, j, 0, 0, 0))
        out_shape = jax.ShapeDtypeStruct((nbatch, n1, 2, n2, ch), F32)
    else:
        out_spec = pl.BlockSpec((None, FFT_KB, 2 * n2, ch), lambda b, j: (b, j, 0, 0))
        out_shape = jax.ShapeDtypeStruct((nbatch, n1, 2 * n2, ch), F32)
    return pl.pallas_call(
        functools.partial(_inner_kernel, conv),
        grid=(nbatch, n1 // FFT_KB),
        in_specs=in_specs, out_specs=out_spec, out_shape=out_shape,
        compiler_params=_cparams(("arbitrary", "arbitrary")),
        name="hyena_dft_inner_conv" if conv else "hyena_dft_inner_spec",
    )(*args)


def _outer_inv_kernel(to_time_major, q_ref, wh_ref, wl_ref, *refs):
    if to_time_major:
        perm_ref, xg_ref, z_ref, b_ref, o_ref = refs
    else:
        xg_ref, z_ref, b_ref, o_ref = refs
    outs = []
    for jj in range(FFT_SB):
        qm = jnp.concatenate([q_ref[:, 0, jj, :], q_ref[:, 1, jj, :]], axis=0)
        g = xg_ref[jj] * (_dot3c(wh_ref[jj], wl_ref[jj], qm) + z_ref[jj] * b_ref[...])
        if to_time_major:
            outs.append(g)
        else:
            o_ref[jj] = g
    if to_time_major:
        y = jnp.concatenate(outs, axis=0)
        h1 = y.astype(BF16)
        r1 = y - h1.astype(F32)
        h2 = r1.astype(BF16)
        h3 = (r1 - h2.astype(F32)).astype(BF16)
        p = perm_ref[...]
        yp = (jnp.dot(p, h1, preferred_element_type=F32) + jnp.dot(p, h2, preferred_element_type=F32)
              + jnp.dot(p, h3, preferred_element_type=F32))
        o_ref[...] = yp.reshape(o_ref.shape)


def _outer_inv_gate(w, q5, xg, z, bias_row, to_time_major):
    nbatch, n2, n1h, ch = z.shape
    slab = pl.BlockSpec((None, FFT_SB, n1h, ch), lambda b, j: (b, j, 0, 0))
    wspec = pl.BlockSpec((FFT_SB, n1h, 2 * FFT_N1), lambda b, j: (j, 0, 0))
    in_specs = [pl.BlockSpec((None, FFT_N1, 2, FFT_SB, ch), lambda b, j: (b, 0, 0, j, 0)), wspec, wspec]
    args = [q5, *_split_bf16(w)]
    if to_time_major:
        rows = FFT_SB * n1h
        r = np.arange(rows)
        perm = np.zeros((rows, rows), np.float32)
        perm[r, (r % FFT_SB) * n1h + r // FFT_SB] = 1.0
        in_specs.append(pl.BlockSpec((rows, rows), lambda b, j: (0, 0)))
        args.append(jnp.asarray(perm, BF16))
        out_spec = pl.BlockSpec((None, n1h, FFT_SB, ch), lambda b, j: (b, 0, j, 0))
        out_shape = jax.ShapeDtypeStruct((nbatch, n1h, n2, ch), F32)
    else:
        out_spec, out_shape = slab, jax.ShapeDtypeStruct(z.shape, F32)
    in_specs += [slab, slab, pl.BlockSpec((1, ch), lambda b, j: (0, 0))]
    args += [xg, z, bias_row]
    return pl.pallas_call(
        functools.partial(_outer_inv_kernel, to_time_major),
        grid=(nbatch, n2 // FFT_SB),
        in_specs=in_specs, out_specs=out_spec, out_shape=out_shape,
        compiler_params=_cparams(("arbitrary", "arbitrary")),
        name="hyena_dft_outer_inv_gate",
    )(*args)


def hyena_latent(x1, x2, z, g, bias, nbatch, seq):
    n1, n2 = FFT_N1, FFT_N2
    assert 2 * seq == n1 * n2
    w1, w3, g2, g2i = _dft_consts_two_stage()
    half = seq // n2
    filt = jnp.stack([_filt2l(g, 0), _filt2l(g, 1)], 0).reshape(2, n1, n2, HY_CH).transpose(0, 2, 1, 3)
    spec = _inner_stage(_outer_fwd(w1, filt), g2)
    w1h = w1[:, :, 0:half]
    w3h = w3[:, 0:half, :]
    q5 = _inner_stage(_outer_fwd(w1h, z), g2, spec[0], g2i)
    z1 = _outer_inv_gate(w3h, q5, x1, z, bias[0].reshape(1, HY_CH), False)
    q5 = _inner_stage(_outer_fwd(w1h, z1), g2, spec[1], g2i)
    out = _outer_inv_gate(w3h, q5, x2, z1, bias[1].reshape(1, HY_CH), True)
    return out.reshape(nbatch * seq, HY_CH)


def _permute_w_in(w):
    sizes = (128, 128, 256, 256, 16, 256, 256, 256, 256, 16, 256, 256, 256, 768)
    offs = np.cumsum((0,) + sizes)
    seg = lambda j: w[:, offs[j]:offs[j + 1]]
    order = (13, 0, 1, 2, 3, 5, 6, 7, 8, 10, 11, 12, 4, 9)
    pad = jnp.zeros((w.shape[0], P_W - P_AUX - 32), w.dtype)
    return jnp.concatenate([seg(j) for j in order] + [pad], axis=1)


def kernel(x_prompt, x_sample, cache_na_k, cache_na_v, state_gla, state_mlstm_C, state_mlstm_n, state_mlstm_m, c, c_ctx, w_ada, b_ada, norm1_g, norm2_g, w_in, w_out, gla_w_gate, gla_b_gate, gla_norm_g, ml_b_gate, ml_norm_g, na_rpb, hy_conv_w, hy_conv_b, hy_w1, hy_b1, hy_w2, hy_b2, hy_w3, hy_b3, hy_freq, hy_bias, router_w, router_b, w_gu, b_gu, w_dn, b_dn, final_norm_g):
    depth = w_ada.shape[0]
    bp, lp, _ = x_prompt.shape
    bs, ls, _ = x_sample.shape
    tp = bp * lp
    assert tp == ls, "modulation rows are selected per block of DEC_SEQ tokens"
    x = jnp.concatenate([x_prompt.reshape(tp, D), x_sample.reshape(bs * ls, D)], axis=0)
    cond8 = jnp.concatenate([c_ctx[None, :], c, jnp.zeros((8 - 1 - bs, D), F32)], axis=0)
    mods_all = ada_mods(cond8, w_ada, b_ada)
    fg = final_norm_g.reshape(1, D)
    eye_h = jnp.eye(N_HEADS, dtype=F32)
    lc = cache_na_k.shape[3]
    new_k, new_v, new_gla, new_c, new_n, new_m = [], [], [], [], [], []
    for l in range(depth):
        mods = mods_all[l].reshape(8, 1, 6 * D)
        proj = in_proj(x, mods, norm1_g[l].reshape(1, D), _permute_w_in(w_in[l]).astype(BF16), ls)
        gla_c, gla_fin = gla_mixer(proj, 0, bp, lp, gla_w_gate[l], gla_b_gate[l], gla_norm_g[l], None)
        s0t = jnp.einsum('bdhkv,hg->bdhvgk', state_gla[:, l], eye_h).reshape(bs, 2, 256, 128)
        gla_s = gla_mixer(proj, tp, bs, ls, gla_w_gate[l], gla_b_gate[l], gla_norm_g[l], s0t)
        ml_c, cf, nf, mf = mlstm_mixer(proj, 0, bp, lp, ml_b_gate[l], ml_norm_g[l], None)
        c0 = jnp.einsum('bdhvk,hg->bdhkgv', state_mlstm_C[:, l], eye_h).reshape(bs, 2, 256, 256)
        n0 = state_mlstm_n[:, l].reshape(bs, 2, 1, 256)
        m0 = jnp.repeat(state_mlstm_m[:, l], ML_DH, axis=-1).reshape(bs, 2, 1, 256)
        ml_s = mlstm_mixer(proj, tp, bs, ls, ml_b_gate[l], ml_norm_g[l], (c0, n0, m0))
        na_c = na_context(proj, bp, lp)
        kct = cache_na_k[:, l].transpose(0, 2, 1, 3).reshape(bs, lc, 256)
        vct = cache_na_v[:, l].transpose(0, 2, 1, 3).reshape(bs, lc, 256)
        na_s = na_latent(proj, tp, bs, ls, kct, vct, na_rpb[l])
        hy_args = (hy_w1[l], hy_b1[l], hy_w2[l], hy_b2[l], hy_w3[l], hy_b3[l], hy_freq[l])
        hy_c = hyena_context(*hyena_short_conv(proj, 0, bp, lp, hy_conv_w[l], hy_conv_b[l]),
                             hyena_filters(lp, *hy_args), hy_bias[l], bp, lp)
        hy_s = hyena_latent(*hyena_short_conv(proj, tp, bs, ls, hy_conv_w[l], hy_conv_b[l], FFT_N2),
                            hyena_filters(ls, *hy_args), hy_bias[l], bs, ls)
        ys = [jnp.concatenate(pair, axis=0) for pair in ((gla_c, gla_s), (ml_c, ml_s), (na_c, na_s), (hy_c, hy_s))]
        x1, h2, eidx, ew, cnt_b = out_proj_route(
            ys, x, mods, norm2_g[l].reshape(1, D), w_out[l].astype(BF16), router_w[l].T,
            jnp.broadcast_to(router_b[l][:, None], (N_EXP, LANES)), ls)
        x = moe_layer(h2, eidx, ew, cnt_b, x1, mods, fg, l, w_gu, b_gu, w_dn, b_dn, ls, l == depth - 1)
        new_k.append(proj[:tp, P_NK:P_NK + 256].reshape(bp, lp, N_HEADS, NA_DH).transpose(0, 2, 1, 3))
        new_v.append(proj[:tp, P_NV:P_NV + 256].reshape(bp, lp, N_HEADS, NA_DH).transpose(0, 2, 1, 3))
        gla_blocks = [gla_fin[:, :, h * GLA_DV:(h + 1) * GLA_DV, h * GLA_DK:(h + 1) * GLA_DK] for h in range(N_HEADS)]
        new_gla.append(jnp.stack(gla_blocks, 2).transpose(0, 1, 2, 4, 3))
        c_blocks = [cf[:, :, h * ML_DH:(h + 1) * ML_DH, h * ML_DH:(h + 1) * ML_DH] for h in range(N_HEADS)]
        new_c.append(jnp.stack(c_blocks, 2).transpose(0, 1, 2, 4, 3))
        new_n.append(nf.reshape(bp, 2, N_HEADS, ML_DH))
        new_m.append(mf.reshape(bp, 2, N_HEADS, ML_DH)[..., 0])
    y_prompt = x[:tp].reshape(bp, lp, D)
    y_sample = x[tp:].reshape(bs, ls, D)
    return (y_prompt, y_sample, jnp.stack(new_k, 1), jnp.stack(new_v, 1), jnp.stack(new_gla, 1),
            jnp.stack(new_c, 1), jnp.stack(new_n, 1), jnp.stack(new_m, 1))
```

```python
import functools
import math

import numpy as np
import jax
import jax.numpy as jnp
from jax import lax
from jax.experimental import pallas as pl
from jax.experimental.pallas import tpu as pltpu

F32 = jnp.float32
BF16 = jnp.bfloat16
HI = lax.Precision.HIGHEST

LANES = 128
SUBLANES = 8
VMEM_LIMIT = 56 * 1024 * 1024

D = 1024
EPS = 1e-6
N_EXP = 32
TOP_K = 4
D_FF = 1024
SWIGLU_LIMIT = 7.0
SWIGLU_ALPHA = 1.702
GRID_W = 64

TB = 256
ROWS_BS = TB * TOP_K + N_EXP * SUBLANES
CH = SUBLANES
NCH = ROWS_BS // CH
TM = 512

P_HU = 0
P_GQ, P_GK, P_GV, P_GG = 768, 896, 1024, 1280
P_MQ, P_MK, P_MV, P_MO = 1536, 1792, 2048, 2304
P_NQ, P_NK, P_NV = 2560, 2816, 3072
P_AUX = 3328
P_W = 3456


def _cparams(sem=None):
    return pltpu.CompilerParams(dimension_semantics=sem, vmem_limit_bytes=VMEM_LIMIT)


def _ada_kernel(c_ref, w_ref, b_ref, o_ref):
    c = c_ref[...]
    s = c * jax.nn.sigmoid(c)
    o_ref[...] = jnp.dot(s, w_ref[...], precision=HI, preferred_element_type=F32) + b_ref[...]


def ada_mods(cond8, w_ada, b_ada):
    depth = w_ada.shape[0]
    tn = 1536
    return pl.pallas_call(
        _ada_kernel,
        grid=(depth, 6 * D // tn),
        in_specs=[pl.BlockSpec((8, D), lambda l, j: (0, 0)),
                  pl.BlockSpec((None, D, tn), lambda l, j: (l, 0, j)),
                  pl.BlockSpec((None, 1, tn), lambda l, j: (l, 0, j))],
        out_specs=pl.BlockSpec((None, 8, tn), lambda l, j: (l, 0, j)),
        out_shape=jax.ShapeDtypeStruct((depth, 8, 6 * D), F32),
        compiler_params=_cparams(("arbitrary", "arbitrary")),
        name="ada_mods",
    )(cond8, w_ada, b_ada.reshape(depth, 1, 6 * D))


def _rms_mod(x, g, sc, sh):
    ms = jnp.mean(x * x, axis=-1, keepdims=True)
    return (x * lax.rsqrt(ms + EPS) * g) * (1.0 + sc) + sh


def _in_kernel(x_ref, mod_ref, g_ref, w_ref, o_ref):
    h = _rms_mod(x_ref[...], g_ref[...], mod_ref[:, D:2 * D], mod_ref[:, 0:D])
    o_ref[...] = jnp.dot(h.astype(BF16), w_ref[...], preferred_element_type=F32)


def in_proj(x, mods, g, w_bf16, tok_per_mod):
    t = x.shape[0]
    tm = 256
    return pl.pallas_call(
        _in_kernel,
        grid=(t // tm,),
        in_specs=[pl.BlockSpec((tm, D), lambda i: (i, 0)),
                  pl.BlockSpec((None, 1, 6 * D), lambda i: (i * tm // tok_per_mod, 0, 0)),
                  pl.BlockSpec((1, D), lambda i: (0, 0)),
                  pl.BlockSpec((D, P_W), lambda i: (0, 0))],
        out_specs=pl.BlockSpec((tm, P_W), lambda i: (i, 0)),
        out_shape=jax.ShapeDtypeStruct((t, P_W), F32),
        compiler_params=_cparams(("arbitrary",)),
        name="in_proj",
    )(x, mods, g, w_bf16)


def _out_kernel(ya_ref, yb_ref, yc_ref, yd_ref, x_ref, mod_ref, g_ref, w_ref, rw_ref, rb_ref,
                x1_ref, h2_ref, eidx_ref, ew_ref, cnt_ref):
    y = jnp.concatenate([ya_ref[...], yb_ref[...], yc_ref[...], yd_ref[...]], axis=1).astype(BF16)
    mix = jnp.dot(y, w_ref[...], preferred_element_type=F32)
    x1 = x_ref[...] + mod_ref[:, 2 * D:3 * D] * mix
    x1_ref[...] = x1
    h2 = _rms_mod(x1, g_ref[...], mod_ref[:, 4 * D:5 * D], mod_ref[:, 3 * D:4 * D])
    h2_ref[...] = h2.astype(BF16)
    lg = lax.dot_general(rw_ref[...], h2, (((1,), (1,)), ((), ())), precision=HI,
                         preferred_element_type=F32) + rb_ref[:, 0:1]
    e_iota = lax.broadcasted_iota(jnp.int32, lg.shape, 0)
    vals, idxs = [], []
    for _ in range(TOP_K):
        m = jnp.max(lg, axis=0, keepdims=True)
        idx = jnp.min(jnp.where(lg == m, e_iota, N_EXP), axis=0, keepdims=True)
        vals.append(m)
        idxs.append(idx)
        lg = jnp.where(e_iota == idx, -jnp.inf, lg)
    ex = [jnp.exp(v - vals[0]) for v in vals]
    den = ex[0] + ex[1] + ex[2] + ex[3]
    eidx_ref[...] = jnp.concatenate(idxs, axis=0)
    ew_ref[...] = jnp.concatenate([e / den for e in ex], axis=0)
    ind = jnp.zeros(lg.shape, F32)
    for idx in idxs:
        ind += (e_iota == idx).astype(F32)
    cnt_ref[...] = jnp.broadcast_to(jnp.sum(ind, axis=1, keepdims=True), (N_EXP, LANES))


def out_proj_route(ys, x, mods, g, w_bf16, rw_t, rb, tok_per_mod):
    t = x.shape[0]
    nb = t // TB
    yspec = pl.BlockSpec((TB, 256), lambda i: (i, 0))
    return pl.pallas_call(
        _out_kernel,
        grid=(nb,),
        in_specs=[yspec, yspec, yspec, yspec,
                  pl.BlockSpec((TB, D), lambda i: (i, 0)),
                  pl.BlockSpec((None, 1, 6 * D), lambda i: (i * TB // tok_per_mod, 0, 0)),
                  pl.BlockSpec((1, D), lambda i: (0, 0)),
                  pl.BlockSpec((D, D), lambda i: (0, 0)),
                  pl.BlockSpec((N_EXP, D), lambda i: (0, 0)),
                  pl.BlockSpec((N_EXP, LANES), lambda i: (0, 0))],
        out_specs=[pl.BlockSpec((TB, D), lambda i: (i, 0)),
                   pl.BlockSpec((TB, D), lambda i: (i, 0)),
                   pl.BlockSpec((None, TOP_K, TB), lambda i: (i, 0, 0)),
                   pl.BlockSpec((None, TOP_K, TB), lambda i: (i, 0, 0)),
                   pl.BlockSpec((None, N_EXP, LANES), lambda i: (i, 0, 0))],
        out_shape=[jax.ShapeDtypeStruct((t, D), F32),
                   jax.ShapeDtypeStruct((t, D), BF16),
                   jax.ShapeDtypeStruct((nb, TOP_K, TB), jnp.int32),
                   jax.ShapeDtypeStruct((nb, TOP_K, TB), F32),
                   jax.ShapeDtypeStruct((nb, N_EXP, LANES), F32)],
        compiler_params=_cparams(("arbitrary",)),
        name="out_proj_route",
    )(*ys, x, mods, g, w_bf16, rw_t, rb)


def moe_tables(cnt, n_tiles):
    nb = cnt.shape[0]
    cnt8 = (cnt + CH - 1) // CH * CH
    ends = jnp.cumsum(cnt8, axis=1)
    off = ends - cnt8
    nchunks = ends[:, -1] // CH
    tot = jnp.sum(cnt8, axis=0)
    totp = (tot + TM - 1) // TM * TM
    eend = jnp.cumsum(totp)
    estart = eend - totp
    gdst = estart[None, :] + jnp.cumsum(cnt8, axis=0) - cnt8
    r = jnp.arange(NCH, dtype=jnp.int32) * CH
    e_of_c = jnp.minimum(jnp.sum((ends[:, None, :] <= r[None, :, None]).astype(jnp.int32), axis=-1), N_EXP - 1)
    pick = e_of_c[:, :, None] == jnp.arange(N_EXP, dtype=jnp.int32)[None, None, :]
    g_of_c = jnp.sum(jnp.where(pick, gdst[:, None, :], 0), axis=-1)
    o_of_c = jnp.sum(jnp.where(pick, off[:, None, :], 0), axis=-1)
    gchunk = (g_of_c + r[None, :] - o_of_c) // CH
    nused = eend[-1] // TM
    ti = jnp.arange(n_tiles, dtype=jnp.int32)
    tile_e = jnp.sum((eend[None, :] // TM <= jnp.minimum(ti, nused - 1)[:, None]).astype(jnp.int32), axis=-1)
    tile_e = jnp.minimum(tile_e, N_EXP - 1)
    has = totp > 0
    ei = jnp.arange(N_EXP, dtype=jnp.int32)
    later = has[None, :] & (ei[None, :] > ei[:, None])
    next_of = jnp.min(jnp.where(later, ei[None, :], N_EXP), axis=1)
    next_of = jnp.where(next_of == N_EXP, -1, next_of)
    ordinal = jnp.cumsum(has.astype(jnp.int32)) - 1
    tile_next = jnp.sum(jnp.where(tile_e[:, None] == ei[None, :], next_of[None, :], 0), axis=1)
    tile_slot = jnp.sum(jnp.where(tile_e[:, None] == ei[None, :], ordinal[None, :], 0), axis=1) % 2
    i32 = lambda a: a.astype(jnp.int32)
    return dict(off=off, gchunk=i32(gchunk.reshape(-1)), nchunks=i32(nchunks), tile_e=i32(tile_e),
                tile_next=i32(tile_next), tile_slot=i32(tile_slot), nused=i32(nused.reshape(1)),
                pad_lo=i32((estart + tot) // CH), pad_hi=i32(eend // CH))


def _dest_rows(eidx, off_col):
    e_iota = lax.broadcasted_iota(jnp.int32, (N_EXP, TB), 0)
    ohs = [e_iota == eidx[k:k + 1, :] for k in range(TOP_K)]
    ind = jnp.zeros((N_EXP, TB), F32)
    for oh in ohs:
        ind += oh.astype(F32)
    ti = lax.broadcasted_iota(jnp.int32, (TB, TB), 0)
    tj = lax.broadcasted_iota(jnp.int32, (TB, TB), 1)
    upper = (ti <= tj).astype(BF16)
    rank_incl = jnp.dot(ind.astype(BF16), upper, preferred_element_type=F32)
    base = off_col + rank_incl - ind
    return [jnp.sum(jnp.where(oh, base, 0.0), axis=0, keepdims=True).astype(jnp.int32) for oh in ohs]


def _dispatch_kernel(gchunk_ref, nch_ref, plo_ref, phi_ref, nused_ref, h2_ref, eidx_ref, off_ref, xs_ref,
                     xbs_ref, zbuf_ref, sem, zsem):
    blk = pl.program_id(0)
    last = pl.num_programs(0) - 1
    n_tiles = xs_ref.shape[0] // TM

    def zero_fill(start):
        def pad_chunk(c, carry):
            cp = pltpu.make_async_copy(zbuf_ref.at[pl.ds(0, CH), :],
                                       xs_ref.at[pl.ds(pl.multiple_of(c * CH, CH), CH), :], zsem)
            cp.start() if start else cp.wait()
            return carry

        def per_expert(e, carry):
            lax.fori_loop(plo_ref[e], phi_ref[e], pad_chunk, 0)
            return carry

        def tail_tile(t, carry):
            cp = pltpu.make_async_copy(zbuf_ref, xs_ref.at[pl.ds(pl.multiple_of(t * TM, TM), TM), :], zsem)
            cp.start() if start else cp.wait()
            return carry

        lax.fori_loop(0, N_EXP, per_expert, 0)
        lax.fori_loop(nused_ref[0], n_tiles, tail_tile, 0)

    @pl.when(blk == 0)
    def _():
        zbuf_ref[...] = jnp.zeros(zbuf_ref.shape, F32)
        zero_fill(True)

    dests = _dest_rows(eidx_ref[...], off_ref[:, 0:1])
    p_iota = lax.broadcasted_iota(jnp.int32, (ROWS_BS, TB), 0)
    perm = jnp.zeros((ROWS_BS, TB), F32)
    for d in dests:
        perm += (p_iota == d).astype(F32)
    slot = blk % 2
    xbs_ref[slot] = jnp.dot(perm.astype(BF16), h2_ref[...], preferred_element_type=F32)

    def copy(b, c):
        dst = pl.multiple_of(gchunk_ref[b * NCH + c] * CH, CH)
        src = pl.multiple_of(c * CH, CH)
        return pltpu.make_async_copy(xbs_ref.at[b % 2, pl.ds(src, CH), :], xs_ref.at[pl.ds(dst, CH), :],
                                     sem.at[b % 2])

    def start(c, carry):
        copy(blk, c).start()
        return carry

    lax.fori_loop(0, nch_ref[blk], start, 0)

    def wait_block(b):
        rows = nch_ref[b] * CH
        pltpu.make_async_copy(xbs_ref.at[b % 2, pl.ds(0, rows), :], xs_ref.at[pl.ds(0, rows), :],
                              sem.at[b % 2]).wait()

    @pl.when(blk > 0)
    def _():
        wait_block(blk - 1)

    @pl.when(blk == last)
    def _():
        wait_block(blk)
        zero_fill(False)


def moe_dispatch(h2, eidx, off_b, tb, rows_alloc):
    nb = h2.shape[0] // TB
    return pl.pallas_call(
        _dispatch_kernel,
        grid_spec=pltpu.PrefetchScalarGridSpec(
            num_scalar_prefetch=5,
            grid=(nb,),
            in_specs=[pl.BlockSpec((TB, D), lambda i, *_: (i, 0)),
                      pl.BlockSpec((None, TOP_K, TB), lambda i, *_: (i, 0, 0)),
                      pl.BlockSpec((None, N_EXP, LANES), lambda i, *_: (i, 0, 0))],
            out_specs=pl.BlockSpec(memory_space=pl.ANY),
            scratch_shapes=[pltpu.VMEM((2, ROWS_BS, D), F32), pltpu.VMEM((TM, D), F32),
                            pltpu.SemaphoreType.DMA((2,)), pltpu.SemaphoreType.DMA(())]),
        out_shape=jax.ShapeDtypeStruct((rows_alloc, D), F32),
        compiler_params=_cparams(("arbitrary",)),
        name="moe_dispatch",
    )(tb["gchunk"], tb["nchunks"], tb["pad_lo"], tb["pad_hi"], tb["nused"], h2, eidx, off_b)


def _expert_kernel(layer, te_ref, tnext_ref, tslot_ref, nused_ref, x_ref, bgu_ref, bdn_ref, wgu_hbm, wdn_hbm,
                   y_ref, wgu_f32, wdn_f32, wgu_bf, wdn_bf, sem):
    i = pl.program_id(0)

    def fetch(e, slot, start):
        for k, (src, dst) in enumerate(((wgu_hbm, wgu_f32), (wdn_hbm, wdn_f32))):
            cp = pltpu.make_async_copy(src.at[layer, e], dst.at[slot], sem.at[slot, k])
            cp.start() if start else cp.wait()

    @pl.when(i >= nused_ref[0])
    def _():
        y_ref[...] = jnp.zeros(y_ref.shape, F32)

    @pl.when(i < nused_ref[0])
    def _():
        e = te_ref[i]
        slot = tslot_ref[i]
        first = jnp.logical_or(i == 0, e != te_ref[jnp.maximum(i - 1, 0)])

        @pl.when(i == 0)
        def _():
            fetch(e, slot, True)

        @pl.when(first)
        def _():
            fetch(e, slot, False)
            nxt = tnext_ref[i]

            @pl.when(nxt >= 0)
            def _():
                fetch(nxt, 1 - slot, True)

            wgu_bf[...] = wgu_f32[slot].astype(BF16)
            wdn_bf[...] = wdn_f32[slot].astype(BF16)

        gu = jnp.dot(x_ref[...].astype(BF16), wgu_bf[...], preferred_element_type=F32) + bgu_ref[...]
        g = jnp.minimum(gu[:, 0:D_FF], SWIGLU_LIMIT)
        u = jnp.clip(gu[:, D_FF:2 * D_FF], -SWIGLU_LIMIT, SWIGLU_LIMIT)
        act = g * jax.nn.sigmoid(SWIGLU_ALPHA * g) * (u + 1.0)
        y_ref[...] = jnp.dot(act.astype(BF16), wdn_bf[...], preferred_element_type=F32) + bdn_ref[...]


def moe_experts(xs, tb, layer, w_gu, b_gu, w_dn, b_dn):
    n_tiles = xs.shape[0] // TM
    depth = w_gu.shape[0]

    def xmap(i, te, tn, ts, nu):
        return (jnp.minimum(i, jnp.maximum(nu[0] - 1, 0)), 0)

    def bmap(i, te, tn, ts, nu):
        return (layer, te[i], 0, 0)

    return pl.pallas_call(
        functools.partial(_expert_kernel, layer),
        grid_spec=pltpu.PrefetchScalarGridSpec(
            num_scalar_prefetch=4,
            grid=(n_tiles,),
            in_specs=[pl.BlockSpec((TM, D), xmap),
                      pl.BlockSpec((None, None, 1, 2 * D_FF), bmap),
                      pl.BlockSpec((None, None, 1, D), bmap),
                      pl.BlockSpec(memory_space=pl.ANY),
                      pl.BlockSpec(memory_space=pl.ANY)],
            out_specs=pl.BlockSpec((TM, D), lambda i, *_: (i, 0)),
            scratch_shapes=[pltpu.VMEM((2, D, 2 * D_FF), F32), pltpu.VMEM((2, D_FF, D), F32),
                            pltpu.VMEM((D, 2 * D_FF), BF16), pltpu.VMEM((D_FF, D), BF16),
                            pltpu.SemaphoreType.DMA((2, 2))]),
        out_shape=jax.ShapeDtypeStruct((xs.shape[0], D), F32),
        compiler_params=_cparams(("arbitrary",)),
        name="moe_experts",
    )(tb["tile_e"], tb["tile_next"], tb["tile_slot"], tb["nused"], xs,
      b_gu.reshape(depth, N_EXP, 1, 2 * D_FF), b_dn.reshape(depth, N_EXP, 1, D), w_gu, w_dn)


def _combine_kernel(final, gchunk_ref, nch_ref, eidx_ref, ew_ref, off_ref, x1_ref, mod_ref, fg_ref, ys_ref,
                    o_ref, ybs_ref, sem):
    blk = pl.program_id(0)
    n = nch_ref[blk]
    slot = blk % 2

    def copy(b, c):
        src = pl.multiple_of(gchunk_ref[b * NCH + c] * CH, CH)
        dst = pl.multiple_of(c * CH, CH)
        return pltpu.make_async_copy(ys_ref.at[pl.ds(src, CH), :], ybs_ref.at[b % 2, pl.ds(dst, CH), :],
                                     sem.at[b % 2])

    def fetch_block(b):
        def start(c, carry):
            copy(b, c).start()
            return carry
        lax.fori_loop(0, nch_ref[b], start, 0)

    @pl.when(blk == 0)
    def _():
        fetch_block(blk)

    @pl.when(blk + 1 < pl.num_programs(0))
    def _():
        fetch_block(blk + 1)

    dests = _dest_rows(eidx_ref[...], off_ref[:, 0:1])
    ew = ew_ref[...]
    p_iota = lax.broadcasted_iota(jnp.int32, (ROWS_BS, TB), 0)
    perm = jnp.zeros((ROWS_BS, TB), F32)
    gsel = jnp.zeros((ROWS_BS, TB), F32)
    for k, d in enumerate(dests):
        hit = p_iota == d
        perm += hit.astype(F32)
        gsel += jnp.where(hit, ew[k:k + 1, :], 0.0)
    gate_col = jnp.sum(gsel, axis=1, keepdims=True)
    pltpu.make_async_copy(ys_ref.at[pl.ds(0, n * CH), :], ybs_ref.at[slot, pl.ds(0, n * CH), :], sem.at[slot]).wait()
    row_iota = lax.broadcasted_iota(jnp.int32, (ROWS_BS, 1), 0)
    yb = jnp.where(row_iota < n * CH, ybs_ref[slot], 0.0) * gate_col
    moe = lax.dot_general(perm.astype(BF16), yb.astype(BF16), (((0,), (0,)), ((), ())),
                          preferred_element_type=F32)
    x2 = x1_ref[...] + mod_ref[:, 5 * D:6 * D] * moe
    if final:
        ms = jnp.mean(x2 * x2, axis=-1, keepdims=True)
        x2 = x2 * lax.rsqrt(ms + EPS) * fg_ref[...]
    o_ref[...] = x2


def moe_combine(ys, eidx, ew, off_b, x1, mods, fg, gchunk, nchunks, tok_per_mod, final):
    t = x1.shape[0]
    nb = t // TB
    return pl.pallas_call(
        functools.partial(_combine_kernel, final),
        grid_spec=pltpu.PrefetchScalarGridSpec(
            num_scalar_prefetch=2,
            grid=(nb,),
            in_specs=[pl.BlockSpec((None, TOP_K, TB), lambda i, *_: (i, 0, 0)),
                      pl.BlockSpec((None, TOP_K, TB), lambda i, *_: (i, 0, 0)),
                      pl.BlockSpec((None, N_EXP, LANES), lambda i, *_: (i, 0, 0)),
                      pl.BlockSpec((TB, D), lambda i, *_: (i, 0)),
                      pl.BlockSpec((None, 1, 6 * D), lambda i, *_: (i * TB // tok_per_mod, 0, 0)),
                      pl.BlockSpec((1, D), lambda i, *_: (0, 0)),
                      pl.BlockSpec(memory_space=pl.ANY)],
            out_specs=pl.BlockSpec((TB, D), lambda i, *_: (i, 0)),
            scratch_shapes=[pltpu.VMEM((2, ROWS_BS, D), F32), pltpu.SemaphoreType.DMA((2,))]),
        out_shape=jax.ShapeDtypeStruct((t, D), F32),
        compiler_params=_cparams(("arbitrary",)),
        name="moe_combine",
    )(gchunk, nchunks, eidx, ew, off_b, x1, mods, fg, ys)


def moe_layer(h2, eidx, ew, cnt_b, x1, mods, fg, layer, w_gu, b_gu, w_dn, b_dn, tok_per_mod, final):
    t = h2.shape[0]
    nb = t // TB
    max_rows = t * TOP_K + nb * N_EXP * (CH - 1) + N_EXP * (TM - CH)
    n_tiles = (max_rows + TM - 1) // TM
    cnt = cnt_b[:, :, 0].astype(jnp.int32)
    tb = moe_tables(cnt, n_tiles)
    off_b = jnp.broadcast_to(tb["off"].astype(F32)[:, :, None], (nb, N_EXP, LANES))
    xs = moe_dispatch(h2, eidx, off_b, tb, n_tiles * TM)
    ys = moe_experts(xs, tb, layer, w_gu, b_gu, w_dn, b_dn)
    return moe_combine(ys, eidx, ew, off_b, x1, mods, fg, tb["gchunk"], tb["nchunks"], tok_per_mod, final)


N_HEADS = 4


def _stack_heads(x, head_w):
    lane_h = lax.broadcasted_iota(jnp.int32, x.shape, 1) // head_w
    return jnp.concatenate([jnp.where(lane_h == h, x, 0.0) for h in range(N_HEADS)], axis=0)


def _unstack_heads(xs, head_w):
    r = xs.shape[0] // N_HEADS
    lane_h = lax.broadcasted_iota(jnp.int32, (r, xs.shape[1]), 1) // head_w
    out = jnp.zeros((r, xs.shape[1]), F32)
    for h in range(N_HEADS):
        out = jnp.where(lane_h == h, xs[h * r:(h + 1) * r, :], out)
    return out


def _block_diag_mask(rows, cols, rw, cw):
    ri = lax.broadcasted_iota(jnp.int32, (rows, cols), 0) // rw
    ci = lax.broadcasted_iota(jnp.int32, (rows, cols), 1) // cw
    return ri == ci


def _head_rmsnorm(o, head_w):
    n = o.shape[1]
    bd = _block_diag_mask(n, n, head_w, head_w).astype(F32)
    ms = jnp.dot(o * o, bd, precision=HI, preferred_element_type=F32) * (1.0 / head_w)
    return o * lax.rsqrt(ms + EPS)


def _nt(a, b, **kw):
    return lax.dot_general(a, b, (((1,), (1,)), ((), ())), preferred_element_type=F32, **kw)


def _tn(a, b, **kw):
    return lax.dot_general(a, b, (((0,), (0,)), ((), ())), preferred_element_type=F32, **kw)


NA_DH = 64
NA_WIN_R = 8
NA_WIN_C = 16
NA_ROWS = 64


NA_RPS = 2


def _na_window_start(r):
    return jnp.clip(r - NA_WIN_R // 2, 0, NA_ROWS - NA_WIN_R)


def _na_lat_kernel(q_ref, k_ref, v_ref, kc_ref, vc_ref, *refs):
    bias_refs, o_ref = refs[:NA_RPS], refs[NA_RPS]
    kc = kc_ref[...].astype(BF16)
    vc = vc_ref[...].astype(BF16)
    for t in range(NA_RPS):
        start = _na_window_start(pl.program_id(1) * NA_RPS + t)
        rows = pl.ds(pl.multiple_of(start * GRID_W, GRID_W), NA_WIN_R * GRID_W)
        tok = slice(t * GRID_W, (t + 1) * GRID_W)
        qs = _stack_heads(q_ref[tok, :] * (NA_DH ** -0.5), NA_DH).astype(BF16)
        k_all = jnp.concatenate([k_ref[rows, :].astype(BF16), kc], axis=0)
        v_all = jnp.concatenate([v_ref[rows, :].astype(BF16), vc], axis=0)
        s = _nt(qs, k_all) + bias_refs[t][...]
        m = jnp.max(s, axis=1, keepdims=True)
        p = jnp.exp(s - m)
        den = jnp.sum(p, axis=1, keepdims=True)
        o = jnp.dot(p.astype(BF16), v_all, preferred_element_type=F32)
        o_ref[tok, :] = _unstack_heads(o / den, NA_DH)


def _na_bias_table(rpb):
    col = np.arange(GRID_W)
    c_start = np.clip(col - NA_WIN_C // 2, 0, GRID_W - NA_WIN_C)
    col_mask = (col[None, :] >= c_start[:, None]) & (col[None, :] < c_start[:, None] + NA_WIN_C)
    c_idx = np.clip(col[None, :] - col[:, None], -(NA_WIN_C - 1), NA_WIN_C - 1) + (NA_WIN_C - 1)
    onehot = jnp.asarray(c_idx[None, :, :] == np.arange(2 * NA_WIN_C - 1)[:, None, None], F32)
    tb = jnp.einsum('hrc,cqk->hrqk', rpb, onehot, precision=HI)
    tb = jnp.where(col_mask[None, None], tb, -jnp.inf)
    out = []
    for ri0 in range(NA_WIN_R):
        blk = tb[:, ri0:ri0 + NA_WIN_R]
        out.append(blk.transpose(0, 2, 1, 3).reshape(N_HEADS * GRID_W, NA_WIN_R * GRID_W))
    return jnp.stack(out, 0)


def na_latent(proj, tok0, bs, ls, kc, vc, rpb):
    lc = kc.shape[1]
    bias = _na_bias_table(rpb)
    bias = jnp.concatenate([bias, jnp.zeros(bias.shape[:2] + (lc,), F32)], axis=-1)
    nstep = ls // GRID_W // NA_RPS
    tq = NA_RPS * GRID_W
    rb0 = tok0 // tq
    sb0 = tok0 // ls

    def bias_spec(t):
        def imap(b, s):
            r = s * NA_RPS + t
            return (_na_window_start(r) - r + NA_WIN_R - 1, 0, 0)
        return pl.BlockSpec((None, N_HEADS * GRID_W, NA_WIN_R * GRID_W + lc), imap)

    return pl.pallas_call(
        _na_lat_kernel,
        grid=(bs, nstep),
        in_specs=[pl.BlockSpec((tq, 256), lambda b, s: (rb0 + b * nstep + s, P_NQ // 256)),
                  pl.BlockSpec((ls, 256), lambda b, s: (sb0 + b, P_NK // 256)),
                  pl.BlockSpec((ls, 256), lambda b, s: (sb0 + b, P_NV // 256)),
                  pl.BlockSpec((None, lc, 256), lambda b, s: (b, 0, 0)),
                  pl.BlockSpec((None, lc, 256), lambda b, s: (b, 0, 0))] + [bias_spec(t) for t in range(NA_RPS)],
        out_specs=pl.BlockSpec((tq, 256), lambda b, s: (b * nstep + s, 0)),
        out_shape=jax.ShapeDtypeStruct((bs * ls, 256), F32),
        compiler_params=_cparams(("arbitrary", "arbitrary")),
        name="na_latent",
    )(proj, proj, proj, kc, vc, *([bias] * NA_RPS))


def _na_ctx_kernel(q_ref, k_ref, v_ref, o_ref):
    qs = _stack_heads(q_ref[...] * (NA_DH ** -0.5), NA_DH).astype(BF16)
    s = _nt(qs, k_ref[...].astype(BF16))
    m = jnp.max(s, axis=1, keepdims=True)
    p = jnp.exp(s - m)
    den = jnp.sum(p, axis=1, keepdims=True)
    o = jnp.dot(p.astype(BF16), v_ref[...].astype(BF16), preferred_element_type=F32)
    o_ref[...] = _unstack_heads(o / den, NA_DH)


def na_context(proj, bp, lp):
    return pl.pallas_call(
        _na_ctx_kernel,
        grid=(bp,),
        in_specs=[pl.BlockSpec((lp, 256), lambda b: (b, P_NQ // 256)),
                  pl.BlockSpec((lp, 256), lambda b: (b, P_NK // 256)),
                  pl.BlockSpec((lp, 256), lambda b: (b, P_NV // 256))],
        out_specs=pl.BlockSpec((lp, 256), lambda b: (b, 0)),
        out_shape=jax.ShapeDtypeStruct((bp * lp, 256), F32),
        compiler_params=_cparams(("arbitrary",)),
        name="na_context",
    )(proj, proj, proj)


GLA_DK = 32
GLA_DV = 64
GLA_C = 128
GLA_NORMALIZER = 16.0


def _gla_kernel(has_state, seq, *refs):
    if has_state:
        q_ref, k_ref, v_ref, g_ref, aux_ref, wg_ref, bg_ref, gn_ref, s0_ref, y_ref, acc_ref, bcum_ref = refs
    else:
        q_ref, k_ref, v_ref, g_ref, aux_ref, wg_ref, bg_ref, gn_ref, y_ref, sfin_ref, acc_ref, bcum_ref = refs
    c_sz = min(GLA_C, seq)
    n_chunks = seq // c_sz
    mid = c_sz // 2
    hk = N_HEADS * GLA_DK
    ti = lax.broadcasted_iota(jnp.int32, (c_sz, c_sz), 0)
    tj = lax.broadcasted_iota(jnp.int32, (c_sz, c_sz), 1)
    ai = lax.broadcasted_iota(jnp.int32, (N_HEADS * c_sz, c_sz), 0) % c_sz
    aj = lax.broadcasted_iota(jnp.int32, (N_HEADS * c_sz, c_sz), 1)
    bd = _block_diag_mask(N_HEADS * GLA_DV, hk, GLA_DV, GLA_DK)
    tri2 = jnp.concatenate([(tj <= ti).astype(BF16), (tj >= ti).astype(BF16)], axis=0)
    n_lr = wg_ref.shape[0]

    def pre(c, carry):
        rows = pl.ds(pl.multiple_of(c * c_sz, c_sz), c_sz)
        la = jax.nn.log_sigmoid(jnp.dot(aux_ref[rows, 0:n_lr], wg_ref[...], precision=HI, preferred_element_type=F32)
                                + bg_ref[...]) * (1.0 / GLA_NORMALIZER)
        l1 = la.astype(BF16)
        r1 = la - l1.astype(F32)
        l2 = r1.astype(BF16)
        l3 = (r1 - l2.astype(F32)).astype(BF16)
        bb = (jnp.dot(tri2, l1, preferred_element_type=F32) + jnp.dot(tri2, l2, preferred_element_type=F32)
              + jnp.dot(tri2, l3, preferred_element_type=F32))
        bcum_ref[rows, 0:hk] = bb[0:c_sz, 0:hk]
        bcum_ref[rows, hk:2 * hk] = bb[c_sz:2 * c_sz, hk:2 * hk]
        return carry

    lax.fori_loop(0, n_chunks, pre, 0)

    def step(d, c, st):
        amask = (aj <= ai) if d == 0 else (aj >= ai)
        rows = pl.ds(pl.multiple_of(c * c_sz, c_sz), c_sz)
        q = q_ref[rows, :] * (GLA_DK ** -0.5)
        k = k_ref[rows, :]
        v = v_ref[rows, :].astype(BF16)
        b = bcum_ref[rows, d * hk:(d + 1) * hk]
        btot = b[c_sz - 1:c_sz, :] if d == 0 else b[0:1, :]
        ref = b[mid - 1:mid, :] if d == 0 else b[mid:mid + 1, :]
        qt = q * jnp.exp(b - ref)
        kt = (k * jnp.exp(ref - b)).astype(BF16)
        ke = (k * jnp.exp(btot - b)).astype(BF16)
        a = _nt(_stack_heads(qt, GLA_DK).astype(BF16), kt)
        a = jnp.where(amask, a, 0.0).astype(BF16)
        o = _unstack_heads(jnp.dot(a, v, preferred_element_type=F32), GLA_DV)
        o += _nt((qt * jnp.exp(ref)).astype(BF16), st.astype(BF16))
        acc_ref[rows, :] += o
        upd = _tn(v, ke)
        return st * jnp.exp(btot) + jnp.where(bd, upd, 0.0)

    def body(n, sts):
        return step(0, n, sts[0]), step(1, n_chunks - 1 - n, sts[1])

    acc_ref[...] = jnp.zeros(acc_ref.shape, F32)
    if has_state:
        st0 = (s0_ref[0], s0_ref[1])
    else:
        st0 = (jnp.zeros((N_HEADS * GLA_DV, hk), F32),) * 2
    st_f, st_b = lax.fori_loop(0, n_chunks, body, st0)
    if not has_state:
        sfin_ref[0] = st_f
        sfin_ref[1] = st_b
    gn = gn_ref[...]

    def epi(i, carry):
        rows = pl.ds(pl.multiple_of(i * 256, 256), 256)
        g = g_ref[rows, :]
        y_ref[rows, :] = _head_rmsnorm(acc_ref[rows, :], GLA_DV) * gn * (g * jax.nn.sigmoid(g))
        return carry

    lax.fori_loop(0, seq // 256, epi, 0)


def gla_mixer(proj, tok0, nbatch, seq, w_gate, b_gate, norm_g, s0t):
    has_state = s0t is not None
    sb = tok0 // seq
    hk = N_HEADS * GLA_DK
    wg = jnp.concatenate([w_gate[0], w_gate[1]], axis=1)
    n_lr = wg.shape[0]
    col = lambda w, c0: pl.BlockSpec((seq, w), lambda b: (sb + b, c0 // w), pipeline_mode=pl.Buffered(1))
    in_specs = [col(128, P_GQ), col(128, P_GK), col(256, P_GV), col(256, P_GG), col(128, P_AUX),
                pl.BlockSpec((n_lr, 2 * hk), lambda b: (0, 0)),
                pl.BlockSpec((1, 2 * hk), lambda b: (0, 0)),
                pl.BlockSpec((1, 256), lambda b: (0, 0))]
    args = [proj, proj, proj, proj, proj, wg, b_gate.reshape(1, 2 * hk), jnp.tile(norm_g, N_HEADS).reshape(1, 256)]
    y_spec = pl.BlockSpec((seq, 256), lambda b: (b, 0))
    y_shape = jax.ShapeDtypeStruct((nbatch * seq, 256), F32)
    st_spec = pl.BlockSpec((None, 2, 256, 128), lambda b: (b, 0, 0, 0))
    if has_state:
        in_specs.append(st_spec)
        args.append(s0t)
        out_specs, out_shape = y_spec, y_shape
    else:
        out_specs = [y_spec, st_spec]
        out_shape = [y_shape, jax.ShapeDtypeStruct((nbatch, 2, 256, 128), F32)]
    return pl.pallas_call(
        functools.partial(_gla_kernel, has_state, seq),
        grid=(nbatch,),
        in_specs=in_specs, out_specs=out_specs, out_shape=out_shape,
        scratch_shapes=[pltpu.VMEM((seq, 256), F32), pltpu.VMEM((seq, 2 * hk), F32)],
        compiler_params=_cparams(("arbitrary",)),
        name="gla_latent" if has_state else "gla_context",
    )(*args)


ML_DH = 64
ML_C = 256
ROPE_BASE = 10000.0
ML_GATE_LANE0 = 16


def _ml_gate_selectors():
    rep = np.zeros((2, 2, LANES, N_HEADS * ML_DH), np.float32)
    sel = np.zeros((2, 8, LANES), np.float32)
    for d in range(2):
        for g in range(2):
            for h in range(N_HEADS):
                lane = ML_GATE_LANE0 + d * 8 + g * 4 + h
                rep[d, g, lane, h * ML_DH:(h + 1) * ML_DH] = 1.0
                sel[d, g * 4 + h, lane] = 1.0
    return jnp.asarray(rep), jnp.asarray(sel)


def _rope_tables(seq):
    nf = ML_DH // 4
    inv = ROPE_BASE ** (-jnp.arange(nf, dtype=F32) / nf)
    t = np.arange(seq)
    j = np.arange(ML_DH)
    pos = np.where(j[None, :] < ML_DH // 2, (t // GRID_W)[:, None], (t % GRID_W)[:, None]).astype(np.float32)
    ang = jnp.asarray(pos) * inv[j % nf][None, :]
    first = (j % (ML_DH // 2)) < nf
    cos = jnp.tile(jnp.cos(ang), (1, N_HEADS))
    sin = jnp.tile(jnp.where(first[None, :], -jnp.sin(ang), jnp.sin(ang)), (1, N_HEADS))
    return cos, sin


def _rope(x, cos, sin_signed):
    nf = ML_DH // 4
    first = (lax.broadcasted_iota(jnp.int32, x.shape, 1) % (ML_DH // 2)) < nf
    partner = jnp.where(first, pltpu.roll(x, x.shape[1] - nf, 1), pltpu.roll(x, nf, 1))
    return x * cos + partner * sin_signed


def _mlstm_kernel(latent, seq, *refs):
    if latent:
        (q_ref, k_ref, v_ref, og_ref, aux_ref, rep_ref, sel_ref, brep_ref, bsel_ref, gn_ref, cos_ref, sin_ref,
         c0_ref, n0_ref, m0_ref, y_ref, acc_ref) = refs
    else:
        (q_ref, k_ref, v_ref, og_ref, aux_ref, rep_ref, sel_ref, brep_ref, bsel_ref, gn_ref,
         y_ref, cf_ref, nf_ref, mf_ref, acc_ref) = refs
    c_sz = min(ML_C, seq)
    n_chunks = seq // c_sz
    hw = N_HEADS * ML_DH
    ti = lax.broadcasted_iota(jnp.int32, (c_sz, c_sz), 0)
    tj = lax.broadcasted_iota(jnp.int32, (c_sz, c_sz), 1)
    bd = _block_diag_mask(hw, hw, ML_DH, ML_DH)
    for d in range(2):
        causal = (tj <= ti) if d == 0 else (tj >= ti)
        tri = causal.astype(F32)
        tri_t = ((ti <= tj) if d == 0 else (ti >= tj)).astype(F32)

        def body(n, carry, d=d, causal=causal, tri=tri, tri_t=tri_t):
            cm, nrow, mrow = carry
            c = n if d == 0 else n_chunks - 1 - n
            rows = pl.ds(pl.multiple_of(c * c_sz, c_sz), c_sz)
            q = q_ref[rows, :]
            k = k_ref[rows, :] * (ML_DH ** -0.5)
            if latent:
                q = _rope(q, cos_ref[rows, :], sin_ref[rows, :])
                k = _rope(k, cos_ref[rows, :], sin_ref[rows, :])
            v = v_ref[rows, :].astype(BF16)
            aux = aux_ref[rows, :]
            li_m = jnp.dot(aux, rep_ref[d, 0], precision=HI, preferred_element_type=F32) + brep_ref[d, 0]
            lf_m = jax.nn.log_sigmoid(jnp.dot(aux, rep_ref[d, 1], precision=HI, preferred_element_type=F32)
                                      + brep_ref[d, 1])
            f_m = jnp.dot(tri, lf_m, precision=HI, preferred_element_type=F32)
            g_t = _nt(sel_ref[d], aux, precision=HI) + bsel_ref[d][:, 0:1]
            li_t = g_t[0:N_HEADS, :]
            f_t = jnp.dot(jax.nn.log_sigmoid(g_t[N_HEADS:2 * N_HEADS, :]), tri_t, precision=HI,
                          preferred_element_type=F32)
            dms, fcols, mcols = [], [], []
            for h in range(N_HEADS):
                fcol = f_m[:, h * ML_DH:h * ML_DH + 1]
                dms.append(jnp.where(causal, fcol - f_t[h:h + 1, :] + li_t[h:h + 1, :], -jnp.inf))
                fcols.append(fcol)
                mcols.append(jnp.broadcast_to(mrow[:, h * ML_DH:h * ML_DH + 1], (c_sz, 1)))
            dm = jnp.concatenate(dms, axis=0)
            log_inter = jnp.concatenate(fcols, axis=0) + jnp.concatenate(mcols, axis=0)
            m_t = jnp.maximum(log_inter, jnp.max(dm, axis=1, keepdims=True))
            qs = _stack_heads(q, ML_DH)
            qsb = qs.astype(BF16)
            s = _nt(qsb, k.astype(BF16)) * jnp.exp(dm - m_t)
            a_t = jnp.exp(log_inter - m_t)
            inter = jnp.dot(qsb, cm.astype(BF16), preferred_element_type=F32)
            num = a_t * inter + jnp.dot(s.astype(BF16), v, preferred_element_type=F32)
            den = a_t * jnp.sum(qs * nrow, axis=1, keepdims=True) + jnp.sum(s, axis=1, keepdims=True)
            hst = num / jnp.maximum(jnp.abs(den), jnp.exp(-m_t))
            hout = _unstack_heads(hst, ML_DH)
            if d == 0:
                acc_ref[rows, :] = hout
            else:
                acc_ref[rows, :] += hout
            f_tot = f_m[c_sz - 1:c_sz, :] if d == 0 else f_m[0:1, :]
            w_end = f_tot - f_m + li_m
            m_new = jnp.maximum(f_tot + mrow, jnp.max(w_end, axis=0, keepdims=True))
            a = jnp.exp(f_tot + mrow - m_new)
            kw = k * jnp.exp(w_end - m_new)
            cm_new = cm * a + jnp.where(bd, _tn(kw.astype(BF16), v), 0.0)
            n_new = nrow * a + jnp.sum(kw, axis=0, keepdims=True)
            return cm_new, n_new, m_new

        if latent:
            init = (c0_ref[d], n0_ref[d], m0_ref[d])
        else:
            init = (jnp.zeros((hw, hw), F32), jnp.zeros((1, hw), F32), jnp.zeros((1, hw), F32))
        cm, nrow, mrow = lax.fori_loop(0, n_chunks, body, init)
        if not latent:
            cf_ref[d] = cm
            nf_ref[d] = nrow
            mf_ref[d] = mrow
    gn = gn_ref[...]

    def epi(i, carry):
        rows = pl.ds(pl.multiple_of(i * 256, 256), 256)
        y_ref[rows, :] = _head_rmsnorm(acc_ref[rows, :], ML_DH) * gn * jax.nn.sigmoid(og_ref[rows, :])
        return carry

    lax.fori_loop(0, seq // 256, epi, 0)


def mlstm_mixer(proj, tok0, nbatch, seq, b_gate, norm_g, state):
    latent = state is not None
    sb = tok0 // seq
    hw = N_HEADS * ML_DH
    rep, sel = _ml_gate_selectors()
    brep = jnp.repeat(b_gate.reshape(2, 2, N_HEADS), ML_DH, axis=-1).reshape(2, 2, 1, hw)
    bsel = jnp.broadcast_to(b_gate.reshape(2, 8, 1), (2, 8, LANES))
    col = lambda c0: pl.BlockSpec((seq, 256), lambda b: (sb + b, c0 // 256), pipeline_mode=pl.Buffered(1))
    full = lambda shape: pl.BlockSpec(shape, lambda b: (0,) * len(shape), pipeline_mode=pl.Buffered(1))
    in_specs = [col(P_MQ), col(P_MK), col(P_MV), col(P_MO),
                pl.BlockSpec((seq, 128), lambda b: (sb + b, P_AUX // 128), pipeline_mode=pl.Buffered(1)),
                full((2, 2, LANES, hw)), full((2, 8, LANES)), full((2, 2, 1, hw)), full((2, 8, LANES)), full((1, hw))]
    args = [proj, proj, proj, proj, proj, rep, sel, brep, bsel, norm_g.reshape(1, hw)]
    y_spec = pl.BlockSpec((seq, 256), lambda b: (b, 0))
    y_shape = jax.ShapeDtypeStruct((nbatch * seq, 256), F32)
    c_spec = pl.BlockSpec((None, 2, hw, hw), lambda b: (b, 0, 0, 0))
    r_spec = pl.BlockSpec((None, 2, 1, hw), lambda b: (b, 0, 0, 0))
    if latent:
        cos, sin = _rope_tables(seq)
        in_specs += [full((seq, hw)), full((seq, hw)), c_spec, r_spec, r_spec]
        args += [cos, sin, *state]
        out_specs, out_shape = y_spec, y_shape
    else:
        out_specs = [y_spec, c_spec, r_spec, r_spec]
        out_shape = [y_shape, jax.ShapeDtypeStruct((nbatch, 2, hw, hw), F32),
                     jax.ShapeDtypeStruct((nbatch, 2, 1, hw), F32), jax.ShapeDtypeStruct((nbatch, 2, 1, hw), F32)]
    return pl.pallas_call(
        functools.partial(_mlstm_kernel, latent, seq),
        grid=(nbatch,),
        in_specs=in_specs, out_specs=out_specs, out_shape=out_shape,
        scratch_shapes=[pltpu.VMEM((seq, 256), F32)],
        compiler_params=_cparams(("arbitrary",)),
        name="mlstm_latent" if latent else "mlstm_context",
    )(*args)


HY_CH = 256
HY_BANDS = 16
HY_EMB = 1 + 2 * HY_BANDS
HY_FFN = 64
FFT_N1 = 64
FFT_N2 = 128


def _hy_filter_kernel(feat_ref, w1_ref, b1_ref, w2_ref, b2_ref, w3_ref, b3_ref, fr_ref, dl_ref, o_ref):
    feats = feat_ref[...]
    a = jnp.sin(fr_ref[0:1, :] * (jnp.dot(feats, w1_ref[...], precision=HI, preferred_element_type=F32) + b1_ref[...]))
    a = jnp.sin(fr_ref[1:2, :] * (jnp.dot(a, w2_ref[...], precision=HI, preferred_element_type=F32) + b2_ref[...]))
    a = jnp.dot(a, w3_ref[...], precision=HI, preferred_element_type=F32) + b3_ref[...]
    o_ref[...] = a * jnp.exp(-feats[:, 0:1] * dl_ref[...])


def hyena_filters(seq, w1, b1, w2, b2, w3, b3, freq):
    t = jnp.arange(seq, dtype=F32)
    t_unit = t / (seq - 1)
    bands = jnp.linspace(1e-4, HY_BANDS - 1, HY_BANDS, dtype=F32)
    ang = (2.0 * math.pi / seq) * t[:, None] * bands[None, :]
    feats = jnp.concatenate([t_unit[:, None], jnp.cos(ang), -jnp.sin(ang),
                             jnp.zeros((seq, LANES - HY_EMB), F32)], axis=-1)
    w1p = jnp.zeros((LANES, HY_FFN), F32).at[0:HY_EMB].set(w1)
    deltas = jnp.abs(jnp.linspace(math.log(1e-2) / 1.5, math.log(1e-2) / 0.3, HY_CH, dtype=F32))
    rb = min(seq, 512)
    full = lambda shape: pl.BlockSpec(shape, lambda i: (0,) * len(shape))
    return pl.pallas_call(
        _hy_filter_kernel,
        grid=(seq // rb,),
        in_specs=[pl.BlockSpec((rb, LANES), lambda i: (i, 0)), full((LANES, HY_FFN)), full((1, HY_FFN)),
                  full((HY_FFN, HY_FFN)), full((1, HY_FFN)), full((HY_FFN, 4 * HY_CH)), full((1, 4 * HY_CH)),
                  full((2, HY_FFN)), full((1, 4 * HY_CH))],
        out_specs=pl.BlockSpec((rb, 4 * HY_CH), lambda i: (i, 0)),
        out_shape=jax.ShapeDtypeStruct((seq, 4 * HY_CH), F32),
        compiler_params=_cparams(("arbitrary",)),
        name="hyena_filters",
    )(feats, w1p, b1.reshape(1, -1), w2, b2.reshape(1, -1), w3, b3.reshape(1, -1), freq,
      jnp.tile(deltas, 4).reshape(1, -1))


def _filt2l(g, order):
    h_fwd = g[:, (2 * order) * HY_CH:(2 * order + 1) * HY_CH]
    h_bwd = g[:, (2 * order + 1) * HY_CH:(2 * order + 2) * HY_CH]
    return jnp.concatenate([h_fwd, jnp.zeros_like(h_fwd[:1]), jnp.flip(h_bwd[1:], axis=0)], axis=0)


def _hy_short_kernel(nblk, slab, u_ref, prev_ref, next_ref, w_ref, b_ref, x1_ref, x2_ref, z_ref):
    i = pl.program_id(1)
    u = u_ref[...]
    rb = u.shape[0]
    row = lax.broadcasted_iota(jnp.int32, u.shape, 0)
    prev_row = jnp.where(i > 0, prev_ref[SUBLANES - 1:SUBLANES, :], 0.0)
    next_row = jnp.where(i < nblk - 1, next_ref[0:1, :], 0.0)
    up = jnp.where(row == 0, prev_row, pltpu.roll(u, 1, 0))
    un = jnp.where(row == rb - 1, next_row, pltpu.roll(u, rb - 1, 0))
    y = up * w_ref[0:1, :] + u * w_ref[1:2, :] + un * w_ref[2:3, :] + b_ref[...]
    for k, o_ref in enumerate((x1_ref, x2_ref, z_ref)):
        if slab:
            for a in range(rb // slab):
                o_ref[:, a, :] = y[a * slab:(a + 1) * slab, k * HY_CH:(k + 1) * HY_CH]
        else:
            o_ref[...] = y[:, k * HY_CH:(k + 1) * HY_CH]


def hyena_short_conv(proj, tok0, nbatch, seq, w, b, slab=0):
    rb = min(seq, 1024 if slab else 512)
    nblk = seq // rb
    r0 = tok0 // rb
    h0 = tok0 // SUBLANES
    hpb = rb // SUBLANES
    last = (tok0 + nbatch * seq) // SUBLANES - 1
    if slab:
        o_spec = pl.BlockSpec((None, slab, rb // slab, HY_CH), lambda bb, i: (bb, 0, i, 0))
        o_shape = jax.ShapeDtypeStruct((nbatch, slab, seq // slab, HY_CH), F32)
    else:
        o_spec = pl.BlockSpec((rb, HY_CH), lambda bb, i: (bb * nblk + i, 0))
        o_shape = jax.ShapeDtypeStruct((nbatch * seq, HY_CH), F32)
    return pl.pallas_call(
        functools.partial(_hy_short_kernel, nblk, slab),
        grid=(nbatch, nblk),
        in_specs=[pl.BlockSpec((rb, 3 * HY_CH), lambda bb, i: (r0 + bb * nblk + i, P_HU // (3 * HY_CH))),
                  pl.BlockSpec((SUBLANES, 3 * HY_CH),
                               lambda bb, i: (jnp.maximum(h0 + (bb * nblk + i) * hpb - 1, 0), P_HU // (3 * HY_CH))),
                  pl.BlockSpec((SUBLANES, 3 * HY_CH),
                               lambda bb, i: (jnp.minimum(h0 + (bb * nblk + i + 1) * hpb, last), P_HU // (3 * HY_CH))),
                  pl.BlockSpec((3, 3 * HY_CH), lambda bb, i: (0, 0)),
                  pl.BlockSpec((1, 3 * HY_CH), lambda bb, i: (0, 0))],
        out_specs=[o_spec, o_spec, o_spec],
        out_shape=[o_shape, o_shape, o_shape],
        compiler_params=_cparams(("arbitrary", "arbitrary")),
        name="hyena_short_conv",
    )(proj, proj, proj, w, b.reshape(1, -1))


def _dft_consts_single(seq):
    n = 2 * seq
    k = np.arange(n)[:, None].astype(np.float64)
    m = np.arange(n)[None, :].astype(np.float64)
    ang = 2.0 * np.pi * k * m / n
    fwd = np.concatenate([np.cos(ang), -np.sin(ang)], axis=0)
    inv = np.concatenate([np.cos(ang.T[:seq]), -np.sin(ang.T[:seq])], axis=1) / n
    return (jnp.asarray(fwd, F32), jnp.asarray(fwd[:, :seq], F32), jnp.asarray(inv, F32))


def _cmul(zr, zi, hr, hi):
    return zr * hr - zi * hi, zr * hi + zi * hr


def _split_bf16(a):
    hi = a.astype(BF16)
    return hi, (a - hi.astype(F32)).astype(BF16)


def _dot3(a, b, dims=None):
    if dims is None:
        dims = (((a.ndim - 1,), (0,)), ((), ()))
    a_hi, a_lo = _split_bf16(a)
    b_hi, b_lo = _split_bf16(b)
    dg = functools.partial(lax.dot_general, dimension_numbers=dims, preferred_element_type=F32)
    return dg(a_hi, b_hi) + dg(a_lo, b_hi) + dg(a_hi, b_lo)


def _hy_spec_single_kernel(f_ref, g_ref, o_ref):
    o_ref[...] = _dot3(f_ref[...], g_ref[...])


def _hy_conv_single_kernel(x1_ref, x2_ref, z_ref, h_ref, bias_ref, f_ref, i_ref, o_ref):
    n = f_ref.shape[0] // 2
    z = z_ref[...]
    for order, xg_ref in enumerate((x1_ref, x2_ref)):
        zz = _dot3(f_ref[...], z)
        pr, pi = _cmul(zz[0:n], zz[n:2 * n], h_ref[order, 0:n, :], h_ref[order, n:2 * n, :])
        y = _dot3(i_ref[...], jnp.concatenate([pr, pi], axis=0))
        z = xg_ref[...] * (y + z * bias_ref[order])
    o_ref[...] = z


def hyena_context(x1, x2, z, g, bias, nbatch, seq):
    n = 2 * seq
    f_full, f_half, inv = _dft_consts_single(seq)
    filt = jnp.stack([_filt2l(g, 0), _filt2l(g, 1)], 0)
    spec = pl.pallas_call(
        _hy_spec_single_kernel,
        grid=(2,),
        in_specs=[pl.BlockSpec((2 * n, n), lambda o: (0, 0)), pl.BlockSpec((None, n, HY_CH), lambda o: (o, 0, 0))],
        out_specs=pl.BlockSpec((None, 2 * n, HY_CH), lambda o: (o, 0, 0)),
        out_shape=jax.ShapeDtypeStruct((2, 2 * n, HY_CH), F32),
        compiler_params=_cparams(("arbitrary",)),
        name="hyena_spec_context",
    )(f_full, filt)
    blk = pl.BlockSpec((seq, HY_CH), lambda b: (b, 0))
    return pl.pallas_call(
        _hy_conv_single_kernel,
        grid=(nbatch,),
        in_specs=[blk, blk, blk, pl.BlockSpec((2, 2 * n, HY_CH), lambda b: (0, 0, 0)),
                  pl.BlockSpec((2, 1, HY_CH), lambda b: (0, 0, 0)),
                  pl.BlockSpec((2 * n, seq), lambda b: (0, 0)), pl.BlockSpec((seq, 2 * n), lambda b: (0, 0))],
        out_specs=blk,
        out_shape=jax.ShapeDtypeStruct((nbatch * seq, HY_CH), F32),
        compiler_params=_cparams(("arbitrary",)),
        name="hyena_conv_context",
    )(x1, x2, z, spec, bias.reshape(2, 1, HY_CH), f_half, inv)


def _dft_consts_two_stage():
    n1, n2 = FFT_N1, FFT_N2
    n = n1 * n2
    a2 = np.arange(n2, dtype=np.float64)[:, None, None]
    k1 = np.arange(n1, dtype=np.float64)[None, :, None]
    a1 = np.arange(n1, dtype=np.float64)[None, None, :]
    th = 2.0 * np.pi * (a1 * k1 / n1 + a2 * k1 / n)
    w1 = np.concatenate([np.cos(th), -np.sin(th)], axis=1)
    tht = np.transpose(th, (0, 2, 1))
    w3 = np.concatenate([np.cos(tht), -np.sin(tht)], axis=2) / n
    ph = 2.0 * np.pi * np.arange(n2, dtype=np.float64)[:, None] * np.arange(n2, dtype=np.float64)[None, :] / n2
    c, s = np.cos(ph), np.sin(ph)
    g2 = np.block([[c, s], [-s, c]])
    g2i = np.block([[c, -s], [s, c]])
    return (jnp.asarray(w1, F32), jnp.asarray(w3, F32), jnp.asarray(g2, F32), jnp.asarray(g2i, F32))


FFT_SB = 16
FFT_KB = 8


def _dot3c(w_hi, w_lo, x):
    x_hi, x_lo = _split_bf16(x)
    return (jnp.dot(w_hi, x_hi, preferred_element_type=F32) + jnp.dot(w_lo, x_hi, preferred_element_type=F32)
            + jnp.dot(w_hi, x_lo, preferred_element_type=F32))


def _outer_fwd_kernel(x_ref, wh_ref, wl_ref, o_ref):
    for jj in range(FFT_SB):
        y = _dot3c(wh_ref[jj], wl_ref[jj], x_ref[jj])
        o_ref[jj, 0] = y[0:FFT_N1]
        o_ref[jj, 1] = y[FFT_N1:2 * FFT_N1]


def _outer_fwd(w, xs):
    nbatch, n2, n1, ch = xs.shape
    wspec = pl.BlockSpec((FFT_SB, 2 * FFT_N1, n1), lambda b, j: (j, 0, 0))
    return pl.pallas_call(
        _outer_fwd_kernel,
        grid=(nbatch, n2 // FFT_SB),
        in_specs=[pl.BlockSpec((None, FFT_SB, n1, ch), lambda b, j: (b, j, 0, 0)), wspec, wspec],
        out_specs=pl.BlockSpec((None, FFT_SB, 2, FFT_N1, ch), lambda b, j: (b, j, 0, 0, 0)),
        out_shape=jax.ShapeDtypeStruct((nbatch, n2, 2, FFT_N1, ch), F32),
        compiler_params=_cparams(("arbitrary", "arbitrary")),
        name="hyena_dft_outer_fwd",
    )(xs, *_split_bf16(w))


def _inner_kernel(conv, *refs):
    if conv:
        a_ref, gh_ref, gl_ref, h_ref, ih_ref, il_ref, o_ref = refs
    else:
        a_ref, gh_ref, gl_ref, o_ref = refs
    n = FFT_N2
    for kk in range(FFT_KB):
        x = jnp.concatenate([a_ref[:, 0, kk, :], a_ref[:, 1, kk, :]], axis=0)
        y = _dot3c(gh_ref[...], gl_ref[...], x)
        if conv:
            pr, pi = _cmul(y[0:n], y[n:2 * n], h_ref[kk, 0:n, :], h_ref[kk, n:2 * n, :])
            q = _dot3c(ih_ref[...], il_ref[...], jnp.concatenate([pr, pi], axis=0))
            o_ref[kk, 0] = q[0:n]
            o_ref[kk, 1] = q[n:2 * n]
        else:
            o_ref[kk] = y


def _inner_stage(a5, g2, spec=None, g2i=None):
    nbatch, n2, _, n1, ch = a5.shape
    conv = spec is not None
    mat = pl.BlockSpec((2 * n2, 2 * n2), lambda b, j: (0, 0))
    in_specs = [pl.BlockSpec((None, n2, 2, FFT_KB, ch), lambda b, j: (b, 0, 0, j, 0)), mat, mat]
    args = [a5, *_split_bf16(g2)]
    if conv:
        in_specs += [pl.BlockSpec((FFT_KB, 2 * n2, ch), lambda b, j: (j, 0, 0)), mat, mat]
        args += [spec, *_split_bf16(g2i)]
        out_spec = pl.BlockSpec((None, FFT_KB, 2, n2, ch), lambda b, j: (b, j, 0, 0, 0))
        out_shape = jax.ShapeDtypeStruct((nbatch, n1, 2, n2, ch), F32)
    else:
        out_spec = pl.BlockSpec((None, FFT_KB, 2 * n2, ch), lambda b, j: (b, j, 0, 0))
        out_shape = jax.ShapeDtypeStruct((nbatch, n1, 2 * n2, ch), F32)
    return pl.pallas_call(
        functools.partial(_inner_kernel, conv),
        grid=(nbatch, n1 // FFT_KB),
        in_specs=in_specs, out_specs=out_spec, out_shape=out_shape,
        compiler_params=_cparams(("arbitrary", "arbitrary")),
        name="hyena_dft_inner_conv" if conv else "hyena_dft_inner_spec",
    )(*args)


def _outer_inv_kernel(to_time_major, q_ref, wh_ref, wl_ref, *refs):
    if to_time_major:
        perm_ref, xg_ref, z_ref, b_ref, o_ref = refs
    else:
        xg_ref, z_ref, b_ref, o_ref = refs
    outs = []
    for jj in range(FFT_SB):
        qm = jnp.concatenate([q_ref[:, 0, jj, :], q_ref[:, 1, jj, :]], axis=0)
        g = xg_ref[jj] * (_dot3c(wh_ref[jj], wl_ref[jj], qm) + z_ref[jj] * b_ref[...])
        if to_time_major:
            outs.append(g)
        else:
            o_ref[jj] = g
    if to_time_major:
        y = jnp.concatenate(outs, axis=0)
        h1 = y.astype(BF16)
        r1 = y - h1.astype(F32)
        h2 = r1.astype(BF16)
        h3 = (r1 - h2.astype(F32)).astype(BF16)
        p = perm_ref[...]
        yp = (jnp.dot(p, h1, preferred_element_type=F32) + jnp.dot(p, h2, preferred_element_type=F32)
              + jnp.dot(p, h3, preferred_element_type=F32))
        o_ref[...] = yp.reshape(o_ref.shape)


def _outer_inv_gate(w, q5, xg, z, bias_row, to_time_major):
    nbatch, n2, n1h, ch = z.shape
    slab = pl.BlockSpec((None, FFT_SB, n1h, ch), lambda b, j: (b, j, 0, 0))
    wspec = pl.BlockSpec((FFT_SB, n1h, 2 * FFT_N1), lambda b, j: (j, 0, 0))
    in_specs = [pl.BlockSpec((None, FFT_N1, 2, FFT_SB, ch), lambda b, j: (b, 0, 0, j, 0)), wspec, wspec]
    args = [q5, *_split_bf16(w)]
    if to_time_major:
        rows = FFT_SB * n1h
        r = np.arange(rows)
        perm = np.zeros((rows, rows), np.float32)
        perm[r, (r % FFT_SB) * n1h + r // FFT_SB] = 1.0
        in_specs.append(pl.BlockSpec((rows, rows), lambda b, j: (0, 0)))
        args.append(jnp.asarray(perm, BF16))
        out_spec = pl.BlockSpec((None, n1h, FFT_SB, ch), lambda b, j: (b, 0, j, 0))
        out_shape = jax.ShapeDtypeStruct((nbatch, n1h, n2, ch), F32)
    else:
        out_spec, out_shape = slab, jax.ShapeDtypeStruct(z.shape, F32)
    in_specs += [slab, slab, pl.BlockSpec((1, ch), lambda b, j: (0, 0))]
    args += [xg, z, bias_row]
    return pl.pallas_call(
        functools.partial(_outer_inv_kernel, to_time_major),
        grid=(nbatch, n2 // FFT_SB),
        in_specs=in_specs, out_specs=out_spec, out_shape=out_shape,
        compiler_params=_cparams(("arbitrary", "arbitrary")),
        name="hyena_dft_outer_inv_gate",
    )(*args)


def hyena_latent(x1, x2, z, g, bias, nbatch, seq):
    n1, n2 = FFT_N1, FFT_N2
    assert 2 * seq == n1 * n2
    w1, w3, g2, g2i = _dft_consts_two_stage()
    half = seq // n2
    filt = jnp.stack([_filt2l(g, 0), _filt2l(g, 1)], 0).reshape(2, n1, n2, HY_CH).transpose(0, 2, 1, 3)
    spec = _inner_stage(_outer_fwd(w1, filt), g2)
    w1h = w1[:, :, 0:half]
    w3h = w3[:, 0:half, :]
    q5 = _inner_stage(_outer_fwd(w1h, z), g2, spec[0], g2i)
    z1 = _outer_inv_gate(w3h, q5, x1, z, bias[0].reshape(1, HY_CH), False)
    q5 = _inner_stage(_outer_fwd(w1h, z1), g2, spec[1], g2i)
    out = _outer_inv_gate(w3h, q5, x2, z1, bias[1].reshape(1, HY_CH), True)
    return out.reshape(nbatch * seq, HY_CH)


def _permute_w_in(w):
    sizes = (128, 128, 256, 256, 16, 256, 256, 256, 256, 16, 256, 256, 256, 768)
    offs = np.cumsum((0,) + sizes)
    seg = lambda j: w[:, offs[j]:offs[j + 1]]
    order = (13, 0, 1, 2, 3, 5, 6, 7, 8, 10, 11, 12, 4, 9)
    pad = jnp.zeros((w.shape[0], P_W - P_AUX - 32), w.dtype)
    return jnp.concatenate([seg(j) for j in order] + [pad], axis=1)


def kernel(x_prompt, x_sample, cache_na_k, cache_na_v, state_gla, state_mlstm_C, state_mlstm_n, state_mlstm_m, c, c_ctx, w_ada, b_ada, norm1_g, norm2_g, w_in, w_out, gla_w_gate, gla_b_gate, gla_norm_g, ml_b_gate, ml_norm_g, na_rpb, hy_conv_w, hy_conv_b, hy_w1, hy_b1, hy_w2, hy_b2, hy_w3, hy_b3, hy_freq, hy_bias, router_w, router_b, w_gu, b_gu, w_dn, b_dn, final_norm_g):
    depth = w_ada.shape[0]
    bp, lp, _ = x_prompt.shape
    bs, ls, _ = x_sample.shape
    tp = bp * lp
    assert tp == ls, "modulation rows are selected per block of DEC_SEQ tokens"
    x = jnp.concatenate([x_prompt.reshape(tp, D), x_sample.reshape(bs * ls, D)], axis=0)
    cond8 = jnp.concatenate([c_ctx[None, :], c, jnp.zeros((8 - 1 - bs, D), F32)], axis=0)
    mods_all = ada_mods(cond8, w_ada, b_ada)
    fg = final_norm_g.reshape(1, D)
    eye_h = jnp.eye(N_HEADS, dtype=F32)
    lc = cache_na_k.shape[3]
    new_k, new_v, new_gla, new_c, new_n, new_m = [], [], [], [], [], []
    for l in range(depth):
        mods = mods_all[l].reshape(8, 1, 6 * D)
        proj = in_proj(x, mods, norm1_g[l].reshape(1, D), _permute_w_in(w_in[l]).astype(BF16), ls)
        gla_c, gla_fin = gla_mixer(proj, 0, bp, lp, gla_w_gate[l], gla_b_gate[l], gla_norm_g[l], None)
        s0t = jnp.einsum('bdhkv,hg->bdhvgk', state_gla[:, l], eye_h).reshape(bs, 2, 256, 128)
        gla_s = gla_mixer(proj, tp, bs, ls, gla_w_gate[l], gla_b_gate[l], gla_norm_g[l], s0t)
        ml_c, cf, nf, mf = mlstm_mixer(proj, 0, bp, lp, ml_b_gate[l], ml_norm_g[l], None)
        c0 = jnp.einsum('bdhvk,hg->bdhkgv', state_mlstm_C[:, l], eye_h).reshape(bs, 2, 256, 256)
        n0 = state_mlstm_n[:, l].reshape(bs, 2, 1, 256)
        m0 = jnp.repeat(state_mlstm_m[:, l], ML_DH, axis=-1).reshape(bs, 2, 1, 256)
        ml_s = mlstm_mixer(proj, tp, bs, ls, ml_b_gate[l], ml_norm_g[l], (c0, n0, m0))
        na_c = na_context(proj, bp, lp)
        kct = cache_na_k[:, l].transpose(0, 2, 1, 3).reshape(bs, lc, 256)
        vct = cache_na_v[:, l].transpose(0, 2, 1, 3).reshape(bs, lc, 256)
        na_s = na_latent(proj, tp, bs, ls, kct, vct, na_rpb[l])
        hy_args = (hy_w1[l], hy_b1[l], hy_w2[l], hy_b2[l], hy_w3[l], hy_b3[l], hy_freq[l])
        hy_c = hyena_context(*hyena_short_conv(proj, 0, bp, lp, hy_conv_w[l], hy_conv_b[l]),
                             hyena_filters(lp, *hy_args), hy_bias[l], bp, lp)
        hy_s = hyena_latent(*hyena_short_conv(proj, tp, bs, ls, hy_conv_w[l], hy_conv_b[l], FFT_N2),
                            hyena_filters(ls, *hy_args), hy_bias[l], bs, ls)
        ys = [jnp.concatenate(pair, axis=0) for pair in ((gla_c, gla_s), (ml_c, ml_s), (na_c, na_s), (hy_c, hy_s))]
        x1, h2, eidx, ew, cnt_b = out_proj_route(
            ys, x, mods, norm2_g[l].reshape(1, D), w_out[l].astype(BF16), router_w[l].T,
            jnp.broadcast_to(router_b[l][:, None], (N_EXP, LANES)), ls)
        x = moe_layer(h2, eidx, ew, cnt_b, x1, mods, fg, l, w_gu, b_gu, w_dn, b_dn, ls, l == depth - 1)
        new_k.append(proj[:tp, P_NK:P_NK + 256].reshape(bp, lp, N_HEADS, NA_DH).transpose(0, 2, 1, 3))
        new_v.append(proj[:tp, P_NV:P_NV + 256].reshape(bp, lp, N_HEADS, NA_DH).transpose(0, 2, 1, 3))
        gla_blocks = [gla_fin[:, :, h * GLA_DV:(h + 1) * GLA_DV, h * GLA_DK:(h + 1) * GLA_DK] for h in range(N_HEADS)]
        new_gla.append(jnp.stack(gla_blocks, 2).transpose(0, 1, 2, 4, 3))
        c_blocks = [cf[:, :, h * ML_DH:(h + 1) * ML_DH, h * ML_DH:(h + 1) * ML_DH] for h in range(N_HEADS)]
        new_c.append(jnp.stack(c_blocks, 2).transpose(0, 1, 2, 4, 3))
        new_n.append(nf.reshape(bp, 2, N_HEADS, ML_DH))
        new_m.append(mf.reshape(bp, 2, N_HEADS, ML_DH)[..., 0])
    y_prompt = x[:tp].reshape(bp, lp, D)
    y_sample = x[tp:].reshape(bs, ls, D)
    return (y_prompt, y_sample, jnp.stack(new_k, 1), jnp.stack(new_v, 1), jnp.stack(new_gla, 1),
            jnp.stack(new_c, 1), jnp.stack(new_n, 1), jnp.stack(new_m, 1))
```

```python
import functools
import math

import numpy as np
import jax
import jax.numpy as jnp
from jax import lax
from jax.experimental import pallas as pl
from jax.experimental.pallas import tpu as pltpu

F32 = jnp.float32
BF16 = jnp.bfloat16
HI = lax.Precision.HIGHEST

LANES = 128
SUBLANES = 8
VMEM_LIMIT = 56 * 1024 * 1024

D = 1024
EPS = 1e-6
N_EXP = 32
TOP_K = 4
D_FF = 1024
SWIGLU_LIMIT = 7.0
SWIGLU_ALPHA = 1.702
GRID_W = 64

TB = 256
ROWS_BS = TB * TOP_K + N_EXP * SUBLANES
CH = SUBLANES
NCH = ROWS_BS // CH
TM = 512

P_HU = 0
P_GQ, P_GK, P_GV, P_GG = 768, 896, 1024, 1280
P_MQ, P_MK, P_MV, P_MO = 1536, 1792, 2048, 2304
P_NQ, P_NK, P_NV = 2560, 2816, 3072
P_AUX = 3328
P_W = 3456


def _cparams(sem=None):
    return pltpu.CompilerParams(dimension_semantics=sem, vmem_limit_bytes=VMEM_LIMIT)


def _ada_kernel(c_ref, w_ref, b_ref, o_ref):
    c = c_ref[...]
    s = c * jax.nn.sigmoid(c)
    o_ref[...] = jnp.dot(s, w_ref[...], precision=HI, preferred_element_type=F32) + b_ref[...]


def ada_mods(cond8, w_ada, b_ada):
    depth = w_ada.shape[0]
    tn = 1536
    return pl.pallas_call(
        _ada_kernel,
        grid=(depth, 6 * D // tn),
        in_specs=[pl.BlockSpec((8, D), lambda l, j: (0, 0)),
                  pl.BlockSpec((None, D, tn), lambda l, j: (l, 0, j)),
                  pl.BlockSpec((None, 1, tn), lambda l, j: (l, 0, j))],
        out_specs=pl.BlockSpec((None, 8, tn), lambda l, j: (l, 0, j)),
        out_shape=jax.ShapeDtypeStruct((depth, 8, 6 * D), F32),
        compiler_params=_cparams(("arbitrary", "arbitrary")),
        name="ada_mods",
    )(cond8, w_ada, b_ada.reshape(depth, 1, 6 * D))


def _rms_mod(x, g, sc, sh):
    ms = jnp.mean(x * x, axis=-1, keepdims=True)
    return (x * lax.rsqrt(ms + EPS) * g) * (1.0 + sc) + sh


def _in_kernel(x_ref, mod_ref, g_ref, w_ref, o_ref):
    h = _rms_mod(x_ref[...], g_ref[...], mod_ref[:, D:2 * D], mod_ref[:, 0:D])
    o_ref[...] = jnp.dot(h.astype(BF16), w_ref[...], preferred_element_type=F32)


def in_proj(x, mods, g, w_bf16, tok_per_mod):
    t = x.shape[0]
    tm = 256
    return pl.pallas_call(
        _in_kernel,
        grid=(t // tm,),
        in_specs=[pl.BlockSpec((tm, D), lambda i: (i, 0)),
                  pl.BlockSpec((None, 1, 6 * D), lambda i: (i * tm // tok_per_mod, 0, 0)),
                  pl.BlockSpec((1, D), lambda i: (0, 0)),
                  pl.BlockSpec((D, P_W), lambda i: (0, 0))],
        out_specs=pl.BlockSpec((tm, P_W), lambda i: (i, 0)),
        out_shape=jax.ShapeDtypeStruct((t, P_W), F32),
        compiler_params=_cparams(("arbitrary",)),
        name="in_proj",
    )(x, mods, g, w_bf16)


def _out_kernel(ya_ref, yb_ref, yc_ref, yd_ref, x_ref, mod_ref, g_ref, w_ref, rw_ref, rb_ref,
                x1_ref, h2_ref, eidx_ref, ew_ref, cnt_ref):
    y = jnp.concatenate([ya_ref[...], yb_ref[...], yc_ref[...], yd_ref[...]], axis=1).astype(BF16)
    mix = jnp.dot(y, w_ref[...], preferred_element_type=F32)
    x1 = x_ref[...] + mod_ref[:, 2 * D:3 * D] * mix
    x1_ref[...] = x1
    h2 = _rms_mod(x1, g_ref[...], mod_ref[:, 4 * D:5 * D], mod_ref[:, 3 * D:4 * D])
    h2_ref[...] = h2.astype(BF16)
    lg = lax.dot_general(rw_ref[...], h2, (((1,), (1,)), ((), ())), precision=HI,
                         preferred_element_type=F32) + rb_ref[:, 0:1]
    e_iota = lax.broadcasted_iota(jnp.int32, lg.shape, 0)
    vals, idxs = [], []
    for _ in range(TOP_K):
        m = jnp.max(lg, axis=0, keepdims=True)
        idx = jnp.min(jnp.where(lg == m, e_iota, N_EXP), axis=0, keepdims=True)
        vals.append(m)
        idxs.append(idx)
        lg = jnp.where(e_iota == idx, -jnp.inf, lg)
    ex = [jnp.exp(v - vals[0]) for v in vals]
    den = ex[0] + ex[1] + ex[2] + ex[3]
    eidx_ref[...] = jnp.concatenate(idxs, axis=0)
    ew_ref[...] = jnp.concatenate([e / den for e in ex], axis=0)
    ind = jnp.zeros(lg.shape, F32)
    for idx in idxs:
        ind += (e_iota == idx).astype(F32)
    cnt_ref[...] = jnp.broadcast_to(jnp.sum(ind, axis=1, keepdims=True), (N_EXP, LANES))


def out_proj_route(ys, x, mods, g, w_bf16, rw_t, rb, tok_per_mod):
    t = x.shape[0]
    nb = t // TB
    yspec = pl.BlockSpec((TB, 256), lambda i: (i, 0))
    return pl.pallas_call(
        _out_kernel,
        grid=(nb,),
        in_specs=[yspec, yspec, yspec, yspec,
                  pl.BlockSpec((TB, D), lambda i: (i, 0)),
                  pl.BlockSpec((None, 1, 6 * D), lambda i: (i * TB // tok_per_mod, 0, 0)),
                  pl.BlockSpec((1, D), lambda i: (0, 0)),
                  pl.BlockSpec((D, D), lambda i: (0, 0)),
                  pl.BlockSpec((N_EXP, D), lambda i: (0, 0)),
                  pl.BlockSpec((N_EXP, LANES), lambda i: (0, 0))],
        out_specs=[pl.BlockSpec((TB, D), lambda i: (i, 0)),
                   pl.BlockSpec((TB, D), lambda i: (i, 0)),
                   pl.BlockSpec((None, TOP_K, TB), lambda i: (i, 0, 0)),
                   pl.BlockSpec((None, TOP_K, TB), lambda i: (i, 0, 0)),
                   pl.BlockSpec((None, N_EXP, LANES), lambda i: (i, 0, 0))],
        out_shape=[jax.ShapeDtypeStruct((t, D), F32),
                   jax.ShapeDtypeStruct((t, D), BF16),
                   jax.ShapeDtypeStruct((nb, TOP_K, TB), jnp.int32),
                   jax.ShapeDtypeStruct((nb, TOP_K, TB), F32),
                   jax.ShapeDtypeStruct((nb, N_EXP, LANES), F32)],
        compiler_params=_cparams(("arbitrary",)),
        name="out_proj_route",
    )(*ys, x, mods, g, w_bf16, rw_t, rb)


def moe_tables(cnt, n_tiles):
    nb = cnt.shape[0]
    cnt8 = (cnt + CH - 1) // CH * CH
    ends = jnp.cumsum(cnt8, axis=1)
    off = ends - cnt8
    nchunks = ends[:, -1] // CH
    tot = jnp.sum(cnt8, axis=0)
    totp = (tot + TM - 1) // TM * TM
    eend = jnp.cumsum(totp)
    estart = eend - totp
    gdst = estart[None, :] + jnp.cumsum(cnt8, axis=0) - cnt8
    r = jnp.arange(NCH, dtype=jnp.int32) * CH
    e_of_c = jnp.minimum(jnp.sum((ends[:, None, :] <= r[None, :, None]).astype(jnp.int32), axis=-1), N_EXP - 1)
    pick = e_of_c[:, :, None] == jnp.arange(N_EXP, dtype=jnp.int32)[None, None, :]
    g_of_c = jnp.sum(jnp.where(pick, gdst[:, None, :], 0), axis=-1)
    o_of_c = jnp.sum(jnp.where(pick, off[:, None, :], 0), axis=-1)
    gchunk = (g_of_c + r[None, :] - o_of_c) // CH
    nused = eend[-1] // TM
    ti = jnp.arange(n_tiles, dtype=jnp.int32)
    tile_e = jnp.sum((eend[None, :] // TM <= jnp.minimum(ti, nused - 1)[:, None]).astype(jnp.int32), axis=-1)
    tile_e = jnp.minimum(tile_e, N_EXP - 1)
    has = totp > 0
    ei = jnp.arange(N_EXP, dtype=jnp.int32)
    later = has[None, :] & (ei[None, :] > ei[:, None])
    next_of = jnp.min(jnp.where(later, ei[None, :], N_EXP), axis=1)
    next_of = jnp.where(next_of == N_EXP, -1, next_of)
    ordinal = jnp.cumsum(has.astype(jnp.int32)) - 1
    tile_next = jnp.sum(jnp.where(tile_e[:, None] == ei[None, :], next_of[None, :], 0), axis=1)
    tile_slot = jnp.sum(jnp.where(tile_e[:, None] == ei[None, :], ordinal[None, :], 0), axis=1) % 2
    i32 = lambda a: a.astype(jnp.int32)
    return dict(off=off, gchunk=i32(gchunk.reshape(-1)), nchunks=i32(nchunks), tile_e=i32(tile_e),
                tile_next=i32(tile_next), tile_slot=i32(tile_slot), nused=i32(nused.reshape(1)),
                pad_lo=i32((estart + tot) // CH), pad_hi=i32(eend // CH))


def _dest_rows(eidx, off_col):
    e_iota = lax.broadcasted_iota(jnp.int32, (N_EXP, TB), 0)
    ohs = [e_iota == eidx[k:k + 1, :] for k in range(TOP_K)]
    ind = jnp.zeros((N_EXP, TB), F32)
    for oh in ohs:
        ind += oh.astype(F32)
    ti = lax.broadcasted_iota(jnp.int32, (TB, TB), 0)
    tj = lax.broadcasted_iota(jnp.int32, (TB, TB), 1)
    upper = (ti <= tj).astype(BF16)
    rank_incl = jnp.dot(ind.astype(BF16), upper, preferred_element_type=F32)
    base = off_col + rank_incl - ind
    return [jnp.sum(jnp.where(oh, base, 0.0), axis=0, keepdims=True).astype(jnp.int32) for oh in ohs]


def _dispatch_kernel(gchunk_ref, nch_ref, plo_ref, phi_ref, nused_ref, h2_ref, eidx_ref, off_ref, xs_ref,
                     xbs_ref, zbuf_ref, sem, zsem):
    blk = pl.program_id(0)
    last = pl.num_programs(0) - 1
    n_tiles = xs_ref.shape[0] // TM

    def zero_fill(start):
        def pad_chunk(c, carry):
            cp = pltpu.make_async_copy(zbuf_ref.at[pl.ds(0, CH), :],
                                       xs_ref.at[pl.ds(pl.multiple_of(c * CH, CH), CH), :], zsem)
            cp.start() if start else cp.wait()
            return carry

        def per_expert(e, carry):
            lax.fori_loop(plo_ref[e], phi_ref[e], pad_chunk, 0)
            return carry

        def tail_tile(t, carry):
            cp = pltpu.make_async_copy(zbuf_ref, xs_ref.at[pl.ds(pl.multiple_of(t * TM, TM), TM), :], zsem)
            cp.start() if start else cp.wait()
            return carry

        lax.fori_loop(0, N_EXP, per_expert, 0)
        lax.fori_loop(nused_ref[0], n_tiles, tail_tile, 0)

    @pl.when(blk == 0)
    def _():
        zbuf_ref[...] = jnp.zeros(zbuf_ref.shape, F32)
        zero_fill(True)

    dests = _dest_rows(eidx_ref[...], off_ref[:, 0:1])
    p_iota = lax.broadcasted_iota(jnp.int32, (ROWS_BS, TB), 0)
    perm = jnp.zeros((ROWS_BS, TB), F32)
    for d in dests:
        perm += (p_iota == d).astype(F32)
    slot = blk % 2
    xbs_ref[slot] = jnp.dot(perm.astype(BF16), h2_ref[...], preferred_element_type=F32)

    def copy(b, c):
        dst = pl.multiple_of(gchunk_ref[b * NCH + c] * CH, CH)
        src = pl.multiple_of(c * CH, CH)
        return pltpu.make_async_copy(xbs_ref.at[b % 2, pl.ds(src, CH), :], xs_ref.at[pl.ds(dst, CH), :],
                                     sem.at[b % 2])

    def start(c, carry):
        copy(blk, c).start()
        return carry

    lax.fori_loop(0, nch_ref[blk], start, 0)

    def wait_block(b):
        rows = nch_ref[b] * CH
        pltpu.make_async_copy(xbs_ref.at[b % 2, pl.ds(0, rows), :], xs_ref.at[pl.ds(0, rows), :],
                              sem.at[b % 2]).wait()

    @pl.when(blk > 0)
    def _():
        wait_block(blk - 1)

    @pl.when(blk == last)
    def _():
        wait_block(blk)
        zero_fill(False)


def moe_dispatch(h2, eidx, off_b, tb, rows_alloc):
    nb = h2.shape[0] // TB
    return pl.pallas_call(
        _dispatch_kernel,
        grid_spec=pltpu.PrefetchScalarGridSpec(
            num_scalar_prefetch=5,
            grid=(nb,),
            in_specs=[pl.BlockSpec((TB, D), lambda i, *_: (i, 0)),
                      pl.BlockSpec((None, TOP_K, TB), lambda i, *_: (i, 0, 0)),
                      pl.BlockSpec((None, N_EXP, LANES), lambda i, *_: (i, 0, 0))],
            out_specs=pl.BlockSpec(memory_space=pl.ANY),
            scratch_shapes=[pltpu.VMEM((2, ROWS_BS, D), F32), pltpu.VMEM((TM, D), F32),
                            pltpu.SemaphoreType.DMA((2,)), pltpu.SemaphoreType.DMA(())]),
        out_shape=jax.ShapeDtypeStruct((rows_alloc, D), F32),
        compiler_params=_cparams(("arbitrary",)),
        name="moe_dispatch",
    )(tb["gchunk"], tb["nchunks"], tb["pad_lo"], tb["pad_hi"], tb["nused"], h2, eidx, off_b)


def _expert_kernel(layer, te_ref, tnext_ref, tslot_ref, nused_ref, x_ref, bgu_ref, bdn_ref, wgu_hbm, wdn_hbm,
                   y_ref, wgu_f32, wdn_f32, wgu_bf, wdn_bf, sem):
    i = pl.program_id(0)

    def fetch(e, slot, start):
        for k, (src, dst) in enumerate(((wgu_hbm, wgu_f32), (wdn_hbm, wdn_f32))):
            cp = pltpu.make_async_copy(src.at[layer, e], dst.at[slot], sem.at[slot, k])
            cp.start() if start else cp.wait()

    @pl.when(i >= nused_ref[0])
    def _():
        y_ref[...] = jnp.zeros(y_ref.shape, F32)

    @pl.when(i < nused_ref[0])
    def _():
        e = te_ref[i]
        slot = tslot_ref[i]
        first = jnp.logical_or(i == 0, e != te_ref[jnp.maximum(i - 1, 0)])

        @pl.when(i == 0)
        def _():
            fetch(e, slot, True)

        @pl.when(first)
        def _():
            fetch(e, slot, False)
            nxt = tnext_ref[i]

            @pl.when(nxt >= 0)
            def _():
                fetch(nxt, 1 - slot, True)

            wgu_bf[...] = wgu_f32[slot].astype(BF16)
            wdn_bf[...] = wdn_f32[slot].astype(BF16)

        gu = jnp.dot(x_ref[...].astype(BF16), wgu_bf[...], preferred_element_type=F32) + bgu_ref[...]
        g = jnp.minimum(gu[:, 0:D_FF], SWIGLU_LIMIT)
        u = jnp.clip(gu[:, D_FF:2 * D_FF], -SWIGLU_LIMIT, SWIGLU_LIMIT)
        act = g * jax.nn.sigmoid(SWIGLU_ALPHA * g) * (u + 1.0)
        y_ref[...] = jnp.dot(act.astype(BF16), wdn_bf[...], preferred_element_type=F32) + bdn_ref[...]


def moe_experts(xs, tb, layer, w_gu, b_gu, w_dn, b_dn):
    n_tiles = xs.shape[0] // TM
    depth = w_gu.shape[0]

    def xmap(i, te, tn, ts, nu):
        return (jnp.minimum(i, jnp.maximum(nu[0] - 1, 0)), 0)

    def bmap(i, te, tn, ts, nu):
        return (layer, te[i], 0, 0)

    return pl.pallas_call(
        functools.partial(_expert_kernel, layer),
        grid_spec=pltpu.PrefetchScalarGridSpec(
            num_scalar_prefetch=4,
            grid=(n_tiles,),
            in_specs=[pl.BlockSpec((TM, D), xmap),
                      pl.BlockSpec((None, None, 1, 2 * D_FF), bmap),
                      pl.BlockSpec((None, None, 1, D), bmap),
                      pl.BlockSpec(memory_space=pl.ANY),
                      pl.BlockSpec(memory_space=pl.ANY)],
            out_specs=pl.BlockSpec((TM, D), lambda i, *_: (i, 0)),
            scratch_shapes=[pltpu.VMEM((2, D, 2 * D_FF), F32), pltpu.VMEM((2, D_FF, D), F32),
                            pltpu.VMEM((D, 2 * D_FF), BF16), pltpu.VMEM((D_FF, D), BF16),
                            pltpu.SemaphoreType.DMA((2, 2))]),
        out_shape=jax.ShapeDtypeStruct((xs.shape[0], D), F32),
        compiler_params=_cparams(("arbitrary",)),
        name="moe_experts",
    )(tb["tile_e"], tb["tile_next"], tb["tile_slot"], tb["nused"], xs,
      b_gu.reshape(depth, N_EXP, 1, 2 * D_FF), b_dn.reshape(depth, N_EXP, 1, D), w_gu, w_dn)


def _combine_kernel(final, gchunk_ref, nch_ref, eidx_ref, ew_ref, off_ref, x1_ref, mod_ref, fg_ref, ys_ref,
                    o_ref, ybs_ref, sem):
    blk = pl.program_id(0)
    n = nch_ref[blk]
    slot = blk % 2

    def copy(b, c):
        src = pl.multiple_of(gchunk_ref[b * NCH + c] * CH, CH)
        dst = pl.multiple_of(c * CH, CH)
        return pltpu.make_async_copy(ys_ref.at[pl.ds(src, CH), :], ybs_ref.at[b % 2, pl.ds(dst, CH), :],
                                     sem.at[b % 2])

    def fetch_block(b):
        def start(c, carry):
            copy(b, c).start()
            return carry
        lax.fori_loop(0, nch_ref[b], start, 0)

    @pl.when(blk == 0)
    def _():
        fetch_block(blk)

    @pl.when(blk + 1 < pl.num_programs(0))
    def _():
        fetch_block(blk + 1)

    dests = _dest_rows(eidx_ref[...], off_ref[:, 0:1])
    ew = ew_ref[...]
    p_iota = lax.broadcasted_iota(jnp.int32, (ROWS_BS, TB), 0)
    perm = jnp.zeros((ROWS_BS, TB), F32)
    gsel = jnp.zeros((ROWS_BS, TB), F32)
    for k, d in enumerate(dests):
        hit = p_iota == d
        perm += hit.astype(F32)
        gsel += jnp.where(hit, ew[k:k + 1, :], 0.0)
    gate_col = jnp.sum(gsel, axis=1, keepdims=True)
    pltpu.make_async_copy(ys_ref.at[pl.ds(0, n * CH), :], ybs_ref.at[slot, pl.ds(0, n * CH), :], sem.at[slot]).wait()
    row_iota = lax.broadcasted_iota(jnp.int32, (ROWS_BS, 1), 0)
    yb = jnp.where(row_iota < n * CH, ybs_ref[slot], 0.0) * gate_col
    moe = lax.dot_general(perm.astype(BF16), yb.astype(BF16), (((0,), (0,)), ((), ())),
                          preferred_element_type=F32)
    x2 = x1_ref[...] + mod_ref[:, 5 * D:6 * D] * moe
    if final:
        ms = jnp.mean(x2 * x2, axis=-1, keepdims=True)
        x2 = x2 * lax.rsqrt(ms + EPS) * fg_ref[...]
    o_ref[...] = x2


def moe_combine(ys, eidx, ew, off_b, x1, mods, fg, gchunk, nchunks, tok_per_mod, final):
    t = x1.shape[0]
    nb = t // TB
    return pl.pallas_call(
        functools.partial(_combine_kernel, final),
        grid_spec=pltpu.PrefetchScalarGridSpec(
            num_scalar_prefetch=2,
            grid=(nb,),
            in_specs=[pl.BlockSpec((None, TOP_K, TB), lambda i, *_: (i, 0, 0)),
                      pl.BlockSpec((None, TOP_K, TB), lambda i, *_: (i, 0, 0)),
                      pl.BlockSpec((None, N_EXP, LANES), lambda i, *_: (i, 0, 0)),
                      pl.BlockSpec((TB, D), lambda i, *_: (i, 0)),
                      pl.BlockSpec((None, 1, 6 * D), lambda i, *_: (i * TB // tok_per_mod, 0, 0)),
                      pl.BlockSpec((1, D), lambda i, *_: (0, 0)),
                      pl.BlockSpec(memory_space=pl.ANY)],
            out_specs=pl.BlockSpec((TB, D), lambda i, *_: (i, 0)),
            scratch_shapes=[pltpu.VMEM((2, ROWS_BS, D), F32), pltpu.SemaphoreType.DMA((2,))]),
        out_shape=jax.ShapeDtypeStruct((t, D), F32),
        compiler_params=_cparams(("arbitrary",)),
        name="moe_combine",
    )(gchunk, nchunks, eidx, ew, off_b, x1, mods, fg, ys)


def moe_layer(h2, eidx, ew, cnt_b, x1, mods, fg, layer, w_gu, b_gu, w_dn, b_dn, tok_per_mod, final):
    t = h2.shape[0]
    nb = t // TB
    max_rows = t * TOP_K + nb * N_EXP * (CH - 1) + N_EXP * (TM - CH)
    n_tiles = (max_rows + TM - 1) // TM
    cnt = cnt_b[:, :, 0].astype(jnp.int32)
    tb = moe_tables(cnt, n_tiles)
    off_b = jnp.broadcast_to(tb["off"].astype(F32)[:, :, None], (nb, N_EXP, LANES))
    xs = moe_dispatch(h2, eidx, off_b, tb, n_tiles * TM)
    ys = moe_experts(xs, tb, layer, w_gu, b_gu, w_dn, b_dn)
    return moe_combine(ys, eidx, ew, off_b, x1, mods, fg, tb["gchunk"], tb["nchunks"], tok_per_mod, final)


N_HEADS = 4


def _stack_heads(x, head_w):
    lane_h = lax.broadcasted_iota(jnp.int32, x.shape, 1) // head_w
    return jnp.concatenate([jnp.where(lane_h == h, x, 0.0) for h in range(N_HEADS)], axis=0)


def _unstack_heads(xs, head_w):
    r = xs.shape[0] // N_HEADS
    lane_h = lax.broadcasted_iota(jnp.int32, (r, xs.shape[1]), 1) // head_w
    out = jnp.zeros((r, xs.shape[1]), F32)
    for h in range(N_HEADS):
        out = jnp.where(lane_h == h, xs[h * r:(h + 1) * r, :], out)
    return out


def _block_diag_mask(rows, cols, rw, cw):
    ri = lax.broadcasted_iota(jnp.int32, (rows, cols), 0) // rw
    ci = lax.broadcasted_iota(jnp.int32, (rows, cols), 1) // cw
    return ri == ci


def _head_rmsnorm(o, head_w):
    n = o.shape[1]
    bd = _block_diag_mask(n, n, head_w, head_w).astype(F32)
    ms = jnp.dot(o * o, bd, precision=HI, preferred_element_type=F32) * (1.0 / head_w)
    return o * lax.rsqrt(ms + EPS)


def _nt(a, b, **kw):
    return lax.dot_general(a, b, (((1,), (1,)), ((), ())), preferred_element_type=F32, **kw)


def _tn(a, b, **kw):
    return lax.dot_general(a, b, (((0,), (0,)), ((), ())), preferred_element_type=F32, **kw)


NA_DH = 64
NA_WIN_R = 8
NA_WIN_C = 16
NA_ROWS = 64


NA_RPS = 2


def _na_window_start(r):
    return jnp.clip(r - NA_WIN_R // 2, 0, NA_ROWS - NA_WIN_R)


def _na_lat_kernel(q_ref, k_ref, v_ref, kc_ref, vc_ref, *refs):
    bias_refs, o_ref = refs[:NA_RPS], refs[NA_RPS]
    kc = kc_ref[...].astype(BF16)
    vc = vc_ref[...].astype(BF16)
    for t in range(NA_RPS):
        start = _na_window_start(pl.program_id(1) * NA_RPS + t)
        rows = pl.ds(pl.multiple_of(start * GRID_W, GRID_W), NA_WIN_R * GRID_W)
        tok = slice(t * GRID_W, (t + 1) * GRID_W)
        qs = _stack_heads(q_ref[tok, :] * (NA_DH ** -0.5), NA_DH).astype(BF16)
        k_all = jnp.concatenate([k_ref[rows, :].astype(BF16), kc], axis=0)
        v_all = jnp.concatenate([v_ref[rows, :].astype(BF16), vc], axis=0)
        s = _nt(qs, k_all) + bias_refs[t][...]
        m = jnp.max(s, axis=1, keepdims=True)
        p = jnp.exp(s - m)
        den = jnp.sum(p, axis=1, keepdims=True)
        o = jnp.dot(p.astype(BF16), v_all, preferred_element_type=F32)
        o_ref[tok, :] = _unstack_heads(o / den, NA_DH)


def _na_bias_table(rpb):
    col = np.arange(GRID_W)
    c_start = np.clip(col - NA_WIN_C // 2, 0, GRID_W - NA_WIN_C)
    col_mask = (col[None, :] >= c_start[:, None]) & (col[None, :] < c_start[:, None] + NA_WIN_C)
    c_idx = np.clip(col[None, :] - col[:, None], -(NA_WIN_C - 1), NA_WIN_C - 1) + (NA_WIN_C - 1)
    onehot = jnp.asarray(c_idx[None, :, :] == np.arange(2 * NA_WIN_C - 1)[:, None, None], F32)
    tb = jnp.einsum('hrc,cqk->hrqk', rpb, onehot, precision=HI)
    tb = jnp.where(col_mask[None, None], tb, -jnp.inf)
    out = []
    for ri0 in range(NA_WIN_R):
        blk = tb[:, ri0:ri0 + NA_WIN_R]
        out.append(blk.transpose(0, 2, 1, 3).reshape(N_HEADS * GRID_W, NA_WIN_R * GRID_W))
    return jnp.stack(out, 0)


def na_latent(proj, tok0, bs, ls, kc, vc, rpb):
    lc = kc.shape[1]
    bias = _na_bias_table(rpb)
    bias = jnp.concatenate([bias, jnp.zeros(bias.shape[:2] + (lc,), F32)], axis=-1)
    nstep = ls // GRID_W // NA_RPS
    tq = NA_RPS * GRID_W
    rb0 = tok0 // tq
    sb0 = tok0 // ls

    def bias_spec(t):
        def imap(b, s):
            r = s * NA_RPS + t
            return (_na_window_start(r) - r + NA_WIN_R - 1, 0, 0)
        return pl.BlockSpec((None, N_HEADS * GRID_W, NA_WIN_R * GRID_W + lc), imap)

    return pl.pallas_call(
        _na_lat_kernel,
        grid=(bs, nstep),
        in_specs=[pl.BlockSpec((tq, 256), lambda b, s: (rb0 + b * nstep + s, P_NQ // 256)),
                  pl.BlockSpec((ls, 256), lambda b, s: (sb0 + b, P_NK // 256)),
                  pl.BlockSpec((ls, 256), lambda b, s: (sb0 + b, P_NV // 256)),
                  pl.BlockSpec((None, lc, 256), lambda b, s: (b, 0, 0)),
                  pl.BlockSpec((None, lc, 256), lambda b, s: (b, 0, 0))] + [bias_spec(t) for t in range(NA_RPS)],
        out_specs=pl.BlockSpec((tq, 256), lambda b, s: (b * nstep + s, 0)),
        out_shape=jax.ShapeDtypeStruct((bs * ls, 256), F32),
        compiler_params=_cparams(("arbitrary", "arbitrary")),
        name="na_latent",
    )(proj, proj, proj, kc, vc, *([bias] * NA_RPS))


def _na_ctx_kernel(q_ref, k_ref, v_ref, o_ref, kc_ref, vc_ref):
    qs = _stack_heads(q_ref[...] * (NA_DH ** -0.5), NA_DH).astype(BF16)
    k = k_ref[...]
    v = v_ref[...]
    s = _nt(qs, k.astype(BF16))
    m = jnp.max(s, axis=1, keepdims=True)
    p = jnp.exp(s - m)
    den = jnp.sum(p, axis=1, keepdims=True)
    o = jnp.dot(p.astype(BF16), v.astype(BF16), preferred_element_type=F32)
    o_ref[...] = _unstack_heads(o / den, NA_DH)
    for h in range(N_HEADS):
        kc_ref[h] = k[:, h * NA_DH:(h + 1) * NA_DH]
        vc_ref[h] = v[:, h * NA_DH:(h + 1) * NA_DH]


def na_context(proj, bp, lp):
    cache_spec = pl.BlockSpec((None, N_HEADS, lp, NA_DH), lambda b: (b, 0, 0, 0))
    cache_shape = jax.ShapeDtypeStruct((bp, N_HEADS, lp, NA_DH), F32)
    return pl.pallas_call(
        _na_ctx_kernel,
        grid=(bp,),
        in_specs=[pl.BlockSpec((lp, 256), lambda b: (b, P_NQ // 256)),
                  pl.BlockSpec((lp, 256), lambda b: (b, P_NK // 256)),
                  pl.BlockSpec((lp, 256), lambda b: (b, P_NV // 256))],
        out_specs=[pl.BlockSpec((lp, 256), lambda b: (b, 0)), cache_spec, cache_spec],
        out_shape=[jax.ShapeDtypeStruct((bp * lp, 256), F32), cache_shape, cache_shape],
        compiler_params=_cparams(("arbitrary",)),
        name="na_context",
    )(proj, proj, proj)


GLA_DK = 32
GLA_DV = 64
GLA_C = 128
GLA_NORMALIZER = 16.0


def _gla_kernel(has_state, seq, *refs):
    if has_state:
        q_ref, k_ref, v_ref, g_ref, aux_ref, wg_ref, bg_ref, gn_ref, s0_ref, y_ref, acc_ref, bcum_ref = refs
    else:
        q_ref, k_ref, v_ref, g_ref, aux_ref, wg_ref, bg_ref, gn_ref, y_ref, sfin_ref, acc_ref, bcum_ref = refs
    c_sz = min(GLA_C, seq)
    n_chunks = seq // c_sz
    mid = c_sz // 2
    hk = N_HEADS * GLA_DK
    ti = lax.broadcasted_iota(jnp.int32, (c_sz, c_sz), 0)
    tj = lax.broadcasted_iota(jnp.int32, (c_sz, c_sz), 1)
    ai = lax.broadcasted_iota(jnp.int32, (N_HEADS * c_sz, c_sz), 0) % c_sz
    aj = lax.broadcasted_iota(jnp.int32, (N_HEADS * c_sz, c_sz), 1)
    bd = _block_diag_mask(N_HEADS * GLA_DV, hk, GLA_DV, GLA_DK)
    tri2 = jnp.concatenate([(tj <= ti).astype(BF16), (tj >= ti).astype(BF16)], axis=0)
    n_lr = wg_ref.shape[0]

    def pre(c, carry):
        rows = pl.ds(pl.multiple_of(c * c_sz, c_sz), c_sz)
        la = jax.nn.log_sigmoid(jnp.dot(aux_ref[rows, 0:n_lr], wg_ref[...], precision=HI, preferred_element_type=F32)
                                + bg_ref[...]) * (1.0 / GLA_NORMALIZER)
        l1 = la.astype(BF16)
        r1 = la - l1.astype(F32)
        l2 = r1.astype(BF16)
        l3 = (r1 - l2.astype(F32)).astype(BF16)
        bb = (jnp.dot(tri2, l1, preferred_element_type=F32) + jnp.dot(tri2, l2, preferred_element_type=F32)
              + jnp.dot(tri2, l3, preferred_element_type=F32))
        bcum_ref[rows, 0:hk] = bb[0:c_sz, 0:hk]
        bcum_ref[rows, hk:2 * hk] = bb[c_sz:2 * c_sz, hk:2 * hk]
        return carry

    lax.fori_loop(0, n_chunks, pre, 0)

    def step(d, c, st):
        amask = (aj <= ai) if d == 0 else (aj >= ai)
        rows = pl.ds(pl.multiple_of(c * c_sz, c_sz), c_sz)
        q = q_ref[rows, :] * (GLA_DK ** -0.5)
        k = k_ref[rows, :]
        v = v_ref[rows, :].astype(BF16)
        b = bcum_ref[rows, d * hk:(d + 1) * hk]
        btot = b[c_sz - 1:c_sz, :] if d == 0 else b[0:1, :]
        ref = b[mid - 1:mid, :] if d == 0 else b[mid:mid + 1, :]
        qt = q * jnp.exp(b - ref)
        kt = (k * jnp.exp(ref - b)).astype(BF16)
        ke = (k * jnp.exp(btot - b)).astype(BF16)
        a = _nt(_stack_heads(qt, GLA_DK).astype(BF16), kt)
        a = jnp.where(amask, a, 0.0).astype(BF16)
        o = _unstack_heads(jnp.dot(a, v, preferred_element_type=F32), GLA_DV)
        o += _nt((qt * jnp.exp(ref)).astype(BF16), st.astype(BF16))
        acc_ref[rows, :] += o
        upd = _tn(v, ke)
        return st * jnp.exp(btot) + jnp.where(bd, upd, 0.0)

    def body(n, sts):
        return step(0, n, sts[0]), step(1, n_chunks - 1 - n, sts[1])

    acc_ref[...] = jnp.zeros(acc_ref.shape, F32)
    if has_state:
        st0 = (s0_ref[0], s0_ref[1])
    else:
        st0 = (jnp.zeros((N_HEADS * GLA_DV, hk), F32),) * 2
    st_f, st_b = lax.fori_loop(0, n_chunks, body, st0)
    if not has_state:
        sfin_ref[0] = st_f
        sfin_ref[1] = st_b
    gn = gn_ref[...]

    def epi(i, carry):
        rows = pl.ds(pl.multiple_of(i * 256, 256), 256)
        g = g_ref[rows, :]
        y_ref[rows, :] = _head_rmsnorm(acc_ref[rows, :], GLA_DV) * gn * (g * jax.nn.sigmoid(g))
        return carry

    lax.fori_loop(0, seq // 256, epi, 0)


def gla_mixer(proj, tok0, nbatch, seq, w_gate, b_gate, norm_g, s0t):
    has_state = s0t is not None
    sb = tok0 // seq
    hk = N_HEADS * GLA_DK
    wg = jnp.concatenate([w_gate[0], w_gate[1]], axis=1)
    n_lr = wg.shape[0]
    col = lambda w, c0: pl.BlockSpec((seq, w), lambda b: (sb + b, c0 // w), pipeline_mode=pl.Buffered(1))
    in_specs = [col(128, P_GQ), col(128, P_GK), col(256, P_GV), col(256, P_GG), col(128, P_AUX),
                pl.BlockSpec((n_lr, 2 * hk), lambda b: (0, 0)),
                pl.BlockSpec((1, 2 * hk), lambda b: (0, 0)),
                pl.BlockSpec((1, 256), lambda b: (0, 0))]
    args = [proj, proj, proj, proj, proj, wg, b_gate.reshape(1, 2 * hk), jnp.tile(norm_g, N_HEADS).reshape(1, 256)]
    y_spec = pl.BlockSpec((seq, 256), lambda b: (b, 0))
    y_shape = jax.ShapeDtypeStruct((nbatch * seq, 256), F32)
    st_spec = pl.BlockSpec((None, 2, 256, 128), lambda b: (b, 0, 0, 0))
    if has_state:
        in_specs.append(st_spec)
        args.append(s0t)
        out_specs, out_shape = y_spec, y_shape
    else:
        out_specs = [y_spec, st_spec]
        out_shape = [y_shape, jax.ShapeDtypeStruct((nbatch, 2, 256, 128), F32)]
    return pl.pallas_call(
        functools.partial(_gla_kernel, has_state, seq),
        grid=(nbatch,),
        in_specs=in_specs, out_specs=out_specs, out_shape=out_shape,
        scratch_shapes=[pltpu.VMEM((seq, 256), F32), pltpu.VMEM((seq, 2 * hk), F32)],
        compiler_params=_cparams(("arbitrary",)),
        name="gla_latent" if has_state else "gla_context",
    )(*args)


ML_DH = 64
ML_C = 256
ROPE_BASE = 10000.0
ML_GATE_LANE0 = 16


def _ml_gate_selectors():
    rep = np.zeros((2, 2, LANES, N_HEADS * ML_DH), np.float32)
    sel = np.zeros((2, 8, LANES), np.float32)
    for d in range(2):
        for g in range(2):
            for h in range(N_HEADS):
                lane = ML_GATE_LANE0 + d * 8 + g * 4 + h
                rep[d, g, lane, h * ML_DH:(h + 1) * ML_DH] = 1.0
                sel[d, g * 4 + h, lane] = 1.0
    return jnp.asarray(rep), jnp.asarray(sel)


def _rope_tables(seq):
    nf = ML_DH // 4
    inv = ROPE_BASE ** (-jnp.arange(nf, dtype=F32) / nf)
    t = np.arange(seq)
    j = np.arange(ML_DH)
    pos = np.where(j[None, :] < ML_DH // 2, (t // GRID_W)[:, None], (t % GRID_W)[:, None]).astype(np.float32)
    ang = jnp.asarray(pos) * inv[j % nf][None, :]
    first = (j % (ML_DH // 2)) < nf
    cos = jnp.tile(jnp.cos(ang), (1, N_HEADS))
    sin = jnp.tile(jnp.where(first[None, :], -jnp.sin(ang), jnp.sin(ang)), (1, N_HEADS))
    return cos, sin


def _rope(x, cos, sin_signed):
    nf = ML_DH // 4
    first = (lax.broadcasted_iota(jnp.int32, x.shape, 1) % (ML_DH // 2)) < nf
    partner = jnp.where(first, pltpu.roll(x, x.shape[1] - nf, 1), pltpu.roll(x, nf, 1))
    return x * cos + partner * sin_signed


def _mlstm_kernel(latent, seq, *refs):
    if latent:
        (q_ref, k_ref, v_ref, og_ref, aux_ref, rep_ref, sel_ref, brep_ref, bsel_ref, gn_ref, cos_ref, sin_ref,
         c0_ref, n0_ref, m0_ref, y_ref, acc_ref) = refs
    else:
        (q_ref, k_ref, v_ref, og_ref, aux_ref, rep_ref, sel_ref, brep_ref, bsel_ref, gn_ref,
         y_ref, cf_ref, nf_ref, mf_ref, acc_ref) = refs
    c_sz = min(ML_C, seq)
    n_chunks = seq // c_sz
    hw = N_HEADS * ML_DH
    ti = lax.broadcasted_iota(jnp.int32, (c_sz, c_sz), 0)
    tj = lax.broadcasted_iota(jnp.int32, (c_sz, c_sz), 1)
    bd = _block_diag_mask(hw, hw, ML_DH, ML_DH)
    for d in range(2):
        causal = (tj <= ti) if d == 0 else (tj >= ti)
        tri = causal.astype(F32)
        tri_t = ((ti <= tj) if d == 0 else (ti >= tj)).astype(F32)

        def body(n, carry, d=d, causal=causal, tri=tri, tri_t=tri_t):
            cm, nrow, mrow = carry
            c = n if d == 0 else n_chunks - 1 - n
            rows = pl.ds(pl.multiple_of(c * c_sz, c_sz), c_sz)
            q = q_ref[rows, :]
            k = k_ref[rows, :] * (ML_DH ** -0.5)
            if latent:
                q = _rope(q, cos_ref[rows, :], sin_ref[rows, :])
                k = _rope(k, cos_ref[rows, :], sin_ref[rows, :])
            v = v_ref[rows, :].astype(BF16)
            aux = aux_ref[rows, :]
            li_m = jnp.dot(aux, rep_ref[d, 0], precision=HI, preferred_element_type=F32) + brep_ref[d, 0]
            lf_m = jax.nn.log_sigmoid(jnp.dot(aux, rep_ref[d, 1], precision=HI, preferred_element_type=F32)
                                      + brep_ref[d, 1])
            f_m = jnp.dot(tri, lf_m, precision=HI, preferred_element_type=F32)
            g_t = _nt(sel_ref[d], aux, precision=HI) + bsel_ref[d][:, 0:1]
            li_t = g_t[0:N_HEADS, :]
            f_t = jnp.dot(jax.nn.log_sigmoid(g_t[N_HEADS:2 * N_HEADS, :]), tri_t, precision=HI,
                          preferred_element_type=F32)
            dms, fcols, mcols = [], [], []
            for h in range(N_HEADS):
                fcol = f_m[:, h * ML_DH:h * ML_DH + 1]
                dms.append(jnp.where(causal, fcol - f_t[h:h + 1, :] + li_t[h:h + 1, :], -jnp.inf))
                fcols.append(fcol)
                mcols.append(jnp.broadcast_to(mrow[:, h * ML_DH:h * ML_DH + 1], (c_sz, 1)))
            dm = jnp.concatenate(dms, axis=0)
            log_inter = jnp.concatenate(fcols, axis=0) + jnp.concatenate(mcols, axis=0)
            m_t = jnp.maximum(log_inter, jnp.max(dm, axis=1, keepdims=True))
            qs = _stack_heads(q, ML_DH)
            qsb = qs.astype(BF16)
            s = _nt(qsb, k.astype(BF16)) * jnp.exp(dm - m_t)
            a_t = jnp.exp(log_inter - m_t)
            inter = jnp.dot(qsb, cm.astype(BF16), preferred_element_type=F32)
            v_ext = jnp.concatenate([v, jnp.ones((c_sz, LANES), BF16)], axis=1)
            sv = jnp.dot(s.astype(BF16), v_ext, preferred_element_type=F32)
            qn = jnp.dot((qs * nrow).astype(BF16), jnp.ones((hw, LANES), BF16), preferred_element_type=F32)
            num = a_t * inter + sv[:, 0:hw]
            den = a_t * qn[:, 0:1] + sv[:, hw:hw + 1]
            hst = num / jnp.maximum(jnp.abs(den), jnp.exp(-m_t))
            hout = _unstack_heads(hst, ML_DH)
            if d == 0:
                acc_ref[rows, :] = hout
            else:
                acc_ref[rows, :] += hout
            f_tot = f_m[c_sz - 1:c_sz, :] if d == 0 else f_m[0:1, :]
            w_end = f_tot - f_m + li_m
            m_new = jnp.maximum(f_tot + mrow, jnp.max(w_end, axis=0, keepdims=True))
            a = jnp.exp(f_tot + mrow - m_new)
            kw = k * jnp.exp(w_end - m_new)
            cm_new = cm * a + jnp.where(bd, _tn(kw.astype(BF16), v), 0.0)
            n_new = nrow * a + jnp.sum(kw, axis=0, keepdims=True)
            return cm_new, n_new, m_new

        if latent:
            init = (c0_ref[d], n0_ref[d], m0_ref[d])
        else:
            init = (jnp.zeros((hw, hw), F32), jnp.zeros((1, hw), F32), jnp.zeros((1, hw), F32))
        cm, nrow, mrow = lax.fori_loop(0, n_chunks, body, init)
        if not latent:
            cf_ref[d] = cm
            nf_ref[d] = nrow
            mf_ref[d] = mrow
    gn = gn_ref[...]

    def epi(i, carry):
        rows = pl.ds(pl.multiple_of(i * 256, 256), 256)
        y_ref[rows, :] = _head_rmsnorm(acc_ref[rows, :], ML_DH) * gn * jax.nn.sigmoid(og_ref[rows, :])
        return carry

    lax.fori_loop(0, seq // 256, epi, 0)


def mlstm_mixer(proj, tok0, nbatch, seq, b_gate, norm_g, state):
    latent = state is not None
    sb = tok0 // seq
    hw = N_HEADS * ML_DH
    rep, sel = _ml_gate_selectors()
    brep = jnp.repeat(b_gate.reshape(2, 2, N_HEADS), ML_DH, axis=-1).reshape(2, 2, 1, hw)
    bsel = jnp.broadcast_to(b_gate.reshape(2, 8, 1), (2, 8, LANES))
    col = lambda c0: pl.BlockSpec((seq, 256), lambda b: (sb + b, c0 // 256), pipeline_mode=pl.Buffered(1))
    full = lambda shape: pl.BlockSpec(shape, lambda b: (0,) * len(shape), pipeline_mode=pl.Buffered(1))
    in_specs = [col(P_MQ), col(P_MK), col(P_MV), col(P_MO),
                pl.BlockSpec((seq, 128), lambda b: (sb + b, P_AUX // 128), pipeline_mode=pl.Buffered(1)),
                full((2, 2, LANES, hw)), full((2, 8, LANES)), full((2, 2, 1, hw)), full((2, 8, LANES)), full((1, hw))]
    args = [proj, proj, proj, proj, proj, rep, sel, brep, bsel, norm_g.reshape(1, hw)]
    y_spec = pl.BlockSpec((seq, 256), lambda b: (b, 0))
    y_shape = jax.ShapeDtypeStruct((nbatch * seq, 256), F32)
    c_spec = pl.BlockSpec((None, 2, hw, hw), lambda b: (b, 0, 0, 0))
    r_spec = pl.BlockSpec((None, 2, 1, hw), lambda b: (b, 0, 0, 0))
    if latent:
        cos, sin = _rope_tables(seq)
        in_specs += [full((seq, hw)), full((seq, hw)), c_spec, r_spec, r_spec]
        args += [cos, sin, *state]
        out_specs, out_shape = y_spec, y_shape
    else:
        out_specs = [y_spec, c_spec, r_spec, r_spec]
        out_shape = [y_shape, jax.ShapeDtypeStruct((nbatch, 2, hw, hw), F32),
                     jax.ShapeDtypeStruct((nbatch, 2, 1, hw), F32), jax.ShapeDtypeStruct((nbatch, 2, 1, hw), F32)]
    return pl.pallas_call(
        functools.partial(_mlstm_kernel, latent, seq),
        grid=(nbatch,),
        in_specs=in_specs, out_specs=out_specs, out_shape=out_shape,
        scratch_shapes=[pltpu.VMEM((seq, 256), F32)],
        compiler_params=_cparams(("arbitrary",)),
        name="mlstm_latent" if latent else "mlstm_context",
    )(*args)


HY_CH = 256
HY_BANDS = 16
HY_EMB = 1 + 2 * HY_BANDS
HY_FFN = 64
FFT_N1 = 64
FFT_N2 = 128


HY_LANE_FWD = HY_EMB
HY_LANE_BWD = HY_EMB + 1


def _hy_filter_kernel(slab, feat_ref, w1_ref, b1_ref, w2_ref, b2_ref, w3_ref, b3_ref, fr_ref, dl_ref, o_ref):
    feats = feat_ref[...]
    a = jnp.sin(fr_ref[0:1, :] * (jnp.dot(feats, w1_ref[...], precision=HI, preferred_element_type=F32) + b1_ref[...]))
    a = jnp.sin(fr_ref[1:2, :] * (jnp.dot(a, w2_ref[...], precision=HI, preferred_element_type=F32) + b2_ref[...]))
    a = jnp.dot(a, w3_ref[...], precision=HI, preferred_element_type=F32) + b3_ref[...]
    a = a * jnp.exp(-feats[:, 0:1] * dl_ref[...])
    fwd = feats[:, HY_LANE_FWD:HY_LANE_FWD + 1]
    bwd = feats[:, HY_LANE_BWD:HY_LANE_BWD + 1]
    for order in range(2):
        h = (fwd * a[:, (2 * order) * HY_CH:(2 * order + 1) * HY_CH]
             + bwd * a[:, (2 * order + 1) * HY_CH:(2 * order + 2) * HY_CH])
        if slab:
            for j in range(h.shape[0] // slab):
                o_ref[order, :, j, :] = h[j * slab:(j + 1) * slab, :]
        else:
            o_ref[order] = h


def hyena_filters(seq, w1, b1, w2, b2, w3, b3, freq, slab=0):
    n = jnp.arange(2 * seq)
    t = jnp.where(n < seq, n, 2 * seq - n).astype(F32)
    t = jnp.where(n == seq, 0.0, t)
    t_unit = t / (seq - 1)
    bands = jnp.linspace(1e-4, HY_BANDS - 1, HY_BANDS, dtype=F32)
    ang = (2.0 * math.pi / seq) * t[:, None] * bands[None, :]
    feats = jnp.concatenate([t_unit[:, None], jnp.cos(ang), -jnp.sin(ang),
                             (n < seq).astype(F32)[:, None], (n > seq).astype(F32)[:, None],
                             jnp.zeros((2 * seq, LANES - HY_EMB - 2), F32)], axis=-1)
    w1p = jnp.zeros((LANES, HY_FFN), F32).at[0:HY_EMB].set(w1)
    deltas = jnp.abs(jnp.linspace(math.log(1e-2) / 1.5, math.log(1e-2) / 0.3, HY_CH, dtype=F32))
    rb = min(2 * seq, 1024 if slab else 512)
    full = lambda shape: pl.BlockSpec(shape, lambda i: (0,) * len(shape))
    if slab:
        out_spec = pl.BlockSpec((2, slab, rb // slab, HY_CH), lambda i: (0, 0, i, 0))
        out_shape = jax.ShapeDtypeStruct((2, slab, 2 * seq // slab, HY_CH), F32)
    else:
        out_spec = pl.BlockSpec((2, rb, HY_CH), lambda i: (0, i, 0))
        out_shape = jax.ShapeDtypeStruct((2, 2 * seq, HY_CH), F32)
    return pl.pallas_call(
        functools.partial(_hy_filter_kernel, slab),
        grid=(2 * seq // rb,),
        in_specs=[pl.BlockSpec((rb, LANES), lambda i: (i, 0)), full((LANES, HY_FFN)), full((1, HY_FFN)),
                  full((HY_FFN, HY_FFN)), full((1, HY_FFN)), full((HY_FFN, 4 * HY_CH)), full((1, 4 * HY_CH)),
                  full((2, HY_FFN)), full((1, 4 * HY_CH))],
        out_specs=out_spec, out_shape=out_shape,
        compiler_params=_cparams(("arbitrary",)),
        name="hyena_filters",
    )(feats, w1p, b1.reshape(1, -1), w2, b2.reshape(1, -1), w3, b3.reshape(1, -1), freq,
      jnp.tile(deltas, 4).reshape(1, -1))


def _hy_short_kernel(nblk, slab, u_ref, prev_ref, next_ref, w_ref, b_ref, x1_ref, x2_ref, z_ref):
    i = pl.program_id(1)
    u = u_ref[...]
    rb = u.shape[0]
    row = lax.broadcasted_iota(jnp.int32, u.shape, 0)
    prev_row = jnp.where(i > 0, prev_ref[SUBLANES - 1:SUBLANES, :], 0.0)
    next_row = jnp.where(i < nblk - 1, next_ref[0:1, :], 0.0)
    up = jnp.where(row == 0, prev_row, pltpu.roll(u, 1, 0))
    un = jnp.where(row == rb - 1, next_row, pltpu.roll(u, rb - 1, 0))
    y = up * w_ref[0:1, :] + u * w_ref[1:2, :] + un * w_ref[2:3, :] + b_ref[...]
    for k, o_ref in enumerate((x1_ref, x2_ref, z_ref)):
        if slab:
            for a in range(rb // slab):
                o_ref[:, a, :] = y[a * slab:(a + 1) * slab, k * HY_CH:(k + 1) * HY_CH]
        else:
            o_ref[...] = y[:, k * HY_CH:(k + 1) * HY_CH]


def hyena_short_conv(proj, tok0, nbatch, seq, w, b, slab=0):
    rb = min(seq, 1024 if slab else 512)
    nblk = seq // rb
    r0 = tok0 // rb
    h0 = tok0 // SUBLANES
    hpb = rb // SUBLANES
    last = (tok0 + nbatch * seq) // SUBLANES - 1
    if slab:
        o_spec = pl.BlockSpec((None, slab, rb // slab, HY_CH), lambda bb, i: (bb, 0, i, 0))
        o_shape = jax.ShapeDtypeStruct((nbatch, slab, seq // slab, HY_CH), F32)
    else:
        o_spec = pl.BlockSpec((rb, HY_CH), lambda bb, i: (bb * nblk + i, 0))
        o_shape = jax.ShapeDtypeStruct((nbatch * seq, HY_CH), F32)
    return pl.pallas_call(
        functools.partial(_hy_short_kernel, nblk, slab),
        grid=(nbatch, nblk),
        in_specs=[pl.BlockSpec((rb, 3 * HY_CH), lambda bb, i: (r0 + bb * nblk + i, P_HU // (3 * HY_CH))),
                  pl.BlockSpec((SUBLANES, 3 * HY_CH),
                               lambda bb, i: (jnp.maximum(h0 + (bb * nblk + i) * hpb - 1, 0), P_HU // (3 * HY_CH))),
                  pl.BlockSpec((SUBLANES, 3 * HY_CH),
                               lambda bb, i: (jnp.minimum(h0 + (bb * nblk + i + 1) * hpb, last), P_HU // (3 * HY_CH))),
                  pl.BlockSpec((3, 3 * HY_CH), lambda bb, i: (0, 0)),
                  pl.BlockSpec((1, 3 * HY_CH), lambda bb, i: (0, 0))],
        out_specs=[o_spec, o_spec, o_spec],
        out_shape=[o_shape, o_shape, o_shape],
        compiler_params=_cparams(("arbitrary", "arbitrary")),
        name="hyena_short_conv",
    )(proj, proj, proj, w, b.reshape(1, -1))


def _dft_consts_single(seq):
    n = 2 * seq
    k = np.arange(n)[:, None].astype(np.float64)
    m = np.arange(n)[None, :].astype(np.float64)
    ang = 2.0 * np.pi * k * m / n
    fwd = np.concatenate([np.cos(ang), -np.sin(ang)], axis=0)
    inv = np.concatenate([np.cos(ang.T[:seq]), -np.sin(ang.T[:seq])], axis=1) / n
    return (jnp.asarray(fwd, F32), jnp.asarray(fwd[:, :seq], F32), jnp.asarray(inv, F32))


def _cmul(zr, zi, hr, hi):
    return zr * hr - zi * hi, zr * hi + zi * hr


def _split_bf16(a):
    hi = a.astype(BF16)
    return hi, (a - hi.astype(F32)).astype(BF16)


def _dot3(a, b, dims=None):
    if dims is None:
        dims = (((a.ndim - 1,), (0,)), ((), ()))
    a_hi, a_lo = _split_bf16(a)
    b_hi, b_lo = _split_bf16(b)
    dg = functools.partial(lax.dot_general, dimension_numbers=dims, preferred_element_type=F32)
    return dg(a_hi, b_hi) + dg(a_lo, b_hi) + dg(a_hi, b_lo)


def _hy_spec_single_kernel(f_ref, g_ref, o_ref):
    o_ref[...] = _dot3(f_ref[...], g_ref[...])


def _hy_conv_single_kernel(x1_ref, x2_ref, z_ref, h_ref, bias_ref, f_ref, i_ref, o_ref):
    n = f_ref.shape[0] // 2
    z = z_ref[...]
    for order, xg_ref in enumerate((x1_ref, x2_ref)):
        zz = _dot3(f_ref[...], z)
        pr, pi = _cmul(zz[0:n], zz[n:2 * n], h_ref[order, 0:n, :], h_ref[order, n:2 * n, :])
        y = _dot3(i_ref[...], jnp.concatenate([pr, pi], axis=0))
        z = xg_ref[...] * (y + z * bias_ref[order])
    o_ref[...] = z


def hyena_context(x1, x2, z, g, bias, nbatch, seq):
    n = 2 * seq
    f_full, f_half, inv = _dft_consts_single(seq)
    spec = pl.pallas_call(
        _hy_spec_single_kernel,
        grid=(2,),
        in_specs=[pl.BlockSpec((2 * n, n), lambda o: (0, 0)), pl.BlockSpec((None, n, HY_CH), lambda o: (o, 0, 0))],
        out_specs=pl.BlockSpec((None, 2 * n, HY_CH), lambda o: (o, 0, 0)),
        out_shape=jax.ShapeDtypeStruct((2, 2 * n, HY_CH), F32),
        compiler_params=_cparams(("arbitrary",)),
        name="hyena_spec_context",
    )(f_full, g)
    blk = pl.BlockSpec((seq, HY_CH), lambda b: (b, 0))
    return pl.pallas_call(
        _hy_conv_single_kernel,
        grid=(nbatch,),
        in_specs=[blk, blk, blk, pl.BlockSpec((2, 2 * n, HY_CH), lambda b: (0, 0, 0)),
                  pl.BlockSpec((2, 1, HY_CH), lambda b: (0, 0, 0)),
                  pl.BlockSpec((2 * n, seq), lambda b: (0, 0)), pl.BlockSpec((seq, 2 * n), lambda b: (0, 0))],
        out_specs=blk,
        out_shape=jax.ShapeDtypeStruct((nbatch * seq, HY_CH), F32),
        compiler_params=_cparams(("arbitrary",)),
        name="hyena_conv_context",
    )(x1, x2, z, spec, bias.reshape(2, 1, HY_CH), f_half, inv)


def _dft_consts_two_stage():
    n1, n2 = FFT_N1, FFT_N2
    n = n1 * n2
    a2 = np.arange(n2, dtype=np.float64)[:, None, None]
    k1 = np.arange(n1, dtype=np.float64)[None, :, None]
    a1 = np.arange(n1, dtype=np.float64)[None, None, :]
    th = 2.0 * np.pi * (a1 * k1 / n1 + a2 * k1 / n)
    w1 = np.concatenate([np.cos(th), -np.sin(th)], axis=1)
    tht = np.transpose(th, (0, 2, 1))
    w3 = np.concatenate([np.cos(tht), -np.sin(tht)], axis=2) / n
    ph = 2.0 * np.pi * np.arange(n2, dtype=np.float64)[:, None] * np.arange(n2, dtype=np.float64)[None, :] / n2
    c, s = np.cos(ph), np.sin(ph)
    g2 = np.block([[c, s], [-s, c]])
    g2i = np.block([[c, -s], [s, c]])
    return (jnp.asarray(w1, F32), jnp.asarray(w3, F32), jnp.asarray(g2, F32), jnp.asarray(g2i, F32))


FFT_SB = 16
FFT_KB = 8


def _dotc(w, x):
    return jnp.dot(w, x.astype(BF16), preferred_element_type=F32)


def _outer_fwd_kernel(x_ref, w_ref, o_ref):
    for jj in range(FFT_SB):
        y = _dotc(w_ref[jj], x_ref[jj])
        o_ref[jj, 0] = y[0:FFT_N1]
        o_ref[jj, 1] = y[FFT_N1:2 * FFT_N1]


def _outer_fwd(w, xs):
    nbatch, n2, n1, ch = xs.shape
    wspec = pl.BlockSpec((FFT_SB, 2 * FFT_N1, n1), lambda b, j: (j, 0, 0))
    return pl.pallas_call(
        _outer_fwd_kernel,
        grid=(nbatch, n2 // FFT_SB),
        in_specs=[pl.BlockSpec((None, FFT_SB, n1, ch), lambda b, j: (b, j, 0, 0)), wspec],
        out_specs=pl.BlockSpec((None, FFT_SB, 2, FFT_N1, ch), lambda b, j: (b, j, 0, 0, 0)),
        out_shape=jax.ShapeDtypeStruct((nbatch, n2, 2, FFT_N1, ch), F32),
        compiler_params=_cparams(("arbitrary", "arbitrary")),
        name="hyena_dft_outer_fwd",
    )(xs, w.astype(BF16))


def _inner_kernel(conv, *refs):
    if conv:
        a_ref, g_ref, h_ref, i_ref, o_ref = refs
    else:
        a_ref, g_ref, o_ref = refs
    n = FFT_N2
    for kk in range(FFT_KB):
        x = jnp.concatenate([a_ref[:, 0, kk, :], a_ref[:, 1, kk, :]], axis=0)
        y = _dotc(g_ref[...], x)
        if conv:
            pr, pi = _cmul(y[0:n], y[n:2 * n], h_ref[kk, 0:n, :], h_ref[kk, n:2 * n, :])
            q = _dotc(i_ref[...], jnp.concatenate([pr, pi], axis=0))
            o_ref[kk, 0] = q[0:n]
            o_ref[kk, 1] = q[n:2 * n]
        else:
            o_ref[kk] = y


def _inner_stage(a5, g2, spec=None, g2i=None):
    nbatch, n2, _, n1, ch = a5.shape
    conv = spec is not None
    mat = pl.BlockSpec((2 * n2, 2 * n2), lambda b, j: (0, 0))
    in_specs = [pl.BlockSpec((None, n2, 2, FFT_KB, ch), lambda b, j: (b, 0, 0, j, 0)), mat]
    args = [a5, g2.astype(BF16)]
    if conv:
        in_specs += [pl.BlockSpec((FFT_KB, 2 * n2, ch), lambda b, j: (j, 0, 0)), mat]
        args += [spec, g2i.astype(BF16)]
        out_spec = pl.BlockSpec((None, FFT_KB, 2, n2, ch), lambda b, j: (b, j, 0, 0, 0))
        out_shape = jax.ShapeDtypeStruct((nbatch, n1, 2, n2, ch), F32)
    else:
        out_spec = pl.BlockSpec((None, FFT_KB, 2 * n2, ch), lambda b, j: (b, j, 0, 0))
        out_shape = jax.ShapeDtypeStruct((nbatch, n1, 2 * n2, ch), F32)
    return pl.pallas_call(
        functools.partial(_inner_kernel, conv),
        grid=(nbatch, n1 // FFT_KB),
        in_specs=in_specs, out_specs=out_spec, out_shape=out_shape,
        compiler_params=_cparams(("arbitrary", "arbitrary")),
        name="hyena_dft_inner_conv" if conv else "hyena_dft_inner_spec",
    )(*args)


def _outer_inv_kernel(to_time_major, q_ref, w_ref, *refs):
    if to_time_major:
        perm_ref, xg_ref, z_ref, b_ref, o_ref = refs
    else:
        xg_ref, z_ref, b_ref, o_ref = refs
    outs = []
    for jj in range(FFT_SB):
        qm = jnp.concatenate([q_ref[:, 0, jj, :], q_ref[:, 1, jj, :]], axis=0)
        g = xg_ref[jj] * (_dotc(w_ref[jj], qm) + z_ref[jj] * b_ref[...])
        if to_time_major:
            outs.append(g)
        else:
            o_ref[jj] = g
    if to_time_major:
        y = jnp.concatenate(outs, axis=0)
        h1 = y.astype(BF16)
        r1 = y - h1.astype(F32)
        h2 = r1.astype(BF16)
        h3 = (r1 - h2.astype(F32)).astype(BF16)
        p = perm_ref[...]
        yp = (jnp.dot(p, h1, preferred_element_type=F32) + jnp.dot(p, h2, preferred_element_type=F32)
              + jnp.dot(p, h3, preferred_element_type=F32))
        o_ref[...] = yp.reshape(o_ref.shape)


def _outer_inv_gate(w, q5, xg, z, bias_row, to_time_major):
    nbatch, n2, n1h, ch = z.shape
    slab = pl.BlockSpec((None, FFT_SB, n1h, ch), lambda b, j: (b, j, 0, 0))
    wspec = pl.BlockSpec((FFT_SB, n1h, 2 * FFT_N1), lambda b, j: (j, 0, 0))
    in_specs = [pl.BlockSpec((None, FFT_N1, 2, FFT_SB, ch), lambda b, j: (b, 0, 0, j, 0)), wspec]
    args = [q5, w.astype(BF16)]
    if to_time_major:
        rows = FFT_SB * n1h
        r = np.arange(rows)
        perm = np.zeros((rows, rows), np.float32)
        perm[r, (r % FFT_SB) * n1h + r // FFT_SB] = 1.0
        in_specs.append(pl.BlockSpec((rows, rows), lambda b, j: (0, 0)))
        args.append(jnp.asarray(perm, BF16))
        out_spec = pl.BlockSpec((None, n1h, FFT_SB, ch), lambda b, j: (b, 0, j, 0))
        out_shape = jax.ShapeDtypeStruct((nbatch, n1h, n2, ch), F32)
    else:
        out_spec, out_shape = slab, jax.ShapeDtypeStruct(z.shape, F32)
    in_specs += [slab, slab, pl.BlockSpec((1, ch), lambda b, j: (0, 0))]
    args += [xg, z, bias_row]
    return pl.pallas_call(
        functools.partial(_outer_inv_kernel, to_time_major),
        grid=(nbatch, n2 // FFT_SB),
        in_specs=in_specs, out_specs=out_spec, out_shape=out_shape,
        compiler_params=_cparams(("arbitrary", "arbitrary")),
        name="hyena_dft_outer_inv_gate",
    )(*args)


def hyena_latent(x1, x2, z, g, bias, nbatch, seq):
    n1, n2 = FFT_N1, FFT_N2
    assert 2 * seq == n1 * n2
    w1, w3, g2, g2i = _dft_consts_two_stage()
    half = seq // n2
    spec = _inner_stage(_outer_fwd(w1, g), g2)
    w1h = w1[:, :, 0:half]
    w3h = w3[:, 0:half, :]
    q5 = _inner_stage(_outer_fwd(w1h, z), g2, spec[0], g2i)
    z1 = _outer_inv_gate(w3h, q5, x1, z, bias[0].reshape(1, HY_CH), False)
    q5 = _inner_stage(_outer_fwd(w1h, z1), g2, spec[1], g2i)
    out = _outer_inv_gate(w3h, q5, x2, z1, bias[1].reshape(1, HY_CH), True)
    return out.reshape(nbatch * seq, HY_CH)


def _permute_w_in(w):
    sizes = (128, 128, 256, 256, 16, 256, 256, 256, 256, 16, 256, 256, 256, 768)
    offs = np.cumsum((0,) + sizes)
    seg = lambda j: w[:, offs[j]:offs[j + 1]]
    order = (13, 0, 1, 2, 3, 5, 6, 7, 8, 10, 11, 12, 4, 9)
    pad = jnp.zeros((w.shape[0], P_W - P_AUX - 32), w.dtype)
    return jnp.concatenate([seg(j) for j in order] + [pad], axis=1)


def kernel(x_prompt, x_sample, cache_na_k, cache_na_v, state_gla, state_mlstm_C, state_mlstm_n, state_mlstm_m, c, c_ctx, w_ada, b_ada, norm1_g, norm2_g, w_in, w_out, gla_w_gate, gla_b_gate, gla_norm_g, ml_b_gate, ml_norm_g, na_rpb, hy_conv_w, hy_conv_b, hy_w1, hy_b1, hy_w2, hy_b2, hy_w3, hy_b3, hy_freq, hy_bias, router_w, router_b, w_gu, b_gu, w_dn, b_dn, final_norm_g):
    depth = w_ada.shape[0]
    bp, lp, _ = x_prompt.shape
    bs, ls, _ = x_sample.shape
    tp = bp * lp
    assert tp == ls, "modulation rows are selected per block of DEC_SEQ tokens"
    x = jnp.concatenate([x_prompt.reshape(tp, D), x_sample.reshape(bs * ls, D)], axis=0)
    cond8 = jnp.concatenate([c_ctx[None, :], c, jnp.zeros((8 - 1 - bs, D), F32)], axis=0)
    mods_all = ada_mods(cond8, w_ada, b_ada)
    fg = final_norm_g.reshape(1, D)
    eye_h = jnp.eye(N_HEADS, dtype=F32)
    lc = cache_na_k.shape[3]
    new_k, new_v, new_gla, new_c, new_n, new_m = [], [], [], [], [], []
    for l in range(depth):
        mods = mods_all[l].reshape(8, 1, 6 * D)
        proj = in_proj(x, mods, norm1_g[l].reshape(1, D), _permute_w_in(w_in[l]).astype(BF16), ls)
        gla_c, gla_fin = gla_mixer(proj, 0, bp, lp, gla_w_gate[l], gla_b_gate[l], gla_norm_g[l], None)
        s0t = jnp.einsum('bdhkv,hg->bdhvgk', state_gla[:, l], eye_h).reshape(bs, 2, 256, 128)
        gla_s = gla_mixer(proj, tp, bs, ls, gla_w_gate[l], gla_b_gate[l], gla_norm_g[l], s0t)
        ml_c, cf, nf, mf = mlstm_mixer(proj, 0, bp, lp, ml_b_gate[l], ml_norm_g[l], None)
        c0 = jnp.einsum('bdhvk,hg->bdhkgv', state_mlstm_C[:, l], eye_h).reshape(bs, 2, 256, 256)
        n0 = state_mlstm_n[:, l].reshape(bs, 2, 1, 256)
        m0 = jnp.repeat(state_mlstm_m[:, l], ML_DH, axis=-1).reshape(bs, 2, 1, 256)
        ml_s = mlstm_mixer(proj, tp, bs, ls, ml_b_gate[l], ml_norm_g[l], (c0, n0, m0))
        na_c, k_l, v_l = na_context(proj, bp, lp)
        kct = cache_na_k[:, l].transpose(0, 2, 1, 3).reshape(bs, lc, 256)
        vct = cache_na_v[:, l].transpose(0, 2, 1, 3).reshape(bs, lc, 256)
        na_s = na_latent(proj, tp, bs, ls, kct, vct, na_rpb[l])
        hy_args = (hy_w1[l], hy_b1[l], hy_w2[l], hy_b2[l], hy_w3[l], hy_b3[l], hy_freq[l])
        hy_c = hyena_context(*hyena_short_conv(proj, 0, bp, lp, hy_conv_w[l], hy_conv_b[l]),
                             hyena_filters(lp, *hy_args), hy_bias[l], bp, lp)
        hy_s = hyena_latent(*hyena_short_conv(proj, tp, bs, ls, hy_conv_w[l], hy_conv_b[l], FFT_N2),
                            hyena_filters(ls, *hy_args, FFT_N2), hy_bias[l], bs, ls)
        ys = [jnp.concatenate(pair, axis=0) for pair in ((gla_c, gla_s), (ml_c, ml_s), (na_c, na_s), (hy_c, hy_s))]
        x1, h2, eidx, ew, cnt_b = out_proj_route(
            ys, x, mods, norm2_g[l].reshape(1, D), w_out[l].astype(BF16), router_w[l].T,
            jnp.broadcast_to(router_b[l][:, None], (N_EXP, LANES)), ls)
        x = moe_layer(h2, eidx, ew, cnt_b, x1, mods, fg, l, w_gu, b_gu, w_dn, b_dn, ls, l == depth - 1)
        new_k.append(k_l)
        new_v.append(v_l)
        gla_blocks = [gla_fin[:, :, h * GLA_DV:(h + 1) * GLA_DV, h * GLA_DK:(h + 1) * GLA_DK] for h in range(N_HEADS)]
        new_gla.append(jnp.stack(gla_blocks, 2).transpose(0, 1, 2, 4, 3))
        c_blocks = [cf[:, :, h * ML_DH:(h + 1) * ML_DH, h * ML_DH:(h + 1) * ML_DH] for h in range(N_HEADS)]
        new_c.append(jnp.stack(c_blocks, 2).transpose(0, 1, 2, 4, 3))
        new_n.append(nf.reshape(bp, 2, N_HEADS, ML_DH))
        new_m.append(mf.reshape(bp, 2, N_HEADS, ML_DH)[..., 0])
    y_prompt = x[:tp].reshape(bp, lp, D)
    y_sample = x[tp:].reshape(bs, ls, D)
    return (y_prompt, y_sample, jnp.stack(new_k, 1), jnp.stack(new_v, 1), jnp.stack(new_gla, 1),
            jnp.stack(new_c, 1), jnp.stack(new_n, 1), jnp.stack(new_m, 1))
```

```python
import functools
import math

import numpy as np
import jax
import jax.numpy as jnp
from jax import lax
from jax.experimental import pallas as pl
from jax.experimental.pallas import tpu as pltpu

F32 = jnp.float32
BF16 = jnp.bfloat16
HI = lax.Precision.HIGHEST

LANES = 128
SUBLANES = 8
VMEM_LIMIT = 56 * 1024 * 1024

D = 1024
EPS = 1e-6
N_EXP = 32
TOP_K = 4
D_FF = 1024
SWIGLU_LIMIT = 7.0
SWIGLU_ALPHA = 1.702
GRID_W = 64

TB = 256
ROWS_BS = TB * TOP_K + N_EXP * SUBLANES
CH = SUBLANES
NCH = ROWS_BS // CH
TM = 512

P_HU = 0
P_GQ, P_GK, P_GV, P_GG = 768, 896, 1024, 1280
P_MQ, P_MK, P_MV, P_MO = 1536, 1792, 2048, 2304
P_NQ, P_NK, P_NV = 2560, 2816, 3072
P_AUX = 3328
P_W = 3456


def _cparams(sem=None):
    return pltpu.CompilerParams(dimension_semantics=sem, vmem_limit_bytes=VMEM_LIMIT)


def _ada_kernel(c_ref, w_ref, b_ref, o_ref):
    c = c_ref[...]
    s = c * jax.nn.sigmoid(c)
    o_ref[...] = jnp.dot(s, w_ref[...], precision=HI, preferred_element_type=F32) + b_ref[...]


def ada_mods(cond8, w_ada, b_ada):
    depth = w_ada.shape[0]
    tn = 1536
    return pl.pallas_call(
        _ada_kernel,
        grid=(depth, 6 * D // tn),
        in_specs=[pl.BlockSpec((8, D), lambda l, j: (0, 0)),
                  pl.BlockSpec((None, D, tn), lambda l, j: (l, 0, j)),
                  pl.BlockSpec((None, 1, tn), lambda l, j: (l, 0, j))],
        out_specs=pl.BlockSpec((None, 8, tn), lambda l, j: (l, 0, j)),
        out_shape=jax.ShapeDtypeStruct((depth, 8, 6 * D), F32),
        compiler_params=_cparams(("arbitrary", "arbitrary")),
        name="ada_mods",
    )(cond8, w_ada, b_ada.reshape(depth, 1, 6 * D))


def _rms_mod(x, g, sc, sh):
    ms = jnp.mean(x * x, axis=-1, keepdims=True)
    return (x * lax.rsqrt(ms + EPS) * g) * (1.0 + sc) + sh


def _in_kernel(x_ref, mod_ref, g_ref, w_ref, o_ref):
    h = _rms_mod(x_ref[...], g_ref[...], mod_ref[:, D:2 * D], mod_ref[:, 0:D])
    o_ref[...] = jnp.dot(h.astype(BF16), w_ref[...], preferred_element_type=F32)


def in_proj(x, mods, g, w_bf16, tok_per_mod):
    t = x.shape[0]
    tm = 256
    return pl.pallas_call(
        _in_kernel,
        grid=(t // tm,),
        in_specs=[pl.BlockSpec((tm, D), lambda i: (i, 0)),
                  pl.BlockSpec((None, 1, 6 * D), lambda i: (i * tm // tok_per_mod, 0, 0)),
                  pl.BlockSpec((1, D), lambda i: (0, 0)),
                  pl.BlockSpec((D, P_W), lambda i: (0, 0))],
        out_specs=pl.BlockSpec((tm, P_W), lambda i: (i, 0)),
        out_shape=jax.ShapeDtypeStruct((t, P_W), F32),
        compiler_params=_cparams(("arbitrary",)),
        name="in_proj",
    )(x, mods, g, w_bf16)


def _out_kernel(nb_ctx, *refs):
    yc_refs, ys_refs = refs[0:4], refs[4:8]
    x_ref, mod_ref, g_ref, w_ref, rw_ref, rb_ref, x1_ref, h2_ref, eidx_ref, ew_ref, cnt_ref = refs[8:]
    is_ctx = pl.program_id(0) < nb_ctx
    y = jnp.concatenate([jnp.where(is_ctx, c[...], s[...]) for c, s in zip(yc_refs, ys_refs)], axis=1).astype(BF16)
    mix = jnp.dot(y, w_ref[...], preferred_element_type=F32)
    x1 = x_ref[...] + mod_ref[:, 2 * D:3 * D] * mix
    x1_ref[...] = x1
    h2 = _rms_mod(x1, g_ref[...], mod_ref[:, 4 * D:5 * D], mod_ref[:, 3 * D:4 * D])
    h2_ref[...] = h2.astype(BF16)
    lg = lax.dot_general(rw_ref[...], h2, (((1,), (1,)), ((), ())), precision=HI,
                         preferred_element_type=F32) + rb_ref[:, 0:1]
    e_iota = lax.broadcasted_iota(jnp.int32, lg.shape, 0)
    vals, idxs = [], []
    for _ in range(TOP_K):
        m = jnp.max(lg, axis=0, keepdims=True)
        idx = jnp.min(jnp.where(lg == m, e_iota, N_EXP), axis=0, keepdims=True)
        vals.append(m)
        idxs.append(idx)
        lg = jnp.where(e_iota == idx, -jnp.inf, lg)
    ex = [jnp.exp(v - vals[0]) for v in vals]
    den = ex[0] + ex[1] + ex[2] + ex[3]
    eidx_ref[...] = jnp.concatenate(idxs, axis=0)
    ew_ref[...] = jnp.concatenate([e / den for e in ex], axis=0)
    ind = jnp.zeros(lg.shape, F32)
    for idx in idxs:
        ind += (e_iota == idx).astype(F32)
    cnt_ref[...] = jnp.broadcast_to(jnp.sum(ind, axis=1, keepdims=True), (N_EXP, LANES))


def out_proj_route(ys_ctx, ys_lat, x, mods, g, w_bf16, rw_t, rb, tok_per_mod):
    t = x.shape[0]
    nb = t // TB
    nb_ctx = ys_ctx[0].shape[0] // TB
    cspec = pl.BlockSpec((TB, 256), lambda i: (jnp.minimum(i, nb_ctx - 1), 0))
    lspec = pl.BlockSpec((TB, 256), lambda i: (jnp.maximum(i - nb_ctx, 0), 0))
    return pl.pallas_call(
        functools.partial(_out_kernel, nb_ctx),
        grid=(nb,),
        in_specs=[cspec] * 4 + [lspec] * 4 + [
                  pl.BlockSpec((TB, D), lambda i: (i, 0)),
                  pl.BlockSpec((None, 1, 6 * D), lambda i: (i * TB // tok_per_mod, 0, 0)),
                  pl.BlockSpec((1, D), lambda i: (0, 0)),
                  pl.BlockSpec((D, D), lambda i: (0, 0)),
                  pl.BlockSpec((N_EXP, D), lambda i: (0, 0)),
                  pl.BlockSpec((N_EXP, LANES), lambda i: (0, 0))],
        out_specs=[pl.BlockSpec((TB, D), lambda i: (i, 0)),
                   pl.BlockSpec((TB, D), lambda i: (i, 0)),
                   pl.BlockSpec((None, TOP_K, TB), lambda i: (i, 0, 0)),
                   pl.BlockSpec((None, TOP_K, TB), lambda i: (i, 0, 0)),
                   pl.BlockSpec((None, N_EXP, LANES), lambda i: (i, 0, 0))],
        out_shape=[jax.ShapeDtypeStruct((t, D), F32),
                   jax.ShapeDtypeStruct((t, D), BF16),
                   jax.ShapeDtypeStruct((nb, TOP_K, TB), jnp.int32),
                   jax.ShapeDtypeStruct((nb, TOP_K, TB), F32),
                   jax.ShapeDtypeStruct((nb, N_EXP, LANES), F32)],
        compiler_params=_cparams(("arbitrary",)),
        name="out_proj_route",
    )(*ys_ctx, *ys_lat, x, mods, g, w_bf16, rw_t, rb)


def moe_tables(cnt, n_tiles):
    nb = cnt.shape[0]
    cnt8 = (cnt + CH - 1) // CH * CH
    ends = jnp.cumsum(cnt8, axis=1)
    off = ends - cnt8
    nchunks = ends[:, -1] // CH
    tot = jnp.sum(cnt8, axis=0)
    totp = (tot + TM - 1) // TM * TM
    eend = jnp.cumsum(totp)
    estart = eend - totp
    gdst = estart[None, :] + jnp.cumsum(cnt8, axis=0) - cnt8
    r = jnp.arange(NCH, dtype=jnp.int32) * CH
    e_of_c = jnp.minimum(jnp.sum((ends[:, None, :] <= r[None, :, None]).astype(jnp.int32), axis=-1), N_EXP - 1)
    pick = e_of_c[:, :, None] == jnp.arange(N_EXP, dtype=jnp.int32)[None, None, :]
    g_of_c = jnp.sum(jnp.where(pick, gdst[:, None, :], 0), axis=-1)
    o_of_c = jnp.sum(jnp.where(pick, off[:, None, :], 0), axis=-1)
    gchunk = (g_of_c + r[None, :] - o_of_c) // CH
    nused = eend[-1] // TM
    ti = jnp.arange(n_tiles, dtype=jnp.int32)
    tile_e = jnp.sum((eend[None, :] // TM <= jnp.minimum(ti, nused - 1)[:, None]).astype(jnp.int32), axis=-1)
    tile_e = jnp.minimum(tile_e, N_EXP - 1)
    has = totp > 0
    ei = jnp.arange(N_EXP, dtype=jnp.int32)
    later = has[None, :] & (ei[None, :] > ei[:, None])
    next_of = jnp.min(jnp.where(later, ei[None, :], N_EXP), axis=1)
    next_of = jnp.where(next_of == N_EXP, -1, next_of)
    ordinal = jnp.cumsum(has.astype(jnp.int32)) - 1
    tile_next = jnp.sum(jnp.where(tile_e[:, None] == ei[None, :], next_of[None, :], 0), axis=1)
    tile_slot = jnp.sum(jnp.where(tile_e[:, None] == ei[None, :], ordinal[None, :], 0), axis=1) % 2
    i32 = lambda a: a.astype(jnp.int32)
    return dict(off=off, gchunk=i32(gchunk.reshape(-1)), nchunks=i32(nchunks), tile_e=i32(tile_e),
                tile_next=i32(tile_next), tile_slot=i32(tile_slot), nused=i32(nused.reshape(1)),
                pad_lo=i32((estart + tot) // CH), pad_hi=i32(eend // CH))


def _dest_rows(eidx, off_col):
    e_iota = lax.broadcasted_iota(jnp.int32, (N_EXP, TB), 0)
    ohs = [e_iota == eidx[k:k + 1, :] for k in range(TOP_K)]
    ind = jnp.zeros((N_EXP, TB), F32)
    for oh in ohs:
        ind += oh.astype(F32)
    ti = lax.broadcasted_iota(jnp.int32, (TB, TB), 0)
    tj = lax.broadcasted_iota(jnp.int32, (TB, TB), 1)
    upper = (ti <= tj).astype(BF16)
    rank_incl = jnp.dot(ind.astype(BF16), upper, preferred_element_type=F32)
    base = off_col + rank_incl - ind
    return [jnp.sum(jnp.where(oh, base, 0.0), axis=0, keepdims=True).astype(jnp.int32) for oh in ohs]


def _dispatch_kernel(gchunk_ref, nch_ref, plo_ref, phi_ref, nused_ref, h2_ref, eidx_ref, off_ref, xs_ref,
                     xbs_ref, zbuf_ref, sem, zsem):
    blk = pl.program_id(0)
    last = pl.num_programs(0) - 1
    n_tiles = xs_ref.shape[0] // TM

    def zero_fill(start):
        def pad_chunk(c, carry):
            cp = pltpu.make_async_copy(zbuf_ref.at[pl.ds(0, CH), :],
                                       xs_ref.at[pl.ds(pl.multiple_of(c * CH, CH), CH), :], zsem)
            cp.start() if start else cp.wait()
            return carry

        def per_expert(e, carry):
            lax.fori_loop(plo_ref[e], phi_ref[e], pad_chunk, 0)
            return carry

        def tail_tile(t, carry):
            cp = pltpu.make_async_copy(zbuf_ref, xs_ref.at[pl.ds(pl.multiple_of(t * TM, TM), TM), :], zsem)
            cp.start() if start else cp.wait()
            return carry

        lax.fori_loop(0, N_EXP, per_expert, 0)
        lax.fori_loop(nused_ref[0], n_tiles, tail_tile, 0)

    @pl.when(blk == 0)
    def _():
        zbuf_ref[...] = jnp.zeros(zbuf_ref.shape, F32)
        zero_fill(True)

    dests = _dest_rows(eidx_ref[...], off_ref[:, 0:1])
    p_iota = lax.broadcasted_iota(jnp.int32, (ROWS_BS, TB), 0)
    perm = jnp.zeros((ROWS_BS, TB), F32)
    for d in dests:
        perm = jnp.where(p_iota == d, 1.0, perm)
    slot = blk % 2
    xbs_ref[slot] = jnp.dot(perm.astype(BF16), h2_ref[...], preferred_element_type=F32)

    def copy(b, c):
        dst = pl.multiple_of(gchunk_ref[b * NCH + c] * CH, CH)
        src = pl.multiple_of(c * CH, CH)
        return pltpu.make_async_copy(xbs_ref.at[b % 2, pl.ds(src, CH), :], xs_ref.at[pl.ds(dst, CH), :],
                                     sem.at[b % 2])

    def start(c, carry):
        copy(blk, c).start()
        return carry

    lax.fori_loop(0, nch_ref[blk], start, 0)

    def wait_block(b):
        rows = nch_ref[b] * CH
        pltpu.make_async_copy(xbs_ref.at[b % 2, pl.ds(0, rows), :], xs_ref.at[pl.ds(0, rows), :],
                              sem.at[b % 2]).wait()

    @pl.when(blk > 0)
    def _():
        wait_block(blk - 1)

    @pl.when(blk == last)
    def _():
        wait_block(blk)
        zero_fill(False)


def moe_dispatch(h2, eidx, off_b, tb, rows_alloc):
    nb = h2.shape[0] // TB
    return pl.pallas_call(
        _dispatch_kernel,
        grid_spec=pltpu.PrefetchScalarGridSpec(
            num_scalar_prefetch=5,
            grid=(nb,),
            in_specs=[pl.BlockSpec((TB, D), lambda i, *_: (i, 0)),
                      pl.BlockSpec((None, TOP_K, TB), lambda i, *_: (i, 0, 0)),
                      pl.BlockSpec((None, N_EXP, LANES), lambda i, *_: (i, 0, 0))],
            out_specs=pl.BlockSpec(memory_space=pl.ANY),
            scratch_shapes=[pltpu.VMEM((2, ROWS_BS, D), F32), pltpu.VMEM((TM, D), F32),
                            pltpu.SemaphoreType.DMA((2,)), pltpu.SemaphoreType.DMA(())]),
        out_shape=jax.ShapeDtypeStruct((rows_alloc, D), F32),
        compiler_params=_cparams(("arbitrary",)),
        name="moe_dispatch",
    )(tb["gchunk"], tb["nchunks"], tb["pad_lo"], tb["pad_hi"], tb["nused"], h2, eidx, off_b)


def _expert_kernel(layer, te_ref, tnext_ref, tslot_ref, nused_ref, x_ref, bgu_ref, bdn_ref, wgu_hbm, wdn_hbm,
                   y_ref, wgu_f32, wdn_f32, wgu_bf, wdn_bf, sem):
    i = pl.program_id(0)

    def fetch(e, slot, start):
        for k, (src, dst) in enumerate(((wgu_hbm, wgu_f32), (wdn_hbm, wdn_f32))):
            cp = pltpu.make_async_copy(src.at[layer, e], dst.at[slot], sem.at[slot, k])
            cp.start() if start else cp.wait()

    @pl.when(i >= nused_ref[0])
    def _():
        y_ref[...] = jnp.zeros(y_ref.shape, F32)

    @pl.when(i < nused_ref[0])
    def _():
        e = te_ref[i]
        slot = tslot_ref[i]
        first = jnp.logical_or(i == 0, e != te_ref[jnp.maximum(i - 1, 0)])

        @pl.when(i == 0)
        def _():
            fetch(e, slot, True)

        @pl.when(first)
        def _():
            fetch(e, slot, False)
            nxt = tnext_ref[i]

            @pl.when(nxt >= 0)
            def _():
                fetch(nxt, 1 - slot, True)

            wgu_bf[...] = wgu_f32[slot].astype(BF16)
            wdn_bf[...] = wdn_f32[slot].astype(BF16)

        gu = jnp.dot(x_ref[...].astype(BF16), wgu_bf[...], preferred_element_type=F32) + bgu_ref[...]
        g = jnp.minimum(gu[:, 0:D_FF], SWIGLU_LIMIT)
        u = jnp.clip(gu[:, D_FF:2 * D_FF], -SWIGLU_LIMIT, SWIGLU_LIMIT)
        act = g * jax.nn.sigmoid(SWIGLU_ALPHA * g) * (u + 1.0)
        y_ref[...] = jnp.dot(act.astype(BF16), wdn_bf[...], preferred_element_type=F32) + bdn_ref[...]


def moe_experts(xs, tb, layer, w_gu, b_gu, w_dn, b_dn):
    n_tiles = xs.shape[0] // TM
    depth = w_gu.shape[0]

    def xmap(i, te, tn, ts, nu):
        return (jnp.minimum(i, jnp.maximum(nu[0] - 1, 0)), 0)

    def bmap(i, te, tn, ts, nu):
        return (layer, te[i], 0, 0)

    return pl.pallas_call(
        functools.partial(_expert_kernel, layer),
        grid_spec=pltpu.PrefetchScalarGridSpec(
            num_scalar_prefetch=4,
            grid=(n_tiles,),
            in_specs=[pl.BlockSpec((TM, D), xmap),
                      pl.BlockSpec((None, None, 1, 2 * D_FF), bmap),
                      pl.BlockSpec((None, None, 1, D), bmap),
                      pl.BlockSpec(memory_space=pl.ANY),
                      pl.BlockSpec(memory_space=pl.ANY)],
            out_specs=pl.BlockSpec((TM, D), lambda i, *_: (i, 0)),
            scratch_shapes=[pltpu.VMEM((2, D, 2 * D_FF), F32), pltpu.VMEM((2, D_FF, D), F32),
                            pltpu.VMEM((D, 2 * D_FF), BF16), pltpu.VMEM((D_FF, D), BF16),
                            pltpu.SemaphoreType.DMA((2, 2))]),
        out_shape=jax.ShapeDtypeStruct((xs.shape[0], D), F32),
        compiler_params=_cparams(("arbitrary",)),
        name="moe_experts",
    )(tb["tile_e"], tb["tile_next"], tb["tile_slot"], tb["nused"], xs,
      b_gu.reshape(depth, N_EXP, 1, 2 * D_FF), b_dn.reshape(depth, N_EXP, 1, D), w_gu, w_dn)


def _combine_kernel(final, gchunk_ref, nch_ref, eidx_ref, ew_ref, off_ref, x1_ref, mod_ref, fg_ref, ys_ref,
                    o_ref, ybs_ref, sem):
    blk = pl.program_id(0)
    n = nch_ref[blk]
    slot = blk % 2

    def copy(b, c):
        src = pl.multiple_of(gchunk_ref[b * NCH + c] * CH, CH)
        dst = pl.multiple_of(c * CH, CH)
        return pltpu.make_async_copy(ys_ref.at[pl.ds(src, CH), :], ybs_ref.at[b % 2, pl.ds(dst, CH), :],
                                     sem.at[b % 2])

    def fetch_block(b):
        def start(c, carry):
            copy(b, c).start()
            return carry
        lax.fori_loop(0, nch_ref[b], start, 0)

    @pl.when(blk == 0)
    def _():
        fetch_block(blk)

    @pl.when(blk + 1 < pl.num_programs(0))
    def _():
        fetch_block(blk + 1)

    dests = _dest_rows(eidx_ref[...], off_ref[:, 0:1])
    ew = ew_ref[...]
    p_iota = lax.broadcasted_iota(jnp.int32, (ROWS_BS, TB), 0)
    gsel = jnp.zeros((ROWS_BS, TB), F32)
    for k, d in enumerate(dests):
        gsel = jnp.where(p_iota == d, ew[k:k + 1, :], gsel)
    perm = jnp.where(gsel > 0.0, 1.0, 0.0)
    gate_col = jnp.sum(gsel, axis=1, keepdims=True)
    pltpu.make_async_copy(ys_ref.at[pl.ds(0, n * CH), :], ybs_ref.at[slot, pl.ds(0, n * CH), :], sem.at[slot]).wait()
    row_iota = lax.broadcasted_iota(jnp.int32, (ROWS_BS, 1), 0)
    yb = jnp.where(row_iota < n * CH, ybs_ref[slot], 0.0) * gate_col
    moe = lax.dot_general(perm.astype(BF16), yb.astype(BF16), (((0,), (0,)), ((), ())),
                          preferred_element_type=F32)
    x2 = x1_ref[...] + mod_ref[:, 5 * D:6 * D] * moe
    if final:
        ms = jnp.mean(x2 * x2, axis=-1, keepdims=True)
        x2 = x2 * lax.rsqrt(ms + EPS) * fg_ref[...]
    o_ref[...] = x2


def moe_combine(ys, eidx, ew, off_b, x1, mods, fg, gchunk, nchunks, tok_per_mod, final):
    t = x1.shape[0]
    nb = t // TB
    return pl.pallas_call(
        functools.partial(_combine_kernel, final),
        grid_spec=pltpu.PrefetchScalarGridSpec(
            num_scalar_prefetch=2,
            grid=(nb,),
            in_specs=[pl.BlockSpec((None, TOP_K, TB), lambda i, *_: (i, 0, 0)),
                      pl.BlockSpec((None, TOP_K, TB), lambda i, *_: (i, 0, 0)),
                      pl.BlockSpec((None, N_EXP, LANES), lambda i, *_: (i, 0, 0)),
                      pl.BlockSpec((TB, D), lambda i, *_: (i, 0)),
                      pl.BlockSpec((None, 1, 6 * D), lambda i, *_: (i * TB // tok_per_mod, 0, 0)),
                      pl.BlockSpec((1, D), lambda i, *_: (0, 0)),
                      pl.BlockSpec(memory_space=pl.ANY)],
            out_specs=pl.BlockSpec((TB, D), lambda i, *_: (i, 0)),
            scratch_shapes=[pltpu.VMEM((2, ROWS_BS, D), F32), pltpu.SemaphoreType.DMA((2,))]),
        out_shape=jax.ShapeDtypeStruct((t, D), F32),
        compiler_params=_cparams(("arbitrary",)),
        name="moe_combine",
    )(gchunk, nchunks, eidx, ew, off_b, x1, mods, fg, ys)


def moe_layer(h2, eidx, ew, cnt_b, x1, mods, fg, layer, w_gu, b_gu, w_dn, b_dn, tok_per_mod, final):
    t = h2.shape[0]
    nb = t // TB
    max_rows = t * TOP_K + nb * N_EXP * (CH - 1) + N_EXP * (TM - CH)
    n_tiles = (max_rows + TM - 1) // TM
    cnt = cnt_b[:, :, 0].astype(jnp.int32)
    tb = moe_tables(cnt, n_tiles)
    off_b = jnp.broadcast_to(tb["off"].astype(F32)[:, :, None], (nb, N_EXP, LANES))
    xs = moe_dispatch(h2, eidx, off_b, tb, n_tiles * TM)
    ys = moe_experts(xs, tb, layer, w_gu, b_gu, w_dn, b_dn)
    return moe_combine(ys, eidx, ew, off_b, x1, mods, fg, tb["gchunk"], tb["nchunks"], tok_per_mod, final)


N_HEADS = 4


def _stack_heads(x, head_w):
    lane_h = lax.broadcasted_iota(jnp.int32, x.shape, 1) // head_w
    return jnp.concatenate([jnp.where(lane_h == h, x, 0.0) for h in range(N_HEADS)], axis=0)


def _unstack_heads(xs, head_w):
    r = xs.shape[0] // N_HEADS
    lane_h = lax.broadcasted_iota(jnp.int32, (r, xs.shape[1]), 1) // head_w
    out = jnp.zeros((r, xs.shape[1]), F32)
    for h in range(N_HEADS):
        out = jnp.where(lane_h == h, xs[h * r:(h + 1) * r, :], out)
    return out


def _block_diag_mask(rows, cols, rw, cw):
    ri = lax.broadcasted_iota(jnp.int32, (rows, cols), 0) // rw
    ci = lax.broadcasted_iota(jnp.int32, (rows, cols), 1) // cw
    return ri == ci


def _head_rmsnorm(o, head_w):
    n = o.shape[1]
    bd = _block_diag_mask(n, n, head_w, head_w).astype(F32)
    ms = jnp.dot(o * o, bd, precision=HI, preferred_element_type=F32) * (1.0 / head_w)
    return o * lax.rsqrt(ms + EPS)


def _nt(a, b, **kw):
    return lax.dot_general(a, b, (((1,), (1,)), ((), ())), preferred_element_type=F32, **kw)


def _tn(a, b, **kw):
    return lax.dot_general(a, b, (((0,), (0,)), ((), ())), preferred_element_type=F32, **kw)


NA_DH = 64
NA_WIN_R = 8
NA_WIN_C = 16
NA_ROWS = 64


NA_RPS = 2


def _na_window_start(r):
    return jnp.clip(r - NA_WIN_R // 2, 0, NA_ROWS - NA_WIN_R)


def _na_lat_kernel(q_ref, k_ref, v_ref, kc_ref, vc_ref, *refs):
    bias_refs, o_ref = refs[:NA_RPS], refs[NA_RPS]
    kc = kc_ref[...].astype(BF16)
    vc = vc_ref[...].astype(BF16)
    for t in range(NA_RPS):
        start = _na_window_start(pl.program_id(1) * NA_RPS + t)
        rows = pl.ds(pl.multiple_of(start * GRID_W, GRID_W), NA_WIN_R * GRID_W)
        tok = slice(t * GRID_W, (t + 1) * GRID_W)
        qs = _stack_heads(q_ref[tok, :] * (NA_DH ** -0.5), NA_DH).astype(BF16)
        k_all = jnp.concatenate([k_ref[rows, :].astype(BF16), kc], axis=0)
        v_all = jnp.concatenate([v_ref[rows, :].astype(BF16), vc], axis=0)
        s = _nt(qs, k_all) + bias_refs[t][...]
        m = jnp.max(s, axis=1, keepdims=True)
        p = jnp.exp(s - m)
        den = jnp.sum(p, axis=1, keepdims=True)
        o = jnp.dot(p.astype(BF16), v_all, preferred_element_type=F32)
        o_ref[tok, :] = _unstack_heads(o / den, NA_DH)


def _na_bias_table(rpb):
    col = np.arange(GRID_W)
    c_start = np.clip(col - NA_WIN_C // 2, 0, GRID_W - NA_WIN_C)
    col_mask = (col[None, :] >= c_start[:, None]) & (col[None, :] < c_start[:, None] + NA_WIN_C)
    c_idx = np.clip(col[None, :] - col[:, None], -(NA_WIN_C - 1), NA_WIN_C - 1) + (NA_WIN_C - 1)
    onehot = jnp.asarray(c_idx[None, :, :] == np.arange(2 * NA_WIN_C - 1)[:, None, None], F32)
    tb = jnp.einsum('hrc,cqk->hrqk', rpb, onehot, precision=HI)
    tb = jnp.where(col_mask[None, None], tb, -jnp.inf)
    out = []
    for ri0 in range(NA_WIN_R):
        blk = tb[:, ri0:ri0 + NA_WIN_R]
        out.append(blk.transpose(0, 2, 1, 3).reshape(N_HEADS * GRID_W, NA_WIN_R * GRID_W))
    return jnp.stack(out, 0)


def na_latent(proj, tok0, bs, ls, kc, vc, rpb):
    lc = kc.shape[1]
    bias = _na_bias_table(rpb)
    bias = jnp.concatenate([bias, jnp.zeros(bias.shape[:2] + (lc,), F32)], axis=-1)
    nstep = ls // GRID_W // NA_RPS
    tq = NA_RPS * GRID_W
    rb0 = tok0 // tq
    sb0 = tok0 // ls

    def bias_spec(t):
        def imap(b, s):
            r = s * NA_RPS + t
            return (_na_window_start(r) - r + NA_WIN_R - 1, 0, 0)
        return pl.BlockSpec((None, N_HEADS * GRID_W, NA_WIN_R * GRID_W + lc), imap)

    return pl.pallas_call(
        _na_lat_kernel,
        grid=(bs, nstep),
        in_specs=[pl.BlockSpec((tq, 256), lambda b, s: (rb0 + b * nstep + s, P_NQ // 256)),
                  pl.BlockSpec((ls, 256), lambda b, s: (sb0 + b, P_NK // 256)),
                  pl.BlockSpec((ls, 256), lambda b, s: (sb0 + b, P_NV // 256)),
                  pl.BlockSpec((None, lc, 256), lambda b, s: (b, 0, 0)),
                  pl.BlockSpec((None, lc, 256), lambda b, s: (b, 0, 0))] + [bias_spec(t) for t in range(NA_RPS)],
        out_specs=pl.BlockSpec((tq, 256), lambda b, s: (b * nstep + s, 0)),
        out_shape=jax.ShapeDtypeStruct((bs * ls, 256), F32),
        compiler_params=_cparams(("arbitrary", "arbitrary")),
        name="na_latent",
    )(proj, proj, proj, kc, vc, *([bias] * NA_RPS))


def _na_ctx_kernel(q_ref, k_ref, v_ref, o_ref, kc_ref, vc_ref):
    qs = _stack_heads(q_ref[...] * (NA_DH ** -0.5), NA_DH).astype(BF16)
    k = k_ref[...]
    v = v_ref[...]
    s = _nt(qs, k.astype(BF16))
    m = jnp.max(s, axis=1, keepdims=True)
    p = jnp.exp(s - m)
    den = jnp.sum(p, axis=1, keepdims=True)
    o = jnp.dot(p.astype(BF16), v.astype(BF16), preferred_element_type=F32)
    o_ref[...] = _unstack_heads(o / den, NA_DH)
    for h in range(N_HEADS):
        kc_ref[h] = k[:, h * NA_DH:(h + 1) * NA_DH]
        vc_ref[h] = v[:, h * NA_DH:(h + 1) * NA_DH]


def na_context(proj, bp, lp):
    cache_spec = pl.BlockSpec((None, N_HEADS, lp, NA_DH), lambda b: (b, 0, 0, 0))
    cache_shape = jax.ShapeDtypeStruct((bp, N_HEADS, lp, NA_DH), F32)
    return pl.pallas_call(
        _na_ctx_kernel,
        grid=(bp,),
        in_specs=[pl.BlockSpec((lp, 256), lambda b: (b, P_NQ // 256)),
                  pl.BlockSpec((lp, 256), lambda b: (b, P_NK // 256)),
                  pl.BlockSpec((lp, 256), lambda b: (b, P_NV // 256))],
        out_specs=[pl.BlockSpec((lp, 256), lambda b: (b, 0)), cache_spec, cache_spec],
        out_shape=[jax.ShapeDtypeStruct((bp * lp, 256), F32), cache_shape, cache_shape],
        compiler_params=_cparams(("arbitrary",)),
        name="na_context",
    )(proj, proj, proj)


GLA_DK = 32
GLA_DV = 64
GLA_C = 128
GLA_NORMALIZER = 16.0


def _gla_kernel(has_state, seq, *refs):
    if has_state:
        q_ref, k_ref, v_ref, g_ref, aux_ref, wg_ref, bg_ref, gn_ref, s0_ref, y_ref, acc_ref, bcum_ref = refs
    else:
        q_ref, k_ref, v_ref, g_ref, aux_ref, wg_ref, bg_ref, gn_ref, y_ref, sfin_ref, acc_ref, bcum_ref = refs
    c_sz = min(GLA_C, seq)
    n_chunks = seq // c_sz
    mid = c_sz // 2
    hk = N_HEADS * GLA_DK
    ti = lax.broadcasted_iota(jnp.int32, (c_sz, c_sz), 0)
    tj = lax.broadcasted_iota(jnp.int32, (c_sz, c_sz), 1)
    ai = lax.broadcasted_iota(jnp.int32, (N_HEADS * c_sz, c_sz), 0) % c_sz
    aj = lax.broadcasted_iota(jnp.int32, (N_HEADS * c_sz, c_sz), 1)
    bd = _block_diag_mask(N_HEADS * GLA_DV, hk, GLA_DV, GLA_DK)
    tri2 = jnp.concatenate([(tj <= ti).astype(BF16), (tj >= ti).astype(BF16)], axis=0)
    n_lr = wg_ref.shape[0]

    def pre(c, carry):
        rows = pl.ds(pl.multiple_of(c * c_sz, c_sz), c_sz)
        la = jax.nn.log_sigmoid(jnp.dot(aux_ref[rows, 0:n_lr], wg_ref[...], precision=HI, preferred_element_type=F32)
                                + bg_ref[...]) * (1.0 / GLA_NORMALIZER)
        l1 = la.astype(BF16)
        r1 = la - l1.astype(F32)
        l2 = r1.astype(BF16)
        l3 = (r1 - l2.astype(F32)).astype(BF16)
        bb = (jnp.dot(tri2, l1, preferred_element_type=F32) + jnp.dot(tri2, l2, preferred_element_type=F32)
              + jnp.dot(tri2, l3, preferred_element_type=F32))
        bcum_ref[rows, 0:hk] = bb[0:c_sz, 0:hk]
        bcum_ref[rows, hk:2 * hk] = bb[c_sz:2 * c_sz, hk:2 * hk]
        return carry

    lax.fori_loop(0, n_chunks, pre, 0)

    def step(d, c, st):
        amask = (aj <= ai) if d == 0 else (aj >= ai)
        rows = pl.ds(pl.multiple_of(c * c_sz, c_sz), c_sz)
        q = q_ref[rows, :] * (GLA_DK ** -0.5)
        k = k_ref[rows, :]
        v = v_ref[rows, :].astype(BF16)
        b = bcum_ref[rows, d * hk:(d + 1) * hk]
        btot = b[c_sz - 1:c_sz, :] if d == 0 else b[0:1, :]
        ref = b[mid - 1:mid, :] if d == 0 else b[mid:mid + 1, :]
        qt = q * jnp.exp(b - ref)
        kt = (k * jnp.exp(ref - b)).astype(BF16)
        ke = (k * jnp.exp(btot - b)).astype(BF16)
        a = _nt(_stack_heads(qt, GLA_DK).astype(BF16), kt)
        a = jnp.where(amask, a, 0.0).astype(BF16)
        o = _unstack_heads(jnp.dot(a, v, preferred_element_type=F32), GLA_DV)
        o += _nt((qt * jnp.exp(ref)).astype(BF16), st.astype(BF16))
        acc_ref[rows, :] += o
        upd = _tn(v, ke)
        return st * jnp.exp(btot) + jnp.where(bd, upd, 0.0)

    def body(n, sts):
        return step(0, n, sts[0]), step(1, n_chunks - 1 - n, sts[1])

    acc_ref[...] = jnp.zeros(acc_ref.shape, F32)
    if has_state:
        st0 = (s0_ref[0], s0_ref[1])
    else:
        st0 = (jnp.zeros((N_HEADS * GLA_DV, hk), F32),) * 2
    st_f, st_b = lax.fori_loop(0, n_chunks, body, st0)
    if not has_state:
        sfin_ref[0] = st_f
        sfin_ref[1] = st_b
    gn = gn_ref[...]

    def epi(i, carry):
        rows = pl.ds(pl.multiple_of(i * 256, 256), 256)
        g = g_ref[rows, :]
        y_ref[rows, :] = _head_rmsnorm(acc_ref[rows, :], GLA_DV) * gn * (g * jax.nn.sigmoid(g))
        return carry

    lax.fori_loop(0, seq // 256, epi, 0)


def gla_mixer(proj, tok0, nbatch, seq, w_gate, b_gate, norm_g, s0t):
    has_state = s0t is not None
    sb = tok0 // seq
    hk = N_HEADS * GLA_DK
    wg = jnp.concatenate([w_gate[0], w_gate[1]], axis=1)
    n_lr = wg.shape[0]
    col = lambda w, c0: pl.BlockSpec((seq, w), lambda b: (sb + b, c0 // w), pipeline_mode=pl.Buffered(1))
    in_specs = [col(128, P_GQ), col(128, P_GK), col(256, P_GV), col(256, P_GG), col(128, P_AUX),
                pl.BlockSpec((n_lr, 2 * hk), lambda b: (0, 0)),
                pl.BlockSpec((1, 2 * hk), lambda b: (0, 0)),
                pl.BlockSpec((1, 256), lambda b: (0, 0))]
    args = [proj, proj, proj, proj, proj, wg, b_gate.reshape(1, 2 * hk), jnp.tile(norm_g, N_HEADS).reshape(1, 256)]
    y_spec = pl.BlockSpec((seq, 256), lambda b: (b, 0))
    y_shape = jax.ShapeDtypeStruct((nbatch * seq, 256), F32)
    st_spec = pl.BlockSpec((None, 2, 256, 128), lambda b: (b, 0, 0, 0))
    if has_state:
        in_specs.append(st_spec)
        args.append(s0t)
        out_specs, out_shape = y_spec, y_shape
    else:
        out_specs = [y_spec, st_spec]
        out_shape = [y_shape, jax.ShapeDtypeStruct((nbatch, 2, 256, 128), F32)]
    return pl.pallas_call(
        functools.partial(_gla_kernel, has_state, seq),
        grid=(nbatch,),
        in_specs=in_specs, out_specs=out_specs, out_shape=out_shape,
        scratch_shapes=[pltpu.VMEM((seq, 256), F32), pltpu.VMEM((seq, 2 * hk), F32)],
        compiler_params=_cparams(("arbitrary",)),
        name="gla_latent" if has_state else "gla_context",
    )(*args)


ML_DH = 64
ML_C = 256
ROPE_BASE = 10000.0
ML_GATE_LANE0 = 16


def _ml_gate_selectors():
    rep = np.zeros((2, 2, LANES, N_HEADS * ML_DH), np.float32)
    sel = np.zeros((2, 8, LANES), np.float32)
    for d in range(2):
        for g in range(2):
            for h in range(N_HEADS):
                lane = ML_GATE_LANE0 + d * 8 + g * 4 + h
                rep[d, g, lane, h * ML_DH:(h + 1) * ML_DH] = 1.0
                sel[d, g * 4 + h, lane] = 1.0
    return jnp.asarray(rep), jnp.asarray(sel)


def _rope_tables(seq):
    nf = ML_DH // 4
    inv = ROPE_BASE ** (-jnp.arange(nf, dtype=F32) / nf)
    t = np.arange(seq)
    j = np.arange(ML_DH)
    pos = np.where(j[None, :] < ML_DH // 2, (t // GRID_W)[:, None], (t % GRID_W)[:, None]).astype(np.float32)
    ang = jnp.asarray(pos) * inv[j % nf][None, :]
    first = (j % (ML_DH // 2)) < nf
    cos = jnp.tile(jnp.cos(ang), (1, N_HEADS))
    sin = jnp.tile(jnp.where(first[None, :], -jnp.sin(ang), jnp.sin(ang)), (1, N_HEADS))
    return cos, sin


def _rope(x, cos, sin_signed):
    nf = ML_DH // 4
    first = (lax.broadcasted_iota(jnp.int32, x.shape, 1) % (ML_DH // 2)) < nf
    partner = jnp.where(first, pltpu.roll(x, x.shape[1] - nf, 1), pltpu.roll(x, nf, 1))
    return x * cos + partner * sin_signed


def _mlstm_kernel(latent, seq, *refs):
    if latent:
        (q_ref, k_ref, v_ref, og_ref, aux_ref, rep_ref, sel_ref, brep_ref, bsel_ref, gn_ref, cos_ref, sin_ref,
         c0_ref, n0_ref, m0_ref, y_ref, acc_ref) = refs
    else:
        (q_ref, k_ref, v_ref, og_ref, aux_ref, rep_ref, sel_ref, brep_ref, bsel_ref, gn_ref,
         y_ref, cf_ref, nf_ref, mf_ref, acc_ref) = refs
    c_sz = min(ML_C, seq)
    n_chunks = seq // c_sz
    hw = N_HEADS * ML_DH
    ti = lax.broadcasted_iota(jnp.int32, (c_sz, c_sz), 0)
    tj = lax.broadcasted_iota(jnp.int32, (c_sz, c_sz), 1)
    bd = _block_diag_mask(hw, hw, ML_DH, ML_DH)
    for d in range(2):
        causal = (tj <= ti) if d == 0 else (tj >= ti)
        tri = causal.astype(F32)
        tri_t = ((ti <= tj) if d == 0 else (ti >= tj)).astype(F32)

        def body(n, carry, d=d, causal=causal, tri=tri, tri_t=tri_t):
            cm, nrow, mrow = carry
            c = n if d == 0 else n_chunks - 1 - n
            rows = pl.ds(pl.multiple_of(c * c_sz, c_sz), c_sz)
            q = q_ref[rows, :]
            k = k_ref[rows, :] * (ML_DH ** -0.5)
            if latent:
                q = _rope(q, cos_ref[rows, :], sin_ref[rows, :])
                k = _rope(k, cos_ref[rows, :], sin_ref[rows, :])
            v = v_ref[rows, :].astype(BF16)
            aux = aux_ref[rows, :]
            li_m = jnp.dot(aux, rep_ref[d, 0], precision=HI, preferred_element_type=F32) + brep_ref[d, 0]
            lf_m = jax.nn.log_sigmoid(jnp.dot(aux, rep_ref[d, 1], precision=HI, preferred_element_type=F32)
                                      + brep_ref[d, 1])
            f_m = jnp.dot(tri, lf_m, precision=HI, preferred_element_type=F32)
            g_t = _nt(sel_ref[d], aux, precision=HI) + bsel_ref[d][:, 0:1]
            li_t = g_t[0:N_HEADS, :]
            f_t = jnp.dot(jax.nn.log_sigmoid(g_t[N_HEADS:2 * N_HEADS, :]), tri_t, precision=HI,
                          preferred_element_type=F32)
            dms, fcols, mcols = [], [], []
            for h in range(N_HEADS):
                fcol = f_m[:, h * ML_DH:h * ML_DH + 1]
                dms.append(jnp.where(causal, fcol - f_t[h:h + 1, :] + li_t[h:h + 1, :], -jnp.inf))
                fcols.append(fcol)
                mcols.append(jnp.broadcast_to(mrow[:, h * ML_DH:h * ML_DH + 1], (c_sz, 1)))
            dm = jnp.concatenate(dms, axis=0)
            log_inter = jnp.concatenate(fcols, axis=0) + jnp.concatenate(mcols, axis=0)
            m_t = jnp.maximum(log_inter, jnp.max(dm, axis=1, keepdims=True))
            qs = _stack_heads(q, ML_DH)
            qsb = qs.astype(BF16)
            s = _nt(qsb, k.astype(BF16)) * jnp.exp(dm - m_t)
            a_t = jnp.exp(log_inter - m_t)
            inter = jnp.dot(qsb, cm.astype(BF16), preferred_element_type=F32)
            v_ext = jnp.concatenate([v, jnp.ones((c_sz, LANES), BF16)], axis=1)
            sv = jnp.dot(s.astype(BF16), v_ext, preferred_element_type=F32)
            qn = jnp.dot((qs * nrow).astype(BF16), jnp.ones((hw, LANES), BF16), preferred_element_type=F32)
            num = a_t * inter + sv[:, 0:hw]
            den = a_t * qn[:, 0:1] + sv[:, hw:hw + 1]
            hst = num / jnp.maximum(jnp.abs(den), jnp.exp(-m_t))
            hout = _unstack_heads(hst, ML_DH)
            if d == 0:
                acc_ref[rows, :] = hout
            else:
                acc_ref[rows, :] += hout
            f_tot = f_m[c_sz - 1:c_sz, :] if d == 0 else f_m[0:1, :]
            w_end = f_tot - f_m + li_m
            m_new = jnp.maximum(f_tot + mrow, jnp.max(w_end, axis=0, keepdims=True))
            a = jnp.exp(f_tot + mrow - m_new)
            kw = k * jnp.exp(w_end - m_new)
            cm_new = cm * a + jnp.where(bd, _tn(kw.astype(BF16), v), 0.0)
            n_new = nrow * a + jnp.sum(kw, axis=0, keepdims=True)
            return cm_new, n_new, m_new

        if latent:
            init = (c0_ref[d], n0_ref[d], m0_ref[d])
        else:
            init = (jnp.zeros((hw, hw), F32), jnp.zeros((1, hw), F32), jnp.zeros((1, hw), F32))
        cm, nrow, mrow = lax.fori_loop(0, n_chunks, body, init)
        if not latent:
            cf_ref[d] = cm
            nf_ref[d] = nrow
            mf_ref[d] = mrow
    gn = gn_ref[...]

    def epi(i, carry):
        rows = pl.ds(pl.multiple_of(i * 256, 256), 256)
        y_ref[rows, :] = _head_rmsnorm(acc_ref[rows, :], ML_DH) * gn * jax.nn.sigmoid(og_ref[rows, :])
        return carry

    lax.fori_loop(0, seq // 256, epi, 0)


def mlstm_mixer(proj, tok0, nbatch, seq, b_gate, norm_g, state):
    latent = state is not None
    sb = tok0 // seq
    hw = N_HEADS * ML_DH
    rep, sel = _ml_gate_selectors()
    brep = jnp.repeat(b_gate.reshape(2, 2, N_HEADS), ML_DH, axis=-1).reshape(2, 2, 1, hw)
    bsel = jnp.broadcast_to(b_gate.reshape(2, 8, 1), (2, 8, LANES))
    col = lambda c0: pl.BlockSpec((seq, 256), lambda b: (sb + b, c0 // 256), pipeline_mode=pl.Buffered(1))
    full = lambda shape: pl.BlockSpec(shape, lambda b: (0,) * len(shape), pipeline_mode=pl.Buffered(1))
    in_specs = [col(P_MQ), col(P_MK), col(P_MV), col(P_MO),
                pl.BlockSpec((seq, 128), lambda b: (sb + b, P_AUX // 128), pipeline_mode=pl.Buffered(1)),
                full((2, 2, LANES, hw)), full((2, 8, LANES)), full((2, 2, 1, hw)), full((2, 8, LANES)), full((1, hw))]
    args = [proj, proj, proj, proj, proj, rep, sel, brep, bsel, norm_g.reshape(1, hw)]
    y_spec = pl.BlockSpec((seq, 256), lambda b: (b, 0))
    y_shape = jax.ShapeDtypeStruct((nbatch * seq, 256), F32)
    c_spec = pl.BlockSpec((None, 2, hw, hw), lambda b: (b, 0, 0, 0))
    r_spec = pl.BlockSpec((None, 2, 1, hw), lambda b: (b, 0, 0, 0))
    if latent:
        cos, sin = _rope_tables(seq)
        in_specs += [full((seq, hw)), full((seq, hw)), c_spec, r_spec, r_spec]
        args += [cos, sin, *state]
        out_specs, out_shape = y_spec, y_shape
    else:
        out_specs = [y_spec, c_spec, r_spec, r_spec]
        out_shape = [y_shape, jax.ShapeDtypeStruct((nbatch, 2, hw, hw), F32),
                     jax.ShapeDtypeStruct((nbatch, 2, 1, hw), F32), jax.ShapeDtypeStruct((nbatch, 2, 1, hw), F32)]
    return pl.pallas_call(
        functools.partial(_mlstm_kernel, latent, seq),
        grid=(nbatch,),
        in_specs=in_specs, out_specs=out_specs, out_shape=out_shape,
        scratch_shapes=[pltpu.VMEM((seq, 256), F32)],
        compiler_params=_cparams(("arbitrary",)),
        name="mlstm_latent" if latent else "mlstm_context",
    )(*args)


HY_CH = 256
HY_BANDS = 16
HY_EMB = 1 + 2 * HY_BANDS
HY_FFN = 64
FFT_N1 = 64
FFT_N2 = 128


HY_LANE_FWD = HY_EMB
HY_LANE_BWD = HY_EMB + 1


def _hy_filter_kernel(slab, feat_ref, w1_ref, b1_ref, w2_ref, b2_ref, w3_ref, b3_ref, fr_ref, dl_ref, o_ref):
    feats = feat_ref[...]
    a = jnp.sin(fr_ref[0:1, :] * (jnp.dot(feats, w1_ref[...], precision=HI, preferred_element_type=F32) + b1_ref[...]))
    a = jnp.sin(fr_ref[1:2, :] * (jnp.dot(a, w2_ref[...], precision=HI, preferred_element_type=F32) + b2_ref[...]))
    a = jnp.dot(a, w3_ref[...], precision=HI, preferred_element_type=F32) + b3_ref[...]
    a = a * jnp.exp(-feats[:, 0:1] * dl_ref[...])
    fwd = feats[:, HY_LANE_FWD:HY_LANE_FWD + 1]
    bwd = feats[:, HY_LANE_BWD:HY_LANE_BWD + 1]
    for order in range(2):
        h = (fwd * a[:, (2 * order) * HY_CH:(2 * order + 1) * HY_CH]
             + bwd * a[:, (2 * order + 1) * HY_CH:(2 * order + 2) * HY_CH])
        if slab:
            for j in range(h.shape[0] // slab):
                o_ref[order, :, j, :] = h[j * slab:(j + 1) * slab, :]
        else:
            o_ref[order] = h


def hyena_filters(seq, w1, b1, w2, b2, w3, b3, freq, slab=0):
    n = jnp.arange(2 * seq)
    t = jnp.where(n < seq, n, 2 * seq - n).astype(F32)
    t = jnp.where(n == seq, 0.0, t)
    t_unit = t / (seq - 1)
    bands = jnp.linspace(1e-4, HY_BANDS - 1, HY_BANDS, dtype=F32)
    ang = (2.0 * math.pi / seq) * t[:, None] * bands[None, :]
    feats = jnp.concatenate([t_unit[:, None], jnp.cos(ang), -jnp.sin(ang),
                             (n < seq).astype(F32)[:, None], (n > seq).astype(F32)[:, None],
                             jnp.zeros((2 * seq, LANES - HY_EMB - 2), F32)], axis=-1)
    w1p = jnp.zeros((LANES, HY_FFN), F32).at[0:HY_EMB].set(w1)
    deltas = jnp.abs(jnp.linspace(math.log(1e-2) / 1.5, math.log(1e-2) / 0.3, HY_CH, dtype=F32))
    rb = min(2 * seq, 1024 if slab else 512)
    full = lambda shape: pl.BlockSpec(shape, lambda i: (0,) * len(shape))
    if slab:
        out_spec = pl.BlockSpec((2, slab, rb // slab, HY_CH), lambda i: (0, 0, i, 0))
        out_shape = jax.ShapeDtypeStruct((2, slab, 2 * seq // slab, HY_CH), F32)
    else:
        out_spec = pl.BlockSpec((2, rb, HY_CH), lambda i: (0, i, 0))
        out_shape = jax.ShapeDtypeStruct((2, 2 * seq, HY_CH), F32)
    return pl.pallas_call(
        functools.partial(_hy_filter_kernel, slab),
        grid=(2 * seq // rb,),
        in_specs=[pl.BlockSpec((rb, LANES), lambda i: (i, 0)), full((LANES, HY_FFN)), full((1, HY_FFN)),
                  full((HY_FFN, HY_FFN)), full((1, HY_FFN)), full((HY_FFN, 4 * HY_CH)), full((1, 4 * HY_CH)),
                  full((2, HY_FFN)), full((1, 4 * HY_CH))],
        out_specs=out_spec, out_shape=out_shape,
        compiler_params=_cparams(("arbitrary",)),
        name="hyena_filters",
    )(feats, w1p, b1.reshape(1, -1), w2, b2.reshape(1, -1), w3, b3.reshape(1, -1), freq,
      jnp.tile(deltas, 4).reshape(1, -1))


def _hy_short_kernel(nblk, slab, u_ref, prev_ref, next_ref, w_ref, b_ref, x1_ref, x2_ref, z_ref):
    i = pl.program_id(1)
    u = u_ref[...]
    rb = u.shape[0]
    row = lax.broadcasted_iota(jnp.int32, u.shape, 0)
    prev_row = jnp.where(i > 0, prev_ref[SUBLANES - 1:SUBLANES, :], 0.0)
    next_row = jnp.where(i < nblk - 1, next_ref[0:1, :], 0.0)
    up = jnp.where(row == 0, prev_row, pltpu.roll(u, 1, 0))
    un = jnp.where(row == rb - 1, next_row, pltpu.roll(u, rb - 1, 0))
    y = up * w_ref[0:1, :] + u * w_ref[1:2, :] + un * w_ref[2:3, :] + b_ref[...]
    for k, o_ref in enumerate((x1_ref, x2_ref, z_ref)):
        if slab:
            for a in range(rb // slab):
                o_ref[:, a, :] = y[a * slab:(a + 1) * slab, k * HY_CH:(k + 1) * HY_CH]
        else:
            o_ref[...] = y[:, k * HY_CH:(k + 1) * HY_CH]


def hyena_short_conv(proj, tok0, nbatch, seq, w, b, slab=0):
    rb = min(seq, 1024 if slab else 512)
    nblk = seq // rb
    r0 = tok0 // rb
    h0 = tok0 // SUBLANES
    hpb = rb // SUBLANES
    last = (tok0 + nbatch * seq) // SUBLANES - 1
    if slab:
        o_spec = pl.BlockSpec((None, slab, rb // slab, HY_CH), lambda bb, i: (bb, 0, i, 0))
        o_shape = jax.ShapeDtypeStruct((nbatch, slab, seq // slab, HY_CH), F32)
    else:
        o_spec = pl.BlockSpec((rb, HY_CH), lambda bb, i: (bb * nblk + i, 0))
        o_shape = jax.ShapeDtypeStruct((nbatch * seq, HY_CH), F32)
    return pl.pallas_call(
        functools.partial(_hy_short_kernel, nblk, slab),
        grid=(nbatch, nblk),
        in_specs=[pl.BlockSpec((rb, 3 * HY_CH), lambda bb, i: (r0 + bb * nblk + i, P_HU // (3 * HY_CH))),
                  pl.BlockSpec((SUBLANES, 3 * HY_CH),
                               lambda bb, i: (jnp.maximum(h0 + (bb * nblk + i) * hpb - 1, 0), P_HU // (3 * HY_CH))),
                  pl.BlockSpec((SUBLANES, 3 * HY_CH),
                               lambda bb, i: (jnp.minimum(h0 + (bb * nblk + i + 1) * hpb, last), P_HU // (3 * HY_CH))),
                  pl.BlockSpec((3, 3 * HY_CH), lambda bb, i: (0, 0)),
                  pl.BlockSpec((1, 3 * HY_CH), lambda bb, i: (0, 0))],
        out_specs=[o_spec, o_spec, o_spec],
        out_shape=[o_shape, o_shape, o_shape],
        compiler_params=_cparams(("arbitrary", "arbitrary")),
        name="hyena_short_conv",
    )(proj, proj, proj, w, b.reshape(1, -1))


def _dft_consts_single(seq):
    n = 2 * seq
    k = np.arange(n)[:, None].astype(np.float64)
    m = np.arange(n)[None, :].astype(np.float64)
    ang = 2.0 * np.pi * k * m / n
    fwd = np.concatenate([np.cos(ang), -np.sin(ang)], axis=0)
    inv = np.concatenate([np.cos(ang.T[:seq]), -np.sin(ang.T[:seq])], axis=1) / n
    return (jnp.asarray(fwd, F32), jnp.asarray(fwd[:, :seq], F32), jnp.asarray(inv, F32))


def _cmul(zr, zi, hr, hi):
    return zr * hr - zi * hi, zr * hi + zi * hr


def _split_bf16(a):
    hi = a.astype(BF16)
    return hi, (a - hi.astype(F32)).astype(BF16)


def _dot3(a, b, dims=None):
    if dims is None:
        dims = (((a.ndim - 1,), (0,)), ((), ()))
    a_hi, a_lo = _split_bf16(a)
    b_hi, b_lo = _split_bf16(b)
    dg = functools.partial(lax.dot_general, dimension_numbers=dims, preferred_element_type=F32)
    return dg(a_hi, b_hi) + dg(a_lo, b_hi) + dg(a_hi, b_lo)


def _hy_spec_single_kernel(f_ref, g_ref, o_ref):
    o_ref[...] = _dot3(f_ref[...], g_ref[...])


def _hy_conv_single_kernel(x1_ref, x2_ref, z_ref, h_ref, bias_ref, f_ref, i_ref, o_ref):
    n = f_ref.shape[0] // 2
    z = z_ref[...]
    for order, xg_ref in enumerate((x1_ref, x2_ref)):
        zz = _dot3(f_ref[...], z)
        pr, pi = _cmul(zz[0:n], zz[n:2 * n], h_ref[order, 0:n, :], h_ref[order, n:2 * n, :])
        y = _dot3(i_ref[...], jnp.concatenate([pr, pi], axis=0))
        z = xg_ref[...] * (y + z * bias_ref[order])
    o_ref[...] = z


def hyena_context(x1, x2, z, g, bias, nbatch, seq):
    n = 2 * seq
    f_full, f_half, inv = _dft_consts_single(seq)
    spec = pl.pallas_call(
        _hy_spec_single_kernel,
        grid=(2,),
        in_specs=[pl.BlockSpec((2 * n, n), lambda o: (0, 0)), pl.BlockSpec((None, n, HY_CH), lambda o: (o, 0, 0))],
        out_specs=pl.BlockSpec((None, 2 * n, HY_CH), lambda o: (o, 0, 0)),
        out_shape=jax.ShapeDtypeStruct((2, 2 * n, HY_CH), F32),
        compiler_params=_cparams(("arbitrary",)),
        name="hyena_spec_context",
    )(f_full, g)
    blk = pl.BlockSpec((seq, HY_CH), lambda b: (b, 0))
    return pl.pallas_call(
        _hy_conv_single_kernel,
        grid=(nbatch,),
        in_specs=[blk, blk, blk, pl.BlockSpec((2, 2 * n, HY_CH), lambda b: (0, 0, 0)),
                  pl.BlockSpec((2, 1, HY_CH), lambda b: (0, 0, 0)),
                  pl.BlockSpec((2 * n, seq), lambda b: (0, 0)), pl.BlockSpec((seq, 2 * n), lambda b: (0, 0))],
        out_specs=blk,
        out_shape=jax.ShapeDtypeStruct((nbatch * seq, HY_CH), F32),
        compiler_params=_cparams(("arbitrary",)),
        name="hyena_conv_context",
    )(x1, x2, z, spec, bias.reshape(2, 1, HY_CH), f_half, inv)


def _dft_consts_two_stage():
    n1, n2 = FFT_N1, FFT_N2
    n = n1 * n2
    a2 = np.arange(n2, dtype=np.float64)[:, None, None]
    k1 = np.arange(n1, dtype=np.float64)[None, :, None]
    a1 = np.arange(n1, dtype=np.float64)[None, None, :]
    th = 2.0 * np.pi * (a1 * k1 / n1 + a2 * k1 / n)
    w1 = np.concatenate([np.cos(th), -np.sin(th)], axis=1)
    tht = np.transpose(th, (0, 2, 1))
    w3 = np.concatenate([np.cos(tht), -np.sin(tht)], axis=2) / n
    ph = 2.0 * np.pi * np.arange(n2, dtype=np.float64)[:, None] * np.arange(n2, dtype=np.float64)[None, :] / n2
    c, s = np.cos(ph), np.sin(ph)
    g2 = np.block([[c, s], [-s, c]])
    g2i = np.block([[c, -s], [s, c]])
    return (jnp.asarray(w1, F32), jnp.asarray(w3, F32), jnp.asarray(g2, F32), jnp.asarray(g2i, F32))


FFT_SB = 16
FFT_KB = 8


def _dotc(w, x):
    return jnp.dot(w, x.astype(BF16), preferred_element_type=F32)


def _outer_fwd_kernel(x_ref, w_ref, o_ref):
    for jj in range(FFT_SB):
        y = _dotc(w_ref[jj], x_ref[jj])
        o_ref[jj, 0] = y[0:FFT_N1]
        o_ref[jj, 1] = y[FFT_N1:2 * FFT_N1]


def _outer_fwd(w, xs):
    nbatch, n2, n1, ch = xs.shape
    wspec = pl.BlockSpec((FFT_SB, 2 * FFT_N1, n1), lambda b, j: (j, 0, 0))
    return pl.pallas_call(
        _outer_fwd_kernel,
        grid=(nbatch, n2 // FFT_SB),
        in_specs=[pl.BlockSpec((None, FFT_SB, n1, ch), lambda b, j: (b, j, 0, 0)), wspec],
        out_specs=pl.BlockSpec((None, FFT_SB, 2, FFT_N1, ch), lambda b, j: (b, j, 0, 0, 0)),
        out_shape=jax.ShapeDtypeStruct((nbatch, n2, 2, FFT_N1, ch), F32),
        compiler_params=_cparams(("arbitrary", "arbitrary")),
        name="hyena_dft_outer_fwd",
    )(xs, w.astype(BF16))


def _inner_kernel(conv, *refs):
    if conv:
        a_ref, g_ref, h_ref, i_ref, o_ref = refs
    else:
        a_ref, g_ref, o_ref = refs
    n = FFT_N2
    for kk in range(FFT_KB):
        x = jnp.concatenate([a_ref[:, 0, kk, :], a_ref[:, 1, kk, :]], axis=0)
        y = _dotc(g_ref[...], x)
        if conv:
            pr, pi = _cmul(y[0:n], y[n:2 * n], h_ref[kk, 0:n, :], h_ref[kk, n:2 * n, :])
            q = _dotc(i_ref[...], jnp.concatenate([pr, pi], axis=0))
            o_ref[kk, 0] = q[0:n]
            o_ref[kk, 1] = q[n:2 * n]
        else:
            o_ref[kk] = y


def _inner_stage(a5, g2, spec=None, g2i=None):
    nbatch, n2, _, n1, ch = a5.shape
    conv = spec is not None
    mat = pl.BlockSpec((2 * n2, 2 * n2), lambda b, j: (0, 0))
    in_specs = [pl.BlockSpec((None, n2, 2, FFT_KB, ch), lambda b, j: (b, 0, 0, j, 0)), mat]
    args = [a5, g2.astype(BF16)]
    if conv:
        in_specs += [pl.BlockSpec((FFT_KB, 2 * n2, ch), lambda b, j: (j, 0, 0)), mat]
        args += [spec, g2i.astype(BF16)]
        out_spec = pl.BlockSpec((None, FFT_KB, 2, n2, ch), lambda b, j: (b, j, 0, 0, 0))
        out_shape = jax.ShapeDtypeStruct((nbatch, n1, 2, n2, ch), F32)
    else:
        out_spec = pl.BlockSpec((None, FFT_KB, 2 * n2, ch), lambda b, j: (b, j, 0, 0))
        out_shape = jax.ShapeDtypeStruct((nbatch, n1, 2 * n2, ch), F32)
    return pl.pallas_call(
        functools.partial(_inner_kernel, conv),
        grid=(nbatch, n1 // FFT_KB),
        in_specs=in_specs, out_specs=out_spec, out_shape=out_shape,
        compiler_params=_cparams(("arbitrary", "arbitrary")),
        name="hyena_dft_inner_conv" if conv else "hyena_dft_inner_spec",
    )(*args)


def _outer_inv_kernel(to_time_major, q_ref, w_ref, *refs):
    if to_time_major:
        perm_ref, xg_ref, z_ref, b_ref, o_ref = refs
    else:
        xg_ref, z_ref, b_ref, o_ref = refs
    outs = []
    for jj in range(FFT_SB):
        qm = jnp.concatenate([q_ref[:, 0, jj, :], q_ref[:, 1, jj, :]], axis=0)
        g = xg_ref[jj] * (_dotc(w_ref[jj], qm) + z_ref[jj] * b_ref[...])
        if to_time_major:
            outs.append(g)
        else:
            o_ref[jj] = g
    if to_time_major:
        y = jnp.concatenate(outs, axis=0)
        h1 = y.astype(BF16)
        r1 = y - h1.astype(F32)
        h2 = r1.astype(BF16)
        h3 = (r1 - h2.astype(F32)).astype(BF16)
        p = perm_ref[...]
        yp = (jnp.dot(p, h1, preferred_element_type=F32) + jnp.dot(p, h2, preferred_element_type=F32)
              + jnp.dot(p, h3, preferred_element_type=F32))
        o_ref[...] = yp.reshape(o_ref.shape)


def _outer_inv_gate(w, q5, xg, z, bias_row, to_time_major):
    nbatch, n2, n1h, ch = z.shape
    slab = pl.BlockSpec((None, FFT_SB, n1h, ch), lambda b, j: (b, j, 0, 0))
    wspec = pl.BlockSpec((FFT_SB, n1h, 2 * FFT_N1), lambda b, j: (j, 0, 0))
    in_specs = [pl.BlockSpec((None, FFT_N1, 2, FFT_SB, ch), lambda b, j: (b, 0, 0, j, 0)), wspec]
    args = [q5, w.astype(BF16)]
    if to_time_major:
        rows = FFT_SB * n1h
        r = np.arange(rows)
        perm = np.zeros((rows, rows), np.float32)
        perm[r, (r % FFT_SB) * n1h + r // FFT_SB] = 1.0
        in_specs.append(pl.BlockSpec((rows, rows), lambda b, j: (0, 0)))
        args.append(jnp.asarray(perm, BF16))
        out_spec = pl.BlockSpec((None, n1h, FFT_SB, ch), lambda b, j: (b, 0, j, 0))
        out_shape = jax.ShapeDtypeStruct((nbatch, n1h, n2, ch), F32)
    else:
        out_spec, out_shape = slab, jax.ShapeDtypeStruct(z.shape, F32)
    in_specs += [slab, slab, pl.BlockSpec((1, ch), lambda b, j: (0, 0))]
    args += [xg, z, bias_row]
    return pl.pallas_call(
        functools.partial(_outer_inv_kernel, to_time_major),
        grid=(nbatch, n2 // FFT_SB),
        in_specs=in_specs, out_specs=out_spec, out_shape=out_shape,
        compiler_params=_cparams(("arbitrary", "arbitrary")),
        name="hyena_dft_outer_inv_gate",
    )(*args)


def hyena_latent(x1, x2, z, g, bias, nbatch, seq):
    n1, n2 = FFT_N1, FFT_N2
    assert 2 * seq == n1 * n2
    w1, w3, g2, g2i = _dft_consts_two_stage()
    half = seq // n2
    spec = _inner_stage(_outer_fwd(w1, g), g2)
    w1h = w1[:, :, 0:half]
    w3h = w3[:, 0:half, :]
    q5 = _inner_stage(_outer_fwd(w1h, z), g2, spec[0], g2i)
    z1 = _outer_inv_gate(w3h, q5, x1, z, bias[0].reshape(1, HY_CH), False)
    q5 = _inner_stage(_outer_fwd(w1h, z1), g2, spec[1], g2i)
    out = _outer_inv_gate(w3h, q5, x2, z1, bias[1].reshape(1, HY_CH), True)
    return out.reshape(nbatch * seq, HY_CH)


def _permute_w_in(w):
    sizes = (128, 128, 256, 256, 16, 256, 256, 256, 256, 16, 256, 256, 256, 768)
    offs = np.cumsum((0,) + sizes)
    seg = lambda j: w[:, offs[j]:offs[j + 1]]
    order = (13, 0, 1, 2, 3, 5, 6, 7, 8, 10, 11, 12, 4, 9)
    pad = jnp.zeros((w.shape[0], P_W - P_AUX - 32), w.dtype)
    return jnp.concatenate([seg(j) for j in order] + [pad], axis=1)


def kernel(x_prompt, x_sample, cache_na_k, cache_na_v, state_gla, state_mlstm_C, state_mlstm_n, state_mlstm_m, c, c_ctx, w_ada, b_ada, norm1_g, norm2_g, w_in, w_out, gla_w_gate, gla_b_gate, gla_norm_g, ml_b_gate, ml_norm_g, na_rpb, hy_conv_w, hy_conv_b, hy_w1, hy_b1, hy_w2, hy_b2, hy_w3, hy_b3, hy_freq, hy_bias, router_w, router_b, w_gu, b_gu, w_dn, b_dn, final_norm_g):
    depth = w_ada.shape[0]
    bp, lp, _ = x_prompt.shape
    bs, ls, _ = x_sample.shape
    tp = bp * lp
    assert tp == ls, "modulation rows are selected per block of DEC_SEQ tokens"
    x = jnp.concatenate([x_prompt.reshape(tp, D), x_sample.reshape(bs * ls, D)], axis=0)
    cond8 = jnp.concatenate([c_ctx[None, :], c, jnp.zeros((8 - 1 - bs, D), F32)], axis=0)
    mods_all = ada_mods(cond8, w_ada, b_ada)
    fg = final_norm_g.reshape(1, D)
    eye_h = jnp.eye(N_HEADS, dtype=F32)
    lc = cache_na_k.shape[3]
    new_k, new_v, new_gla, new_c, new_n, new_m = [], [], [], [], [], []
    for l in range(depth):
        mods = mods_all[l].reshape(8, 1, 6 * D)
        proj = in_proj(x, mods, norm1_g[l].reshape(1, D), _permute_w_in(w_in[l]).astype(BF16), ls)
        gla_c, gla_fin = gla_mixer(proj, 0, bp, lp, gla_w_gate[l], gla_b_gate[l], gla_norm_g[l], None)
        s0t = jnp.einsum('bdhkv,hg->bdhvgk', state_gla[:, l], eye_h).reshape(bs, 2, 256, 128)
        gla_s = gla_mixer(proj, tp, bs, ls, gla_w_gate[l], gla_b_gate[l], gla_norm_g[l], s0t)
        ml_c, cf, nf, mf = mlstm_mixer(proj, 0, bp, lp, ml_b_gate[l], ml_norm_g[l], None)
        c0 = jnp.einsum('bdhvk,hg->bdhkgv', state_mlstm_C[:, l], eye_h).reshape(bs, 2, 256, 256)
        n0 = state_mlstm_n[:, l].reshape(bs, 2, 1, 256)
        m0 = jnp.repeat(state_mlstm_m[:, l], ML_DH, axis=-1).reshape(bs, 2, 1, 256)
        ml_s = mlstm_mixer(proj, tp, bs, ls, ml_b_gate[l], ml_norm_g[l], (c0, n0, m0))
        na_c, k_l, v_l = na_context(proj, bp, lp)
        kct = cache_na_k[:, l].transpose(0, 2, 1, 3).reshape(bs, lc, 256)
        vct = cache_na_v[:, l].transpose(0, 2, 1, 3).reshape(bs, lc, 256)
        na_s = na_latent(proj, tp, bs, ls, kct, vct, na_rpb[l])
        hy_args = (hy_w1[l], hy_b1[l], hy_w2[l], hy_b2[l], hy_w3[l], hy_b3[l], hy_freq[l])
        hy_c = hyena_context(*hyena_short_conv(proj, 0, bp, lp, hy_conv_w[l], hy_conv_b[l]),
                             hyena_filters(lp, *hy_args), hy_bias[l], bp, lp)
        hy_s = hyena_latent(*hyena_short_conv(proj, tp, bs, ls, hy_conv_w[l], hy_conv_b[l], FFT_N2),
                            hyena_filters(ls, *hy_args, FFT_N2), hy_bias[l], bs, ls)
        x1, h2, eidx, ew, cnt_b = out_proj_route(
            (gla_c, ml_c, na_c, hy_c), (gla_s, ml_s, na_s, hy_s), x, mods, norm2_g[l].reshape(1, D), w_out[l].astype(BF16), router_w[l].T,
            jnp.broadcast_to(router_b[l][:, None], (N_EXP, LANES)), ls)
        x = moe_layer(h2, eidx, ew, cnt_b, x1, mods, fg, l, w_gu, b_gu, w_dn, b_dn, ls, l == depth - 1)
        new_k.append(k_l)
        new_v.append(v_l)
        gla_blocks = [gla_fin[:, :, h * GLA_DV:(h + 1) * GLA_DV, h * GLA_DK:(h + 1) * GLA_DK] for h in range(N_HEADS)]
        new_gla.append(jnp.stack(gla_blocks, 2).transpose(0, 1, 2, 4, 3))
        c_blocks = [cf[:, :, h * ML_DH:(h + 1) * ML_DH, h * ML_DH:(h + 1) * ML_DH] for h in range(N_HEADS)]
        new_c.append(jnp.stack(c_blocks, 2).transpose(0, 1, 2, 4, 3))
        new_n.append(nf.reshape(bp, 2, N_HEADS, ML_DH))
        new_m.append(mf.reshape(bp, 2, N_HEADS, ML_DH)[..., 0])
    y_prompt = x[:tp].reshape(bp, lp, D)
    y_sample = x[tp:].reshape(bs, ls, D)
    return (y_prompt, y_sample, jnp.stack(new_k, 1), jnp.stack(new_v, 1), jnp.stack(new_gla, 1),
            jnp.stack(new_c, 1), jnp.stack(new_n, 1), jnp.stack(new_m, 1))
```

```python
import functools
import math

import numpy as np
import jax
import jax.numpy as jnp
from jax import lax
from jax.experimental import pallas as pl
from jax.experimental.pallas import tpu as pltpu

F32 = jnp.float32
BF16 = jnp.bfloat16
HI = lax.Precision.HIGHEST

LANES = 128
SUBLANES = 8
VMEM_LIMIT = 56 * 1024 * 1024

D = 1024
EPS = 1e-6
N_EXP = 32
TOP_K = 4
D_FF = 1024
SWIGLU_LIMIT = 7.0
SWIGLU_ALPHA = 1.702
GRID_W = 64

TB = 256
ROWS_BS = TB * TOP_K + N_EXP * SUBLANES
CH = SUBLANES
NCH = ROWS_BS // CH
TM = 512

P_HU = 0
P_GQ, P_GK, P_GV, P_GG = 768, 896, 1024, 1280
P_MQ, P_MK, P_MV, P_MO = 1536, 1792, 2048, 2304
P_NQ, P_NK, P_NV = 2560, 2816, 3072
P_AUX = 3328
P_W = 3456


def _cparams(sem=None):
    return pltpu.CompilerParams(dimension_semantics=sem, vmem_limit_bytes=VMEM_LIMIT)


def _ada_kernel(c_ref, w_ref, b_ref, o_ref):
    c = c_ref[...]
    s = c * jax.nn.sigmoid(c)
    o_ref[...] = jnp.dot(s, w_ref[...], precision=HI, preferred_element_type=F32) + b_ref[...]


def ada_mods(cond8, w_ada, b_ada):
    depth = w_ada.shape[0]
    tn = 1536
    return pl.pallas_call(
        _ada_kernel,
        grid=(depth, 6 * D // tn),
        in_specs=[pl.BlockSpec((8, D), lambda l, j: (0, 0)),
                  pl.BlockSpec((None, D, tn), lambda l, j: (l, 0, j)),
                  pl.BlockSpec((None, 1, tn), lambda l, j: (l, 0, j))],
        out_specs=pl.BlockSpec((None, 8, tn), lambda l, j: (l, 0, j)),
        out_shape=jax.ShapeDtypeStruct((depth, 8, 6 * D), F32),
        compiler_params=_cparams(("arbitrary", "arbitrary")),
        name="ada_mods",
    )(cond8, w_ada, b_ada.reshape(depth, 1, 6 * D))


def _rms_mod(x, g, sc, sh):
    ms = jnp.mean(x * x, axis=-1, keepdims=True)
    return (x * lax.rsqrt(ms + EPS) * g) * (1.0 + sc) + sh


def _in_kernel(x_ref, mod_ref, g_ref, w_ref, o_ref):
    h = _rms_mod(x_ref[...], g_ref[...], mod_ref[:, D:2 * D], mod_ref[:, 0:D])
    o_ref[...] = jnp.dot(h.astype(BF16), w_ref[...], preferred_element_type=F32)


def in_proj(x, mods, g, w_bf16, tok_per_mod):
    t = x.shape[0]
    tm = 512
    return pl.pallas_call(
        _in_kernel,
        grid=(t // tm,),
        in_specs=[pl.BlockSpec((tm, D), lambda i: (i, 0)),
                  pl.BlockSpec((None, 1, 6 * D), lambda i: (i * tm // tok_per_mod, 0, 0)),
                  pl.BlockSpec((1, D), lambda i: (0, 0)),
                  pl.BlockSpec((D, P_W), lambda i: (0, 0))],
        out_specs=pl.BlockSpec((tm, P_W), lambda i: (i, 0)),
        out_shape=jax.ShapeDtypeStruct((t, P_W), F32),
        compiler_params=_cparams(("arbitrary",)),
        name="in_proj",
    )(x, mods, g, w_bf16)


def _out_kernel(nb_ctx, *refs):
    yc_refs, ys_refs = refs[0:4], refs[4:8]
    x_ref, mod_ref, g_ref, w_ref, rw_ref, rb_ref, x1_ref, h2_ref, eidx_ref, ew_ref, cnt_ref = refs[8:]
    is_ctx = pl.program_id(0) < nb_ctx
    y = jnp.concatenate([jnp.where(is_ctx, c[...], s[...]) for c, s in zip(yc_refs, ys_refs)], axis=1).astype(BF16)
    mix = jnp.dot(y, w_ref[...], preferred_element_type=F32)
    x1 = x_ref[...] + mod_ref[:, 2 * D:3 * D] * mix
    x1_ref[...] = x1
    h2 = _rms_mod(x1, g_ref[...], mod_ref[:, 4 * D:5 * D], mod_ref[:, 3 * D:4 * D])
    h2_ref[...] = h2.astype(BF16)
    lg = lax.dot_general(rw_ref[...], h2, (((1,), (1,)), ((), ())), precision=HI,
                         preferred_element_type=F32) + rb_ref[:, 0:1]
    e_iota = lax.broadcasted_iota(jnp.int32, lg.shape, 0)
    vals, idxs = [], []
    for _ in range(TOP_K):
        m = jnp.max(lg, axis=0, keepdims=True)
        idx = jnp.min(jnp.where(lg == m, e_iota, N_EXP), axis=0, keepdims=True)
        vals.append(m)
        idxs.append(idx)
        lg = jnp.where(e_iota == idx, -jnp.inf, lg)
    ex = [jnp.exp(v - vals[0]) for v in vals]
    den = ex[0] + ex[1] + ex[2] + ex[3]
    eidx_ref[...] = jnp.concatenate(idxs, axis=0)
    ew_ref[...] = jnp.concatenate([e / den for e in ex], axis=0)
    ind = jnp.zeros(lg.shape, F32)
    for idx in idxs:
        ind += (e_iota == idx).astype(F32)
    cnt_ref[...] = jnp.broadcast_to(jnp.sum(ind, axis=1, keepdims=True), (N_EXP, LANES))


def out_proj_route(ys_ctx, ys_lat, x, mods, g, w_bf16, rw_t, rb, tok_per_mod):
    t = x.shape[0]
    nb = t // TB
    nb_ctx = ys_ctx[0].shape[0] // TB
    cspec = pl.BlockSpec((TB, 256), lambda i: (jnp.minimum(i, nb_ctx - 1), 0))
    lspec = pl.BlockSpec((TB, 256), lambda i: (jnp.maximum(i - nb_ctx, 0), 0))
    return pl.pallas_call(
        functools.partial(_out_kernel, nb_ctx),
        grid=(nb,),
        in_specs=[cspec] * 4 + [lspec] * 4 + [
                  pl.BlockSpec((TB, D), lambda i: (i, 0)),
                  pl.BlockSpec((None, 1, 6 * D), lambda i: (i * TB // tok_per_mod, 0, 0)),
                  pl.BlockSpec((1, D), lambda i: (0, 0)),
                  pl.BlockSpec((D, D), lambda i: (0, 0)),
                  pl.BlockSpec((N_EXP, D), lambda i: (0, 0)),
                  pl.BlockSpec((N_EXP, LANES), lambda i: (0, 0))],
        out_specs=[pl.BlockSpec((TB, D), lambda i: (i, 0)),
                   pl.BlockSpec((TB, D), lambda i: (i, 0)),
                   pl.BlockSpec((None, TOP_K, TB), lambda i: (i, 0, 0)),
                   pl.BlockSpec((None, TOP_K, TB), lambda i: (i, 0, 0)),
                   pl.BlockSpec((None, N_EXP, LANES), lambda i: (i, 0, 0))],
        out_shape=[jax.ShapeDtypeStruct((t, D), F32),
                   jax.ShapeDtypeStruct((t, D), BF16),
                   jax.ShapeDtypeStruct((nb, TOP_K, TB), jnp.int32),
                   jax.ShapeDtypeStruct((nb, TOP_K, TB), F32),
                   jax.ShapeDtypeStruct((nb, N_EXP, LANES), F32)],
        compiler_params=_cparams(("arbitrary",)),
        name="out_proj_route",
    )(*ys_ctx, *ys_lat, x, mods, g, w_bf16, rw_t, rb)


def moe_tables(cnt, n_tiles):
    nb = cnt.shape[0]
    cnt8 = (cnt + CH - 1) // CH * CH
    ends = jnp.cumsum(cnt8, axis=1)
    off = ends - cnt8
    nchunks = ends[:, -1] // CH
    tot = jnp.sum(cnt8, axis=0)
    totp = (tot + TM - 1) // TM * TM
    eend = jnp.cumsum(totp)
    estart = eend - totp
    gdst = estart[None, :] + jnp.cumsum(cnt8, axis=0) - cnt8
    r = jnp.arange(NCH, dtype=jnp.int32) * CH
    e_of_c = jnp.minimum(jnp.sum((ends[:, None, :] <= r[None, :, None]).astype(jnp.int32), axis=-1), N_EXP - 1)
    pick = e_of_c[:, :, None] == jnp.arange(N_EXP, dtype=jnp.int32)[None, None, :]
    g_of_c = jnp.sum(jnp.where(pick, gdst[:, None, :], 0), axis=-1)
    o_of_c = jnp.sum(jnp.where(pick, off[:, None, :], 0), axis=-1)
    gchunk = (g_of_c + r[None, :] - o_of_c) // CH
    nused = eend[-1] // TM
    ti = jnp.arange(n_tiles, dtype=jnp.int32)
    tile_e = jnp.sum((eend[None, :] // TM <= jnp.minimum(ti, nused - 1)[:, None]).astype(jnp.int32), axis=-1)
    tile_e = jnp.minimum(tile_e, N_EXP - 1)
    has = totp > 0
    ei = jnp.arange(N_EXP, dtype=jnp.int32)
    later = has[None, :] & (ei[None, :] > ei[:, None])
    next_of = jnp.min(jnp.where(later, ei[None, :], N_EXP), axis=1)
    next_of = jnp.where(next_of == N_EXP, -1, next_of)
    ordinal = jnp.cumsum(has.astype(jnp.int32)) - 1
    tile_next = jnp.sum(jnp.where(tile_e[:, None] == ei[None, :], next_of[None, :], 0), axis=1)
    tile_slot = jnp.sum(jnp.where(tile_e[:, None] == ei[None, :], ordinal[None, :], 0), axis=1) % 2
    i32 = lambda a: a.astype(jnp.int32)
    return dict(off=off, gchunk=i32(gchunk.reshape(-1)), nchunks=i32(nchunks), tile_e=i32(tile_e),
                tile_next=i32(tile_next), tile_slot=i32(tile_slot), nused=i32(nused.reshape(1)),
                pad_lo=i32((estart + tot) // CH), pad_hi=i32(eend // CH))


def _dest_rows(eidx, off_col):
    e_iota = lax.broadcasted_iota(jnp.int32, (N_EXP, TB), 0)
    ohs = [e_iota == eidx[k:k + 1, :] for k in range(TOP_K)]
    ind = jnp.zeros((N_EXP, TB), F32)
    for oh in ohs:
        ind += oh.astype(F32)
    ti = lax.broadcasted_iota(jnp.int32, (TB, TB), 0)
    tj = lax.broadcasted_iota(jnp.int32, (TB, TB), 1)
    upper = (ti <= tj).astype(BF16)
    rank_incl = jnp.dot(ind.astype(BF16), upper, preferred_element_type=F32)
    base = off_col + rank_incl - ind
    return [jnp.sum(jnp.where(oh, base, 0.0), axis=0, keepdims=True).astype(jnp.int32) for oh in ohs]


def _dispatch_kernel(gchunk_ref, nch_ref, plo_ref, phi_ref, nused_ref, h2_ref, eidx_ref, off_ref, xs_ref,
                     xbs_ref, zbuf_ref, sem, zsem):
    blk = pl.program_id(0)
    last = pl.num_programs(0) - 1
    n_tiles = xs_ref.shape[0] // TM

    def zero_fill(start):
        def pad_chunk(c, carry):
            cp = pltpu.make_async_copy(zbuf_ref.at[pl.ds(0, CH), :],
                                       xs_ref.at[pl.ds(pl.multiple_of(c * CH, CH), CH), :], zsem)
            cp.start() if start else cp.wait()
            return carry

        def per_expert(e, carry):
            lax.fori_loop(plo_ref[e], phi_ref[e], pad_chunk, 0)
            return carry

        def tail_tile(t, carry):
            cp = pltpu.make_async_copy(zbuf_ref, xs_ref.at[pl.ds(pl.multiple_of(t * TM, TM), TM), :], zsem)
            cp.start() if start else cp.wait()
            return carry

        lax.fori_loop(0, N_EXP, per_expert, 0)
        lax.fori_loop(nused_ref[0], n_tiles, tail_tile, 0)

    @pl.when(blk == 0)
    def _():
        zbuf_ref[...] = jnp.zeros(zbuf_ref.shape, F32)
        zero_fill(True)

    dests = _dest_rows(eidx_ref[...], off_ref[:, 0:1])
    p_iota = lax.broadcasted_iota(jnp.int32, (ROWS_BS, TB), 0)
    perm = jnp.zeros((ROWS_BS, TB), F32)
    for d in dests:
        perm = jnp.where(p_iota == d, 1.0, perm)
    slot = blk % 2
    xbs_ref[slot] = jnp.dot(perm.astype(BF16), h2_ref[...], preferred_element_type=F32)

    def copy(b, c):
        dst = pl.multiple_of(gchunk_ref[b * NCH + c] * CH, CH)
        src = pl.multiple_of(c * CH, CH)
        return pltpu.make_async_copy(xbs_ref.at[b % 2, pl.ds(src, CH), :], xs_ref.at[pl.ds(dst, CH), :],
                                     sem.at[b % 2])

    def start(c, carry):
        copy(blk, c).start()
        return carry

    lax.fori_loop(0, nch_ref[blk], start, 0)

    def wait_block(b):
        rows = nch_ref[b] * CH
        pltpu.make_async_copy(xbs_ref.at[b % 2, pl.ds(0, rows), :], xs_ref.at[pl.ds(0, rows), :],
                              sem.at[b % 2]).wait()

    @pl.when(blk > 0)
    def _():
        wait_block(blk - 1)

    @pl.when(blk == last)
    def _():
        wait_block(blk)
        zero_fill(False)


def moe_dispatch(h2, eidx, off_b, tb, rows_alloc):
    nb = h2.shape[0] // TB
    return pl.pallas_call(
        _dispatch_kernel,
        grid_spec=pltpu.PrefetchScalarGridSpec(
            num_scalar_prefetch=5,
            grid=(nb,),
            in_specs=[pl.BlockSpec((TB, D), lambda i, *_: (i, 0)),
                      pl.BlockSpec((None, TOP_K, TB), lambda i, *_: (i, 0, 0)),
                      pl.BlockSpec((None, N_EXP, LANES), lambda i, *_: (i, 0, 0))],
            out_specs=pl.BlockSpec(memory_space=pl.ANY),
            scratch_shapes=[pltpu.VMEM((2, ROWS_BS, D), F32), pltpu.VMEM((TM, D), F32),
                            pltpu.SemaphoreType.DMA((2,)), pltpu.SemaphoreType.DMA(())]),
        out_shape=jax.ShapeDtypeStruct((rows_alloc, D), F32),
        compiler_params=_cparams(("arbitrary",)),
        name="moe_dispatch",
    )(tb["gchunk"], tb["nchunks"], tb["pad_lo"], tb["pad_hi"], tb["nused"], h2, eidx, off_b)


def _expert_kernel(layer, te_ref, tnext_ref, tslot_ref, nused_ref, x_ref, bgu_ref, bdn_ref, wgu_hbm, wdn_hbm,
                   y_ref, wgu_f32, wdn_f32, wgu_bf, wdn_bf, sem):
    i = pl.program_id(0)

    def fetch(e, slot, start):
        for k, (src, dst) in enumerate(((wgu_hbm, wgu_f32), (wdn_hbm, wdn_f32))):
            cp = pltpu.make_async_copy(src.at[layer, e], dst.at[slot], sem.at[slot, k])
            cp.start() if start else cp.wait()

    @pl.when(i >= nused_ref[0])
    def _():
        y_ref[...] = jnp.zeros(y_ref.shape, F32)

    @pl.when(i < nused_ref[0])
    def _():
        e = te_ref[i]
        slot = tslot_ref[i]
        first = jnp.logical_or(i == 0, e != te_ref[jnp.maximum(i - 1, 0)])

        @pl.when(i == 0)
        def _():
            fetch(e, slot, True)

        @pl.when(first)
        def _():
            fetch(e, slot, False)
            nxt = tnext_ref[i]

            @pl.when(nxt >= 0)
            def _():
                fetch(nxt, 1 - slot, True)

            wgu_bf[...] = wgu_f32[slot].astype(BF16)
            wdn_bf[...] = wdn_f32[slot].astype(BF16)

        gu = jnp.dot(x_ref[...].astype(BF16), wgu_bf[...], preferred_element_type=F32) + bgu_ref[...]
        g = jnp.minimum(gu[:, 0:D_FF], SWIGLU_LIMIT)
        u = jnp.clip(gu[:, D_FF:2 * D_FF], -SWIGLU_LIMIT, SWIGLU_LIMIT)
        act = g * jax.nn.sigmoid(SWIGLU_ALPHA * g) * (u + 1.0)
        y_ref[...] = jnp.dot(act.astype(BF16), wdn_bf[...], preferred_element_type=F32) + bdn_ref[...]


def moe_experts(xs, tb, layer, w_gu, b_gu, w_dn, b_dn):
    n_tiles = xs.shape[0] // TM
    depth = w_gu.shape[0]

    def xmap(i, te, tn, ts, nu):
        return (jnp.minimum(i, jnp.maximum(nu[0] - 1, 0)), 0)

    def bmap(i, te, tn, ts, nu):
        return (layer, te[i], 0, 0)

    return pl.pallas_call(
        functools.partial(_expert_kernel, layer),
        grid_spec=pltpu.PrefetchScalarGridSpec(
            num_scalar_prefetch=4,
            grid=(n_tiles,),
            in_specs=[pl.BlockSpec((TM, D), xmap),
                      pl.BlockSpec((None, None, 1, 2 * D_FF), bmap),
                      pl.BlockSpec((None, None, 1, D), bmap),
                      pl.BlockSpec(memory_space=pl.ANY),
                      pl.BlockSpec(memory_space=pl.ANY)],
            out_specs=pl.BlockSpec((TM, D), lambda i, *_: (i, 0)),
            scratch_shapes=[pltpu.VMEM((2, D, 2 * D_FF), F32), pltpu.VMEM((2, D_FF, D), F32),
                            pltpu.VMEM((D, 2 * D_FF), BF16), pltpu.VMEM((D_FF, D), BF16),
                            pltpu.SemaphoreType.DMA((2, 2))]),
        out_shape=jax.ShapeDtypeStruct((xs.shape[0], D), F32),
        compiler_params=_cparams(("arbitrary",)),
        name="moe_experts",
    )(tb["tile_e"], tb["tile_next"], tb["tile_slot"], tb["nused"], xs,
      b_gu.reshape(depth, N_EXP, 1, 2 * D_FF), b_dn.reshape(depth, N_EXP, 1, D), w_gu, w_dn)


def _combine_kernel(final, gchunk_ref, nch_ref, eidx_ref, ew_ref, off_ref, x1_ref, mod_ref, fg_ref, ys_ref,
                    o_ref, ybs_ref, sem):
    blk = pl.program_id(0)
    n = nch_ref[blk]
    slot = blk % 2

    def copy(b, c):
        src = pl.multiple_of(gchunk_ref[b * NCH + c] * CH, CH)
        dst = pl.multiple_of(c * CH, CH)
        return pltpu.make_async_copy(ys_ref.at[pl.ds(src, CH), :], ybs_ref.at[b % 2, pl.ds(dst, CH), :],
                                     sem.at[b % 2])

    def fetch_block(b):
        def start(c, carry):
            copy(b, c).start()
            return carry
        lax.fori_loop(0, nch_ref[b], start, 0)

    @pl.when(blk == 0)
    def _():
        fetch_block(blk)

    @pl.when(blk + 1 < pl.num_programs(0))
    def _():
        fetch_block(blk + 1)

    dests = _dest_rows(eidx_ref[...], off_ref[:, 0:1])
    ew = ew_ref[...]
    p_iota = lax.broadcasted_iota(jnp.int32, (ROWS_BS, TB), 0)
    gsel = jnp.zeros((ROWS_BS, TB), F32)
    for k, d in enumerate(dests):
        gsel = jnp.where(p_iota == d, ew[k:k + 1, :], gsel)
    perm = jnp.where(gsel > 0.0, 1.0, 0.0)
    gate_col = jnp.sum(gsel, axis=1, keepdims=True)
    pltpu.make_async_copy(ys_ref.at[pl.ds(0, n * CH), :], ybs_ref.at[slot, pl.ds(0, n * CH), :], sem.at[slot]).wait()
    row_iota = lax.broadcasted_iota(jnp.int32, (ROWS_BS, 1), 0)
    yb = jnp.where(row_iota < n * CH, ybs_ref[slot], 0.0) * gate_col
    moe = lax.dot_general(perm.astype(BF16), yb.astype(BF16), (((0,), (0,)), ((), ())),
                          preferred_element_type=F32)
    x2 = x1_ref[...] + mod_ref[:, 5 * D:6 * D] * moe
    if final:
        ms = jnp.mean(x2 * x2, axis=-1, keepdims=True)
        x2 = x2 * lax.rsqrt(ms + EPS) * fg_ref[...]
    o_ref[...] = x2


def moe_combine(ys, eidx, ew, off_b, x1, mods, fg, gchunk, nchunks, tok_per_mod, final):
    t = x1.shape[0]
    nb = t // TB
    return pl.pallas_call(
        functools.partial(_combine_kernel, final),
        grid_spec=pltpu.PrefetchScalarGridSpec(
            num_scalar_prefetch=2,
            grid=(nb,),
            in_specs=[pl.BlockSpec((None, TOP_K, TB), lambda i, *_: (i, 0, 0)),
                      pl.BlockSpec((None, TOP_K, TB), lambda i, *_: (i, 0, 0)),
                      pl.BlockSpec((None, N_EXP, LANES), lambda i, *_: (i, 0, 0)),
                      pl.BlockSpec((TB, D), lambda i, *_: (i, 0)),
                      pl.BlockSpec((None, 1, 6 * D), lambda i, *_: (i * TB // tok_per_mod, 0, 0)),
                      pl.BlockSpec((1, D), lambda i, *_: (0, 0)),
                      pl.BlockSpec(memory_space=pl.ANY)],
            out_specs=pl.BlockSpec((TB, D), lambda i, *_: (i, 0)),
            scratch_shapes=[pltpu.VMEM((2, ROWS_BS, D), F32), pltpu.SemaphoreType.DMA((2,))]),
        out_shape=jax.ShapeDtypeStruct((t, D), F32),
        compiler_params=_cparams(("arbitrary",)),
        name="moe_combine",
    )(gchunk, nchunks, eidx, ew, off_b, x1, mods, fg, ys)


def moe_layer(h2, eidx, ew, cnt_b, x1, mods, fg, layer, w_gu, b_gu, w_dn, b_dn, tok_per_mod, final):
    t = h2.shape[0]
    nb = t // TB
    max_rows = t * TOP_K + nb * N_EXP * (CH - 1) + N_EXP * (TM - CH)
    n_tiles = (max_rows + TM - 1) // TM
    cnt = cnt_b[:, :, 0].astype(jnp.int32)
    tb = moe_tables(cnt, n_tiles)
    off_b = jnp.broadcast_to(tb["off"].astype(F32)[:, :, None], (nb, N_EXP, LANES))
    xs = moe_dispatch(h2, eidx, off_b, tb, n_tiles * TM)
    ys = moe_experts(xs, tb, layer, w_gu, b_gu, w_dn, b_dn)
    return moe_combine(ys, eidx, ew, off_b, x1, mods, fg, tb["gchunk"], tb["nchunks"], tok_per_mod, final)


N_HEADS = 4


def _stack_heads(x, head_w):
    lane_h = lax.broadcasted_iota(jnp.int32, x.shape, 1) // head_w
    return jnp.concatenate([jnp.where(lane_h == h, x, 0.0) for h in range(N_HEADS)], axis=0)


def _unstack_heads(xs, head_w):
    r = xs.shape[0] // N_HEADS
    lane_h = lax.broadcasted_iota(jnp.int32, (r, xs.shape[1]), 1) // head_w
    out = jnp.zeros((r, xs.shape[1]), F32)
    for h in range(N_HEADS):
        out = jnp.where(lane_h == h, xs[h * r:(h + 1) * r, :], out)
    return out


def _block_diag_mask(rows, cols, rw, cw):
    ri = lax.broadcasted_iota(jnp.int32, (rows, cols), 0) // rw
    ci = lax.broadcasted_iota(jnp.int32, (rows, cols), 1) // cw
    return ri == ci


def _head_rmsnorm(o, head_w):
    n = o.shape[1]
    bd = _block_diag_mask(n, n, head_w, head_w).astype(F32)
    ms = jnp.dot(o * o, bd, precision=HI, preferred_element_type=F32) * (1.0 / head_w)
    return o * lax.rsqrt(ms + EPS)


def _nt(a, b, **kw):
    return lax.dot_general(a, b, (((1,), (1,)), ((), ())), preferred_element_type=F32, **kw)


def _tn(a, b, **kw):
    return lax.dot_general(a, b, (((0,), (0,)), ((), ())), preferred_element_type=F32, **kw)


NA_DH = 64
NA_WIN_R = 8
NA_WIN_C = 16
NA_ROWS = 64


NA_RPS = 4


def _na_window_start(r):
    return jnp.clip(r - NA_WIN_R // 2, 0, NA_ROWS - NA_WIN_R)


def _na_lat_kernel(q_ref, k_ref, v_ref, kc_ref, vc_ref, *refs):
    bias_refs, o_ref = refs[:NA_RPS], refs[NA_RPS]
    kc = kc_ref[...].astype(BF16)
    vc = vc_ref[...].astype(BF16)
    for t in range(NA_RPS):
        start = _na_window_start(pl.program_id(1) * NA_RPS + t)
        rows = pl.ds(pl.multiple_of(start * GRID_W, GRID_W), NA_WIN_R * GRID_W)
        tok = slice(t * GRID_W, (t + 1) * GRID_W)
        qs = _stack_heads(q_ref[tok, :] * (NA_DH ** -0.5), NA_DH).astype(BF16)
        k_all = jnp.concatenate([k_ref[rows, :].astype(BF16), kc], axis=0)
        v_all = jnp.concatenate([v_ref[rows, :].astype(BF16), vc], axis=0)
        s = _nt(qs, k_all) + bias_refs[t][...]
        m = jnp.max(s, axis=1, keepdims=True)
        p = jnp.exp(s - m)
        den = jnp.sum(p, axis=1, keepdims=True)
        o = jnp.dot(p.astype(BF16), v_all, preferred_element_type=F32)
        o_ref[tok, :] = _unstack_heads(o / den, NA_DH)


def _na_bias_table(rpb):
    col = np.arange(GRID_W)
    c_start = np.clip(col - NA_WIN_C // 2, 0, GRID_W - NA_WIN_C)
    col_mask = (col[None, :] >= c_start[:, None]) & (col[None, :] < c_start[:, None] + NA_WIN_C)
    c_idx = np.clip(col[None, :] - col[:, None], -(NA_WIN_C - 1), NA_WIN_C - 1) + (NA_WIN_C - 1)
    onehot = jnp.asarray(c_idx[None, :, :] == np.arange(2 * NA_WIN_C - 1)[:, None, None], F32)
    tb = jnp.einsum('hrc,cqk->hrqk', rpb, onehot, precision=HI)
    tb = jnp.where(col_mask[None, None], tb, -jnp.inf)
    out = []
    for ri0 in range(NA_WIN_R):
        blk = tb[:, ri0:ri0 + NA_WIN_R]
        out.append(blk.transpose(0, 2, 1, 3).reshape(N_HEADS * GRID_W, NA_WIN_R * GRID_W))
    return jnp.stack(out, 0)


def na_latent(proj, tok0, bs, ls, kc, vc, rpb):
    lc = kc.shape[1]
    bias = _na_bias_table(rpb)
    bias = jnp.concatenate([bias, jnp.zeros(bias.shape[:2] + (lc,), F32)], axis=-1)
    nstep = ls // GRID_W // NA_RPS
    tq = NA_RPS * GRID_W
    rb0 = tok0 // tq
    sb0 = tok0 // ls

    def bias_spec(t):
        def imap(b, s):
            r = s * NA_RPS + t
            return (_na_window_start(r) - r + NA_WIN_R - 1, 0, 0)
        return pl.BlockSpec((None, N_HEADS * GRID_W, NA_WIN_R * GRID_W + lc), imap)

    return pl.pallas_call(
        _na_lat_kernel,
        grid=(bs, nstep),
        in_specs=[pl.BlockSpec((tq, 256), lambda b, s: (rb0 + b * nstep + s, P_NQ // 256)),
                  pl.BlockSpec((ls, 256), lambda b, s: (sb0 + b, P_NK // 256)),
                  pl.BlockSpec((ls, 256), lambda b, s: (sb0 + b, P_NV // 256)),
                  pl.BlockSpec((None, lc, 256), lambda b, s: (b, 0, 0)),
                  pl.BlockSpec((None, lc, 256), lambda b, s: (b, 0, 0))] + [bias_spec(t) for t in range(NA_RPS)],
        out_specs=pl.BlockSpec((tq, 256), lambda b, s: (b * nstep + s, 0)),
        out_shape=jax.ShapeDtypeStruct((bs * ls, 256), F32),
        compiler_params=_cparams(("arbitrary", "arbitrary")),
        name="na_latent",
    )(proj, proj, proj, kc, vc, *([bias] * NA_RPS))


def _na_ctx_kernel(q_ref, k_ref, v_ref, o_ref, kc_ref, vc_ref):
    qs = _stack_heads(q_ref[...] * (NA_DH ** -0.5), NA_DH).astype(BF16)
    k = k_ref[...]
    v = v_ref[...]
    s = _nt(qs, k.astype(BF16))
    m = jnp.max(s, axis=1, keepdims=True)
    p = jnp.exp(s - m)
    den = jnp.sum(p, axis=1, keepdims=True)
    o = jnp.dot(p.astype(BF16), v.astype(BF16), preferred_element_type=F32)
    o_ref[...] = _unstack_heads(o / den, NA_DH)
    for h in range(N_HEADS):
        kc_ref[h] = k[:, h * NA_DH:(h + 1) * NA_DH]
        vc_ref[h] = v[:, h * NA_DH:(h + 1) * NA_DH]


def na_context(proj, bp, lp):
    cache_spec = pl.BlockSpec((None, N_HEADS, lp, NA_DH), lambda b: (b, 0, 0, 0))
    cache_shape = jax.ShapeDtypeStruct((bp, N_HEADS, lp, NA_DH), F32)
    return pl.pallas_call(
        _na_ctx_kernel,
        grid=(bp,),
        in_specs=[pl.BlockSpec((lp, 256), lambda b: (b, P_NQ // 256)),
                  pl.BlockSpec((lp, 256), lambda b: (b, P_NK // 256)),
                  pl.BlockSpec((lp, 256), lambda b: (b, P_NV // 256))],
        out_specs=[pl.BlockSpec((lp, 256), lambda b: (b, 0)), cache_spec, cache_spec],
        out_shape=[jax.ShapeDtypeStruct((bp * lp, 256), F32), cache_shape, cache_shape],
        compiler_params=_cparams(("arbitrary",)),
        name="na_context",
    )(proj, proj, proj)


GLA_DK = 32
GLA_DV = 64
GLA_C = 128
GLA_NORMALIZER = 16.0


def _gla_kernel(has_state, seq, *refs):
    if has_state:
        q_ref, k_ref, v_ref, g_ref, aux_ref, wg_ref, bg_ref, gn_ref, s0_ref, y_ref, acc_ref, bcum_ref = refs
    else:
        q_ref, k_ref, v_ref, g_ref, aux_ref, wg_ref, bg_ref, gn_ref, y_ref, sfin_ref, acc_ref, bcum_ref = refs
    c_sz = min(GLA_C, seq)
    n_chunks = seq // c_sz
    mid = c_sz // 2
    hk = N_HEADS * GLA_DK
    ti = lax.broadcasted_iota(jnp.int32, (c_sz, c_sz), 0)
    tj = lax.broadcasted_iota(jnp.int32, (c_sz, c_sz), 1)
    ai = lax.broadcasted_iota(jnp.int32, (N_HEADS * c_sz, c_sz), 0) % c_sz
    aj = lax.broadcasted_iota(jnp.int32, (N_HEADS * c_sz, c_sz), 1)
    bd = _block_diag_mask(N_HEADS * GLA_DV, hk, GLA_DV, GLA_DK)
    tri2 = jnp.concatenate([(tj <= ti).astype(BF16), (tj >= ti).astype(BF16)], axis=0)
    n_lr = wg_ref.shape[0]

    def pre(c, carry):
        rows = pl.ds(pl.multiple_of(c * c_sz, c_sz), c_sz)
        la = jax.nn.log_sigmoid(jnp.dot(aux_ref[rows, 0:n_lr], wg_ref[...], precision=HI, preferred_element_type=F32)
                                + bg_ref[...]) * (1.0 / GLA_NORMALIZER)
        l1 = la.astype(BF16)
        r1 = la - l1.astype(F32)
        l2 = r1.astype(BF16)
        l3 = (r1 - l2.astype(F32)).astype(BF16)
        bb = (jnp.dot(tri2, l1, preferred_element_type=F32) + jnp.dot(tri2, l2, preferred_element_type=F32)
              + jnp.dot(tri2, l3, preferred_element_type=F32))
        bcum_ref[rows, 0:hk] = bb[0:c_sz, 0:hk]
        bcum_ref[rows, hk:2 * hk] = bb[c_sz:2 * c_sz, hk:2 * hk]
        return carry

    lax.fori_loop(0, n_chunks, pre, 0)

    def step(d, c, st):
        amask = (aj <= ai) if d == 0 else (aj >= ai)
        rows = pl.ds(pl.multiple_of(c * c_sz, c_sz), c_sz)
        q = q_ref[rows, :] * (GLA_DK ** -0.5)
        k = k_ref[rows, :]
        v = v_ref[rows, :].astype(BF16)
        b = bcum_ref[rows, d * hk:(d + 1) * hk]
        btot = b[c_sz - 1:c_sz, :] if d == 0 else b[0:1, :]
        ref = b[mid - 1:mid, :] if d == 0 else b[mid:mid + 1, :]
        qt = q * jnp.exp(b - ref)
        kt = (k * jnp.exp(ref - b)).astype(BF16)
        ke = (k * jnp.exp(btot - b)).astype(BF16)
        a = _nt(_stack_heads(qt, GLA_DK).astype(BF16), kt)
        a = jnp.where(amask, a, 0.0).astype(BF16)
        o = _unstack_heads(jnp.dot(a, v, preferred_element_type=F32), GLA_DV)
        o += _nt((qt * jnp.exp(ref)).astype(BF16), st.astype(BF16))
        acc_ref[rows, :] += o
        upd = _tn(v, ke)
        return st * jnp.exp(btot) + jnp.where(bd, upd, 0.0)

    def body(n, sts):
        return step(0, n, sts[0]), step(1, n_chunks - 1 - n, sts[1])

    acc_ref[...] = jnp.zeros(acc_ref.shape, F32)
    if has_state:
        st0 = (s0_ref[0], s0_ref[1])
    else:
        st0 = (jnp.zeros((N_HEADS * GLA_DV, hk), F32),) * 2
    st_f, st_b = lax.fori_loop(0, n_chunks, body, st0)
    if not has_state:
        sfin_ref[0] = st_f
        sfin_ref[1] = st_b
    gn = gn_ref[...]

    def epi(i, carry):
        rows = pl.ds(pl.multiple_of(i * 256, 256), 256)
        g = g_ref[rows, :]
        y_ref[rows, :] = _head_rmsnorm(acc_ref[rows, :], GLA_DV) * gn * (g * jax.nn.sigmoid(g))
        return carry

    lax.fori_loop(0, seq // 256, epi, 0)


def gla_mixer(proj, tok0, nbatch, seq, w_gate, b_gate, norm_g, s0t):
    has_state = s0t is not None
    sb = tok0 // seq
    hk = N_HEADS * GLA_DK
    wg = jnp.concatenate([w_gate[0], w_gate[1]], axis=1)
    n_lr = wg.shape[0]
    col = lambda w, c0: pl.BlockSpec((seq, w), lambda b: (sb + b, c0 // w), pipeline_mode=pl.Buffered(1))
    in_specs = [col(128, P_GQ), col(128, P_GK), col(256, P_GV), col(256, P_GG), col(128, P_AUX),
                pl.BlockSpec((n_lr, 2 * hk), lambda b: (0, 0)),
                pl.BlockSpec((1, 2 * hk), lambda b: (0, 0)),
                pl.BlockSpec((1, 256), lambda b: (0, 0))]
    args = [proj, proj, proj, proj, proj, wg, b_gate.reshape(1, 2 * hk), jnp.tile(norm_g, N_HEADS).reshape(1, 256)]
    y_spec = pl.BlockSpec((seq, 256), lambda b: (b, 0))
    y_shape = jax.ShapeDtypeStruct((nbatch * seq, 256), F32)
    st_spec = pl.BlockSpec((None, 2, 256, 128), lambda b: (b, 0, 0, 0))
    if has_state:
        in_specs.append(st_spec)
        args.append(s0t)
        out_specs, out_shape = y_spec, y_shape
    else:
        out_specs = [y_spec, st_spec]
        out_shape = [y_shape, jax.ShapeDtypeStruct((nbatch, 2, 256, 128), F32)]
    return pl.pallas_call(
        functools.partial(_gla_kernel, has_state, seq),
        grid=(nbatch,),
        in_specs=in_specs, out_specs=out_specs, out_shape=out_shape,
        scratch_shapes=[pltpu.VMEM((seq, 256), F32), pltpu.VMEM((seq, 2 * hk), F32)],
        compiler_params=_cparams(("arbitrary",)),
        name="gla_latent" if has_state else "gla_context",
    )(*args)


ML_DH = 64
ML_C = 256
ROPE_BASE = 10000.0
ML_GATE_LANE0 = 16


def _ml_gate_selectors():
    rep = np.zeros((2, 2, LANES, N_HEADS * ML_DH), np.float32)
    sel = np.zeros((2, 8, LANES), np.float32)
    for d in range(2):
        for g in range(2):
            for h in range(N_HEADS):
                lane = ML_GATE_LANE0 + d * 8 + g * 4 + h
                rep[d, g, lane, h * ML_DH:(h + 1) * ML_DH] = 1.0
                sel[d, g * 4 + h, lane] = 1.0
    return jnp.asarray(rep), jnp.asarray(sel)


def _rope_tables(seq):
    nf = ML_DH // 4
    inv = ROPE_BASE ** (-jnp.arange(nf, dtype=F32) / nf)
    t = np.arange(seq)
    j = np.arange(ML_DH)
    pos = np.where(j[None, :] < ML_DH // 2, (t // GRID_W)[:, None], (t % GRID_W)[:, None]).astype(np.float32)
    ang = jnp.asarray(pos) * inv[j % nf][None, :]
    first = (j % (ML_DH // 2)) < nf
    cos = jnp.tile(jnp.cos(ang), (1, N_HEADS))
    sin = jnp.tile(jnp.where(first[None, :], -jnp.sin(ang), jnp.sin(ang)), (1, N_HEADS))
    return cos, sin


def _rope(x, cos, sin_signed):
    nf = ML_DH // 4
    first = (lax.broadcasted_iota(jnp.int32, x.shape, 1) % (ML_DH // 2)) < nf
    partner = jnp.where(first, pltpu.roll(x, x.shape[1] - nf, 1), pltpu.roll(x, nf, 1))
    return x * cos + partner * sin_signed


def _mlstm_kernel(latent, seq, *refs):
    if latent:
        (q_ref, k_ref, v_ref, og_ref, aux_ref, rep_ref, sel_ref, brep_ref, bsel_ref, gn_ref, cos_ref, sin_ref,
         c0_ref, n0_ref, m0_ref, y_ref, acc_ref) = refs
    else:
        (q_ref, k_ref, v_ref, og_ref, aux_ref, rep_ref, sel_ref, brep_ref, bsel_ref, gn_ref,
         y_ref, cf_ref, nf_ref, mf_ref, acc_ref) = refs
    c_sz = min(ML_C, seq)
    n_chunks = seq // c_sz
    hw = N_HEADS * ML_DH
    ti = lax.broadcasted_iota(jnp.int32, (c_sz, c_sz), 0)
    tj = lax.broadcasted_iota(jnp.int32, (c_sz, c_sz), 1)
    bd = _block_diag_mask(hw, hw, ML_DH, ML_DH)
    for d in range(2):
        causal = (tj <= ti) if d == 0 else (tj >= ti)
        tri = causal.astype(F32)
        tri_t = ((ti <= tj) if d == 0 else (ti >= tj)).astype(F32)

        def body(n, carry, d=d, causal=causal, tri=tri, tri_t=tri_t):
            cm, nrow, mrow = carry
            c = n if d == 0 else n_chunks - 1 - n
            rows = pl.ds(pl.multiple_of(c * c_sz, c_sz), c_sz)
            q = q_ref[rows, :]
            k = k_ref[rows, :] * (ML_DH ** -0.5)
            if latent:
                q = _rope(q, cos_ref[rows, :], sin_ref[rows, :])
                k = _rope(k, cos_ref[rows, :], sin_ref[rows, :])
            v = v_ref[rows, :].astype(BF16)
            aux = aux_ref[rows, :]
            li_m = jnp.dot(aux, rep_ref[d, 0], precision=HI, preferred_element_type=F32) + brep_ref[d, 0]
            lf_m = jax.nn.log_sigmoid(jnp.dot(aux, rep_ref[d, 1], precision=HI, preferred_element_type=F32)
                                      + brep_ref[d, 1])
            f_m = jnp.dot(tri, lf_m, precision=HI, preferred_element_type=F32)
            g_t = _nt(sel_ref[d], aux, precision=HI) + bsel_ref[d][:, 0:1]
            li_t = g_t[0:N_HEADS, :]
            f_t = jnp.dot(jax.nn.log_sigmoid(g_t[N_HEADS:2 * N_HEADS, :]), tri_t, precision=HI,
                          preferred_element_type=F32)
            dms, fcols, mcols = [], [], []
            for h in range(N_HEADS):
                fcol = f_m[:, h * ML_DH:h * ML_DH + 1]
                dms.append(jnp.where(causal, fcol - f_t[h:h + 1, :] + li_t[h:h + 1, :], -jnp.inf))
                fcols.append(fcol)
                mcols.append(jnp.broadcast_to(mrow[:, h * ML_DH:h * ML_DH + 1], (c_sz, 1)))
            dm = jnp.concatenate(dms, axis=0)
            log_inter = jnp.concatenate(fcols, axis=0) + jnp.concatenate(mcols, axis=0)
            m_t = jnp.maximum(log_inter, jnp.max(dm, axis=1, keepdims=True))
            qs = _stack_heads(q, ML_DH)
            qsb = qs.astype(BF16)
            s = _nt(qsb, k.astype(BF16)) * jnp.exp(dm - m_t)
            a_t = jnp.exp(log_inter - m_t)
            inter = jnp.dot(qsb, cm.astype(BF16), preferred_element_type=F32)
            num = a_t * inter + jnp.dot(s.astype(BF16), v, preferred_element_type=F32)
            den = a_t * jnp.sum(qs * nrow, axis=1, keepdims=True) + jnp.sum(s, axis=1, keepdims=True)
            hst = num / jnp.maximum(jnp.abs(den), jnp.exp(-m_t))
            hout = _unstack_heads(hst, ML_DH)
            if d == 0:
                acc_ref[rows, :] = hout
            else:
                acc_ref[rows, :] += hout
            f_tot = f_m[c_sz - 1:c_sz, :] if d == 0 else f_m[0:1, :]
            w_end = f_tot - f_m + li_m
            m_new = jnp.maximum(f_tot + mrow, jnp.max(w_end, axis=0, keepdims=True))
            a = jnp.exp(f_tot + mrow - m_new)
            kw = k * jnp.exp(w_end - m_new)
            cm_new = cm * a + jnp.where(bd, _tn(kw.astype(BF16), v), 0.0)
            n_new = nrow * a + jnp.sum(kw, axis=0, keepdims=True)
            return cm_new, n_new, m_new

        if latent:
            init = (c0_ref[d], n0_ref[d], m0_ref[d])
        else:
            init = (jnp.zeros((hw, hw), F32), jnp.zeros((1, hw), F32), jnp.zeros((1, hw), F32))
        cm, nrow, mrow = lax.fori_loop(0, n_chunks, body, init)
        if not latent:
            cf_ref[d] = cm
            nf_ref[d] = nrow
            mf_ref[d] = mrow
    gn = gn_ref[...]

    def epi(i, carry):
        rows = pl.ds(pl.multiple_of(i * 256, 256), 256)
        y_ref[rows, :] = _head_rmsnorm(acc_ref[rows, :], ML_DH) * gn * jax.nn.sigmoid(og_ref[rows, :])
        return carry

    lax.fori_loop(0, seq // 256, epi, 0)


def mlstm_mixer(proj, tok0, nbatch, seq, b_gate, norm_g, state):
    latent = state is not None
    sb = tok0 // seq
    hw = N_HEADS * ML_DH
    rep, sel = _ml_gate_selectors()
    brep = jnp.repeat(b_gate.reshape(2, 2, N_HEADS), ML_DH, axis=-1).reshape(2, 2, 1, hw)
    bsel = jnp.broadcast_to(b_gate.reshape(2, 8, 1), (2, 8, LANES))
    col = lambda c0: pl.BlockSpec((seq, 256), lambda b: (sb + b, c0 // 256), pipeline_mode=pl.Buffered(1))
    full = lambda shape: pl.BlockSpec(shape, lambda b: (0,) * len(shape), pipeline_mode=pl.Buffered(1))
    in_specs = [col(P_MQ), col(P_MK), col(P_MV), col(P_MO),
                pl.BlockSpec((seq, 128), lambda b: (sb + b, P_AUX // 128), pipeline_mode=pl.Buffered(1)),
                full((2, 2, LANES, hw)), full((2, 8, LANES)), full((2, 2, 1, hw)), full((2, 8, LANES)), full((1, hw))]
    args = [proj, proj, proj, proj, proj, rep, sel, brep, bsel, norm_g.reshape(1, hw)]
    y_spec = pl.BlockSpec((seq, 256), lambda b: (b, 0))
    y_shape = jax.ShapeDtypeStruct((nbatch * seq, 256), F32)
    c_spec = pl.BlockSpec((None, 2, hw, hw), lambda b: (b, 0, 0, 0))
    r_spec = pl.BlockSpec((None, 2, 1, hw), lambda b: (b, 0, 0, 0))
    if latent:
        cos, sin = _rope_tables(seq)
        in_specs += [full((seq, hw)), full((seq, hw)), c_spec, r_spec, r_spec]
        args += [cos, sin, *state]
        out_specs, out_shape = y_spec, y_shape
    else:
        out_specs = [y_spec, c_spec, r_spec, r_spec]
        out_shape = [y_shape, jax.ShapeDtypeStruct((nbatch, 2, hw, hw), F32),
                     jax.ShapeDtypeStruct((nbatch, 2, 1, hw), F32), jax.ShapeDtypeStruct((nbatch, 2, 1, hw), F32)]
    return pl.pallas_call(
        functools.partial(_mlstm_kernel, latent, seq),
        grid=(nbatch,),
        in_specs=in_specs, out_specs=out_specs, out_shape=out_shape,
        scratch_shapes=[pltpu.VMEM((seq, 256), F32)],
        compiler_params=_cparams(("arbitrary",)),
        name="mlstm_latent" if latent else "mlstm_context",
    )(*args)


HY_CH = 256
HY_BANDS = 16
HY_EMB = 1 + 2 * HY_BANDS
HY_FFN = 64
FFT_N1 = 64
FFT_N2 = 128


HY_LANE_FWD = HY_EMB
HY_LANE_BWD = HY_EMB + 1


def _hy_filter_kernel(slab, feat_ref, w1_ref, b1_ref, w2_ref, b2_ref, w3_ref, b3_ref, fr_ref, dl_ref, o_ref):
    feats = feat_ref[...]
    a = jnp.sin(fr_ref[0:1, :] * (jnp.dot(feats, w1_ref[...], precision=HI, preferred_element_type=F32) + b1_ref[...]))
    a = jnp.sin(fr_ref[1:2, :] * (jnp.dot(a, w2_ref[...], precision=HI, preferred_element_type=F32) + b2_ref[...]))
    a = jnp.dot(a, w3_ref[...], precision=HI, preferred_element_type=F32) + b3_ref[...]
    a = a * jnp.exp(-feats[:, 0:1] * dl_ref[...])
    fwd = feats[:, HY_LANE_FWD:HY_LANE_FWD + 1]
    bwd = feats[:, HY_LANE_BWD:HY_LANE_BWD + 1]
    for order in range(2):
        h = (fwd * a[:, (2 * order) * HY_CH:(2 * order + 1) * HY_CH]
             + bwd * a[:, (2 * order + 1) * HY_CH:(2 * order + 2) * HY_CH])
        if slab:
            for j in range(h.shape[0] // slab):
                o_ref[order, :, j, :] = h[j * slab:(j + 1) * slab, :]
        else:
            o_ref[order] = h


def hyena_filters(seq, w1, b1, w2, b2, w3, b3, freq, slab=0):
    n = jnp.arange(2 * seq)
    t = jnp.where(n < seq, n, 2 * seq - n).astype(F32)
    t = jnp.where(n == seq, 0.0, t)
    t_unit = t / (seq - 1)
    bands = jnp.linspace(1e-4, HY_BANDS - 1, HY_BANDS, dtype=F32)
    ang = (2.0 * math.pi / seq) * t[:, None] * bands[None, :]
    feats = jnp.concatenate([t_unit[:, None], jnp.cos(ang), -jnp.sin(ang),
                             (n < seq).astype(F32)[:, None], (n > seq).astype(F32)[:, None],
                             jnp.zeros((2 * seq, LANES - HY_EMB - 2), F32)], axis=-1)
    w1p = jnp.zeros((LANES, HY_FFN), F32).at[0:HY_EMB].set(w1)
    deltas = jnp.abs(jnp.linspace(math.log(1e-2) / 1.5, math.log(1e-2) / 0.3, HY_CH, dtype=F32))
    rb = min(2 * seq, 1024 if slab else 512)
    full = lambda shape: pl.BlockSpec(shape, lambda i: (0,) * len(shape))
    if slab:
        out_spec = pl.BlockSpec((2, slab, rb // slab, HY_CH), lambda i: (0, 0, i, 0))
        out_shape = jax.ShapeDtypeStruct((2, slab, 2 * seq // slab, HY_CH), F32)
    else:
        out_spec = pl.BlockSpec((2, rb, HY_CH), lambda i: (0, i, 0))
        out_shape = jax.ShapeDtypeStruct((2, 2 * seq, HY_CH), F32)
    return pl.pallas_call(
        functools.partial(_hy_filter_kernel, slab),
        grid=(2 * seq // rb,),
        in_specs=[pl.BlockSpec((rb, LANES), lambda i: (i, 0)), full((LANES, HY_FFN)), full((1, HY_FFN)),
                  full((HY_FFN, HY_FFN)), full((1, HY_FFN)), full((HY_FFN, 4 * HY_CH)), full((1, 4 * HY_CH)),
                  full((2, HY_FFN)), full((1, 4 * HY_CH))],
        out_specs=out_spec, out_shape=out_shape,
        compiler_params=_cparams(("arbitrary",)),
        name="hyena_filters",
    )(feats, w1p, b1.reshape(1, -1), w2, b2.reshape(1, -1), w3, b3.reshape(1, -1), freq,
      jnp.tile(deltas, 4).reshape(1, -1))


def _hy_short_kernel(nblk, slab, u_ref, prev_ref, next_ref, w_ref, b_ref, x1_ref, x2_ref, z_ref):
    i = pl.program_id(1)
    u = u_ref[...]
    rb = u.shape[0]
    row = lax.broadcasted_iota(jnp.int32, u.shape, 0)
    prev_row = jnp.where(i > 0, prev_ref[SUBLANES - 1:SUBLANES, :], 0.0)
    next_row = jnp.where(i < nblk - 1, next_ref[0:1, :], 0.0)
    up = jnp.where(row == 0, prev_row, pltpu.roll(u, 1, 0))
    un = jnp.where(row == rb - 1, next_row, pltpu.roll(u, rb - 1, 0))
    y = up * w_ref[0:1, :] + u * w_ref[1:2, :] + un * w_ref[2:3, :] + b_ref[...]
    for k, o_ref in enumerate((x1_ref, x2_ref, z_ref)):
        if slab:
            for a in range(rb // slab):
                o_ref[:, a, :] = y[a * slab:(a + 1) * slab, k * HY_CH:(k + 1) * HY_CH]
        else:
            o_ref[...] = y[:, k * HY_CH:(k + 1) * HY_CH]


def hyena_short_conv(proj, tok0, nbatch, seq, w, b, slab=0):
    rb = min(seq, 1024 if slab else 512)
    nblk = seq // rb
    r0 = tok0 // rb
    h0 = tok0 // SUBLANES
    hpb = rb // SUBLANES
    last = (tok0 + nbatch * seq) // SUBLANES - 1
    if slab:
        o_spec = pl.BlockSpec((None, slab, rb // slab, HY_CH), lambda bb, i: (bb, 0, i, 0))
        o_shape = jax.ShapeDtypeStruct((nbatch, slab, seq // slab, HY_CH), F32)
    else:
        o_spec = pl.BlockSpec((rb, HY_CH), lambda bb, i: (bb * nblk + i, 0))
        o_shape = jax.ShapeDtypeStruct((nbatch * seq, HY_CH), F32)
    return pl.pallas_call(
        functools.partial(_hy_short_kernel, nblk, slab),
        grid=(nbatch, nblk),
        in_specs=[pl.BlockSpec((rb, 3 * HY_CH), lambda bb, i: (r0 + bb * nblk + i, P_HU // (3 * HY_CH))),
                  pl.BlockSpec((SUBLANES, 3 * HY_CH),
                               lambda bb, i: (jnp.maximum(h0 + (bb * nblk + i) * hpb - 1, 0), P_HU // (3 * HY_CH))),
                  pl.BlockSpec((SUBLANES, 3 * HY_CH),
                               lambda bb, i: (jnp.minimum(h0 + (bb * nblk + i + 1) * hpb, last), P_HU // (3 * HY_CH))),
                  pl.BlockSpec((3, 3 * HY_CH), lambda bb, i: (0, 0)),
                  pl.BlockSpec((1, 3 * HY_CH), lambda bb, i: (0, 0))],
        out_specs=[o_spec, o_spec, o_spec],
        out_shape=[o_shape, o_shape, o_shape],
        compiler_params=_cparams(("arbitrary", "arbitrary")),
        name="hyena_short_conv",
    )(proj, proj, proj, w, b.reshape(1, -1))


def _dft_consts_single(seq):
    n = 2 * seq
    k = np.arange(n)[:, None].astype(np.float64)
    m = np.arange(n)[None, :].astype(np.float64)
    ang = 2.0 * np.pi * k * m / n
    fwd = np.concatenate([np.cos(ang), -np.sin(ang)], axis=0)
    inv = np.concatenate([np.cos(ang.T[:seq]), -np.sin(ang.T[:seq])], axis=1) / n
    return (jnp.asarray(fwd, F32), jnp.asarray(fwd[:, :seq], F32), jnp.asarray(inv, F32))


def _cmul(zr, zi, hr, hi):
    return zr * hr - zi * hi, zr * hi + zi * hr


def _split_bf16(a):
    hi = a.astype(BF16)
    return hi, (a - hi.astype(F32)).astype(BF16)


def _dot3(a, b, dims=None):
    if dims is None:
        dims = (((a.ndim - 1,), (0,)), ((), ()))
    a_hi, a_lo = _split_bf16(a)
    b_hi, b_lo = _split_bf16(b)
    dg = functools.partial(lax.dot_general, dimension_numbers=dims, preferred_element_type=F32)
    return dg(a_hi, b_hi) + dg(a_lo, b_hi) + dg(a_hi, b_lo)


def _hy_spec_single_kernel(f_ref, g_ref, o_ref):
    o_ref[...] = _dot3(f_ref[...], g_ref[...])


def _hy_conv_single_kernel(x1_ref, x2_ref, z_ref, h_ref, bias_ref, f_ref, i_ref, o_ref):
    n = f_ref.shape[0] // 2
    z = z_ref[...]
    for order, xg_ref in enumerate((x1_ref, x2_ref)):
        zz = _dot3(f_ref[...], z)
        pr, pi = _cmul(zz[0:n], zz[n:2 * n], h_ref[order, 0:n, :], h_ref[order, n:2 * n, :])
        y = _dot3(i_ref[...], jnp.concatenate([pr, pi], axis=0))
        z = xg_ref[...] * (y + z * bias_ref[order])
    o_ref[...] = z


def hyena_context(x1, x2, z, g, bias, nbatch, seq):
    n = 2 * seq
    f_full, f_half, inv = _dft_consts_single(seq)
    spec = pl.pallas_call(
        _hy_spec_single_kernel,
        grid=(2,),
        in_specs=[pl.BlockSpec((2 * n, n), lambda o: (0, 0)), pl.BlockSpec((None, n, HY_CH), lambda o: (o, 0, 0))],
        out_specs=pl.BlockSpec((None, 2 * n, HY_CH), lambda o: (o, 0, 0)),
        out_shape=jax.ShapeDtypeStruct((2, 2 * n, HY_CH), F32),
        compiler_params=_cparams(("arbitrary",)),
        name="hyena_spec_context",
    )(f_full, g)
    blk = pl.BlockSpec((seq, HY_CH), lambda b: (b, 0))
    return pl.pallas_call(
        _hy_conv_single_kernel,
        grid=(nbatch,),
        in_specs=[blk, blk, blk, pl.BlockSpec((2, 2 * n, HY_CH), lambda b: (0, 0, 0)),
                  pl.BlockSpec((2, 1, HY_CH), lambda b: (0, 0, 0)),
                  pl.BlockSpec((2 * n, seq), lambda b: (0, 0)), pl.BlockSpec((seq, 2 * n), lambda b: (0, 0))],
        out_specs=blk,
        out_shape=jax.ShapeDtypeStruct((nbatch * seq, HY_CH), F32),
        compiler_params=_cparams(("arbitrary",)),
        name="hyena_conv_context",
    )(x1, x2, z, spec, bias.reshape(2, 1, HY_CH), f_half, inv)


def _dft_consts_two_stage():
    n1, n2 = FFT_N1, FFT_N2
    n = n1 * n2
    a2 = np.arange(n2, dtype=np.float64)[:, None, None]
    k1 = np.arange(n1, dtype=np.float64)[None, :, None]
    a1 = np.arange(n1, dtype=np.float64)[None, None, :]
    th = 2.0 * np.pi * (a1 * k1 / n1 + a2 * k1 / n)
    w1 = np.concatenate([np.cos(th), -np.sin(th)], axis=1)
    tht = np.transpose(th, (0, 2, 1))
    w3 = np.concatenate([np.cos(tht), -np.sin(tht)], axis=2) / n
    ph = 2.0 * np.pi * np.arange(n2, dtype=np.float64)[:, None] * np.arange(n2, dtype=np.float64)[None, :] / n2
    c, s = np.cos(ph), np.sin(ph)
    g2 = np.block([[c, s], [-s, c]])
    g2i = np.block([[c, -s], [s, c]])
    return (jnp.asarray(w1, F32), jnp.asarray(w3, F32), jnp.asarray(g2, F32), jnp.asarray(g2i, F32))


FFT_SB = 16
FFT_KB = 8


def _dotc(w, x):
    return jnp.dot(w, x.astype(BF16), preferred_element_type=F32)


def _outer_fwd_kernel(x_ref, w_ref, o_ref):
    for jj in range(FFT_SB):
        y = _dotc(w_ref[jj], x_ref[jj])
        o_ref[jj, 0] = y[0:FFT_N1]
        o_ref[jj, 1] = y[FFT_N1:2 * FFT_N1]


def _outer_fwd(w, xs):
    nbatch, n2, n1, ch = xs.shape
    wspec = pl.BlockSpec((FFT_SB, 2 * FFT_N1, n1), lambda b, j: (j, 0, 0))
    return pl.pallas_call(
        _outer_fwd_kernel,
        grid=(nbatch, n2 // FFT_SB),
        in_specs=[pl.BlockSpec((None, FFT_SB, n1, ch), lambda b, j: (b, j, 0, 0)), wspec],
        out_specs=pl.BlockSpec((None, FFT_SB, 2, FFT_N1, ch), lambda b, j: (b, j, 0, 0, 0)),
        out_shape=jax.ShapeDtypeStruct((nbatch, n2, 2, FFT_N1, ch), F32),
        compiler_params=_cparams(("arbitrary", "arbitrary")),
        name="hyena_dft_outer_fwd",
    )(xs, w.astype(BF16))


def _inner_kernel(conv, *refs):
    if conv:
        a_ref, g_ref, h_ref, i_ref, o_ref = refs
    else:
        a_ref, g_ref, o_ref = refs
    n = FFT_N2
    for kk in range(FFT_KB):
        x = jnp.concatenate([a_ref[:, 0, kk, :], a_ref[:, 1, kk, :]], axis=0)
        y = _dotc(g_ref[...], x)
        if conv:
            pr, pi = _cmul(y[0:n], y[n:2 * n], h_ref[kk, 0:n, :], h_ref[kk, n:2 * n, :])
            q = _dotc(i_ref[...], jnp.concatenate([pr, pi], axis=0))
            o_ref[kk, 0] = q[0:n]
            o_ref[kk, 1] = q[n:2 * n]
        else:
            o_ref[kk] = y


def _inner_stage(a5, g2, spec=None, g2i=None):
    nbatch, n2, _, n1, ch = a5.shape
    conv = spec is not None
    mat = pl.BlockSpec((2 * n2, 2 * n2), lambda b, j: (0, 0))
    in_specs = [pl.BlockSpec((None, n2, 2, FFT_KB, ch), lambda b, j: (b, 0, 0, j, 0)), mat]
    args = [a5, g2.astype(BF16)]
    if conv:
        in_specs += [pl.BlockSpec((FFT_KB, 2 * n2, ch), lambda b, j: (j, 0, 0)), mat]
        args += [spec, g2i.astype(BF16)]
        out_spec = pl.BlockSpec((None, FFT_KB, 2, n2, ch), lambda b, j: (b, j, 0, 0, 0))
        out_shape = jax.ShapeDtypeStruct((nbatch, n1, 2, n2, ch), F32)
    else:
        out_spec = pl.BlockSpec((None, FFT_KB, 2 * n2, ch), lambda b, j: (b, j, 0, 0))
        out_shape = jax.ShapeDtypeStruct((nbatch, n1, 2 * n2, ch), F32)
    return pl.pallas_call(
        functools.partial(_inner_kernel, conv),
        grid=(nbatch, n1 // FFT_KB),
        in_specs=in_specs, out_specs=out_spec, out_shape=out_shape,
        compiler_params=_cparams(("arbitrary", "arbitrary")),
        name="hyena_dft_inner_conv" if conv else "hyena_dft_inner_spec",
    )(*args)


def _outer_inv_kernel(to_time_major, q_ref, w_ref, *refs):
    if to_time_major:
        perm_ref, xg_ref, z_ref, b_ref, o_ref = refs
    else:
        xg_ref, z_ref, b_ref, o_ref = refs
    outs = []
    for jj in range(FFT_SB):
        qm = jnp.concatenate([q_ref[:, 0, jj, :], q_ref[:, 1, jj, :]], axis=0)
        g = xg_ref[jj] * (_dotc(w_ref[jj], qm) + z_ref[jj] * b_ref[...])
        if to_time_major:
            outs.append(g)
        else:
            o_ref[jj] = g
    if to_time_major:
        y = jnp.concatenate(outs, axis=0)
        h1 = y.astype(BF16)
        r1 = y - h1.astype(F32)
        h2 = r1.astype(BF16)
        h3 = (r1 - h2.astype(F32)).astype(BF16)
        p = perm_ref[...]
        yp = (jnp.dot(p, h1, preferred_element_type=F32) + jnp.dot(p, h2, preferred_element_type=F32)
              + jnp.dot(p, h3, preferred_element_type=F32))
        o_ref[...] = yp.reshape(o_ref.shape)


def _outer_inv_gate(w, q5, xg, z, bias_row, to_time_major):
    nbatch, n2, n1h, ch = z.shape
    slab = pl.BlockSpec((None, FFT_SB, n1h, ch), lambda b, j: (b, j, 0, 0))
    wspec = pl.BlockSpec((FFT_SB, n1h, 2 * FFT_N1), lambda b, j: (j, 0, 0))
    in_specs = [pl.BlockSpec((None, FFT_N1, 2, FFT_SB, ch), lambda b, j: (b, 0, 0, j, 0)), wspec]
    args = [q5, w.astype(BF16)]
    if to_time_major:
        rows = FFT_SB * n1h
        r = np.arange(rows)
        perm = np.zeros((rows, rows), np.float32)
        perm[r, (r % FFT_SB) * n1h + r // FFT_SB] = 1.0
        in_specs.append(pl.BlockSpec((rows, rows), lambda b, j: (0, 0)))
        args.append(jnp.asarray(perm, BF16))
        out_spec = pl.BlockSpec((None, n1h, FFT_SB, ch), lambda b, j: (b, 0, j, 0))
        out_shape = jax.ShapeDtypeStruct((nbatch, n1h, n2, ch), F32)
    else:
        out_spec, out_shape = slab, jax.ShapeDtypeStruct(z.shape, F32)
    in_specs += [slab, slab, pl.BlockSpec((1, ch), lambda b, j: (0, 0))]
    args += [xg, z, bias_row]
    return pl.pallas_call(
        functools.partial(_outer_inv_kernel, to_time_major),
        grid=(nbatch, n2 // FFT_SB),
        in_specs=in_specs, out_specs=out_spec, out_shape=out_shape,
        compiler_params=_cparams(("arbitrary", "arbitrary")),
        name="hyena_dft_outer_inv_gate",
    )(*args)


def hyena_latent(x1, x2, z, g, bias, nbatch, seq):
    n1, n2 = FFT_N1, FFT_N2
    assert 2 * seq == n1 * n2
    w1, w3, g2, g2i = _dft_consts_two_stage()
    half = seq // n2
    spec = _inner_stage(_outer_fwd(w1, g), g2)
    w1h = w1[:, :, 0:half]
    w3h = w3[:, 0:half, :]
    q5 = _inner_stage(_outer_fwd(w1h, z), g2, spec[0], g2i)
    z1 = _outer_inv_gate(w3h, q5, x1, z, bias[0].reshape(1, HY_CH), False)
    q5 = _inner_stage(_outer_fwd(w1h, z1), g2, spec[1], g2i)
    out = _outer_inv_gate(w3h, q5, x2, z1, bias[1].reshape(1, HY_CH), True)
    return out.reshape(nbatch * seq, HY_CH)


def _permute_w_in(w):
    sizes = (128, 128, 256, 256, 16, 256, 256, 256, 256, 16, 256, 256, 256, 768)
    offs = np.cumsum((0,) + sizes)
    seg = lambda j: w[:, offs[j]:offs[j + 1]]
    order = (13, 0, 1, 2, 3, 5, 6, 7, 8, 10, 11, 12, 4, 9)
    pad = jnp.zeros((w.shape[0], P_W - P_AUX - 32), w.dtype)
    return jnp.concatenate([seg(j) for j in order] + [pad], axis=1)


def kernel(x_prompt, x_sample, cache_na_k, cache_na_v, state_gla, state_mlstm_C, state_mlstm_n, state_mlstm_m, c, c_ctx, w_ada, b_ada, norm1_g, norm2_g, w_in, w_out, gla_w_gate, gla_b_gate, gla_norm_g, ml_b_gate, ml_norm_g, na_rpb, hy_conv_w, hy_conv_b, hy_w1, hy_b1, hy_w2, hy_b2, hy_w3, hy_b3, hy_freq, hy_bias, router_w, router_b, w_gu, b_gu, w_dn, b_dn, final_norm_g):
    depth = w_ada.shape[0]
    bp, lp, _ = x_prompt.shape
    bs, ls, _ = x_sample.shape
    tp = bp * lp
    assert tp == ls, "modulation rows are selected per block of DEC_SEQ tokens"
    x = jnp.concatenate([x_prompt.reshape(tp, D), x_sample.reshape(bs * ls, D)], axis=0)
    cond8 = jnp.concatenate([c_ctx[None, :], c, jnp.zeros((8 - 1 - bs, D), F32)], axis=0)
    mods_all = ada_mods(cond8, w_ada, b_ada)
    fg = final_norm_g.reshape(1, D)
    eye_h = jnp.eye(N_HEADS, dtype=F32)
    lc = cache_na_k.shape[3]
    new_k, new_v, new_gla, new_c, new_n, new_m = [], [], [], [], [], []
    for l in range(depth):
        mods = mods_all[l].reshape(8, 1, 6 * D)
        proj = in_proj(x, mods, norm1_g[l].reshape(1, D), _permute_w_in(w_in[l]).astype(BF16), ls)
        gla_c, gla_fin = gla_mixer(proj, 0, bp, lp, gla_w_gate[l], gla_b_gate[l], gla_norm_g[l], None)
        s0t = jnp.einsum('bdhkv,hg->bdhvgk', state_gla[:, l], eye_h).reshape(bs, 2, 256, 128)
        gla_s = gla_mixer(proj, tp, bs, ls, gla_w_gate[l], gla_b_gate[l], gla_norm_g[l], s0t)
        ml_c, cf, nf, mf = mlstm_mixer(proj, 0, bp, lp, ml_b_gate[l], ml_norm_g[l], None)
        c0 = jnp.einsum('bdhvk,hg->bdhkgv', state_mlstm_C[:, l], eye_h).reshape(bs, 2, 256, 256)
        n0 = state_mlstm_n[:, l].reshape(bs, 2, 1, 256)
        m0 = jnp.repeat(state_mlstm_m[:, l], ML_DH, axis=-1).reshape(bs, 2, 1, 256)
        ml_s = mlstm_mixer(proj, tp, bs, ls, ml_b_gate[l], ml_norm_g[l], (c0, n0, m0))
        na_c, k_l, v_l = na_context(proj, bp, lp)
        kct = cache_na_k[:, l].transpose(0, 2, 1, 3).reshape(bs, lc, 256)
        vct = cache_na_v[:, l].transpose(0, 2, 1, 3).reshape(bs, lc, 256)
        na_s = na_latent(proj, tp, bs, ls, kct, vct, na_rpb[l])
        hy_args = (hy_w1[l], hy_b1[l], hy_w2[l], hy_b2[l], hy_w3[l], hy_b3[l], hy_freq[l])
        hy_c = hyena_context(*hyena_short_conv(proj, 0, bp, lp, hy_conv_w[l], hy_conv_b[l]),
                             hyena_filters(lp, *hy_args), hy_bias[l], bp, lp)
        hy_s = hyena_latent(*hyena_short_conv(proj, tp, bs, ls, hy_conv_w[l], hy_conv_b[l], FFT_N2),
                            hyena_filters(ls, *hy_args, FFT_N2), hy_bias[l], bs, ls)
        x1, h2, eidx, ew, cnt_b = out_proj_route(
            (gla_c, ml_c, na_c, hy_c), (gla_s, ml_s, na_s, hy_s), x, mods, norm2_g[l].reshape(1, D), w_out[l].astype(BF16), router_w[l].T,
            jnp.broadcast_to(router_b[l][:, None], (N_EXP, LANES)), ls)
        x = moe_layer(h2, eidx, ew, cnt_b, x1, mods, fg, l, w_gu, b_gu, w_dn, b_dn, ls, l == depth - 1)
        new_k.append(k_l)
        new_v.append(v_l)
        gla_blocks = [gla_fin[:, :, h * GLA_DV:(h + 1) * GLA_DV, h * GLA_DK:(h + 1) * GLA_DK] for h in range(N_HEADS)]
        new_gla.append(jnp.stack(gla_blocks, 2).transpose(0, 1, 2, 4, 3))
        c_blocks = [cf[:, :, h * ML_DH:(h + 1) * ML_DH, h * ML_DH:(h + 1) * ML_DH] for h in range(N_HEADS)]
        new_c.append(jnp.stack(c_blocks, 2).transpose(0, 1, 2, 4, 3))
        new_n.append(nf.reshape(bp, 2, N_HEADS, ML_DH))
        new_m.append(mf.reshape(bp, 2, N_HEADS, ML_DH)[..., 0])
    y_prompt = x[:tp].reshape(bp, lp, D)
    y_sample = x[tp:].reshape(bs, ls, D)
    return (y_prompt, y_sample, jnp.stack(new_k, 1), jnp.stack(new_v, 1), jnp.stack(new_gla, 1),
            jnp.stack(new_c, 1), jnp.stack(new_n, 1), jnp.stack(new_m, 1))
```

```python
import functools
import math

import numpy as np
import jax
import jax.numpy as jnp
from jax import lax
from jax.experimental import pallas as pl
from jax.experimental.pallas import tpu as pltpu

F32 = jnp.float32
BF16 = jnp.bfloat16
HI = lax.Precision.HIGHEST

LANES = 128
SUBLANES = 8
VMEM_LIMIT = 56 * 1024 * 1024

D = 1024
EPS = 1e-6
N_EXP = 32
TOP_K = 4
D_FF = 1024
SWIGLU_LIMIT = 7.0
SWIGLU_ALPHA = 1.702
GRID_W = 64

TB = 256
ROWS_BS = TB * TOP_K + N_EXP * SUBLANES
CH = SUBLANES
NCH = ROWS_BS // CH
TM = 512

P_HU = 0
P_GQ, P_GK, P_GV, P_GG = 768, 896, 1024, 1280
P_MQ, P_MK, P_MV, P_MO = 1536, 1792, 2048, 2304
P_NQ, P_NK, P_NV = 2560, 2816, 3072
P_AUX = 3328
P_W = 3456


def _cparams(sem=None):
    return pltpu.CompilerParams(dimension_semantics=sem, vmem_limit_bytes=VMEM_LIMIT)


def _ada_kernel(c_ref, w_ref, b_ref, o_ref):
    c = c_ref[...]
    s = c * jax.nn.sigmoid(c)
    o_ref[...] = jnp.dot(s, w_ref[...], precision=HI, preferred_element_type=F32) + b_ref[...]


def ada_mods(cond8, w_ada, b_ada):
    depth = w_ada.shape[0]
    tn = 1536
    return pl.pallas_call(
        _ada_kernel,
        grid=(depth, 6 * D // tn),
        in_specs=[pl.BlockSpec((8, D), lambda l, j: (0, 0)),
                  pl.BlockSpec((None, D, tn), lambda l, j: (l, 0, j)),
                  pl.BlockSpec((None, 1, tn), lambda l, j: (l, 0, j))],
        out_specs=pl.BlockSpec((None, 8, tn), lambda l, j: (l, 0, j)),
        out_shape=jax.ShapeDtypeStruct((depth, 8, 6 * D), F32),
        compiler_params=_cparams(("arbitrary", "arbitrary")),
        name="ada_mods",
    )(cond8, w_ada, b_ada.reshape(depth, 1, 6 * D))


def _rms_mod(x, g, sc, sh):
    ms = jnp.mean(x * x, axis=-1, keepdims=True)
    return (x * lax.rsqrt(ms + EPS) * g) * (1.0 + sc) + sh


def _in_kernel(x_ref, mod_ref, g_ref, w_ref, o_ref):
    h = _rms_mod(x_ref[...], g_ref[...], mod_ref[:, D:2 * D], mod_ref[:, 0:D])
    o_ref[...] = jnp.dot(h.astype(BF16), w_ref[...], preferred_element_type=F32)


def in_proj(x, mods, g, w_bf16, tok_per_mod):
    t = x.shape[0]
    tm = 512
    return pl.pallas_call(
        _in_kernel,
        grid=(t // tm,),
        in_specs=[pl.BlockSpec((tm, D), lambda i: (i, 0)),
                  pl.BlockSpec((None, 1, 6 * D), lambda i: (i * tm // tok_per_mod, 0, 0)),
                  pl.BlockSpec((1, D), lambda i: (0, 0)),
                  pl.BlockSpec((D, P_W), lambda i: (0, 0))],
        out_specs=pl.BlockSpec((tm, P_W), lambda i: (i, 0)),
        out_shape=jax.ShapeDtypeStruct((t, P_W), F32),
        compiler_params=_cparams(("arbitrary",)),
        name="in_proj",
    )(x, mods, g, w_bf16)


def _out_kernel(nb_ctx, *refs):
    yc_refs, ys_refs = refs[0:4], refs[4:8]
    x_ref, mod_ref, g_ref, w_ref, rw_ref, rb_ref, x1_ref, h2_ref, eidx_ref, ew_ref, cnt_ref = refs[8:]
    is_ctx = pl.program_id(0) < nb_ctx
    y = jnp.concatenate([jnp.where(is_ctx, c[...], s[...]) for c, s in zip(yc_refs, ys_refs)], axis=1).astype(BF16)
    mix = jnp.dot(y, w_ref[...], preferred_element_type=F32)
    x1 = x_ref[...] + mod_ref[:, 2 * D:3 * D] * mix
    x1_ref[...] = x1
    h2 = _rms_mod(x1, g_ref[...], mod_ref[:, 4 * D:5 * D], mod_ref[:, 3 * D:4 * D])
    h2_ref[...] = h2.astype(BF16)
    lg = lax.dot_general(rw_ref[...], h2, (((1,), (1,)), ((), ())), precision=HI,
                         preferred_element_type=F32) + rb_ref[:, 0:1]
    e_iota = lax.broadcasted_iota(jnp.int32, lg.shape, 0)
    vals, idxs = [], []
    for _ in range(TOP_K):
        m = jnp.max(lg, axis=0, keepdims=True)
        idx = jnp.min(jnp.where(lg == m, e_iota, N_EXP), axis=0, keepdims=True)
        vals.append(m)
        idxs.append(idx)
        lg = jnp.where(e_iota == idx, -jnp.inf, lg)
    ex = [jnp.exp(v - vals[0]) for v in vals]
    den = ex[0] + ex[1] + ex[2] + ex[3]
    eidx_ref[...] = jnp.concatenate(idxs, axis=0)
    ew_ref[...] = jnp.concatenate([e / den for e in ex], axis=0)
    ind = jnp.zeros(lg.shape, F32)
    for idx in idxs:
        ind += (e_iota == idx).astype(F32)
    cnt_ref[...] = jnp.broadcast_to(jnp.sum(ind, axis=1, keepdims=True), (N_EXP, LANES))


def out_proj_route(ys_ctx, ys_lat, x, mods, g, w_bf16, rw_t, rb, tok_per_mod):
    t = x.shape[0]
    nb = t // TB
    nb_ctx = ys_ctx[0].shape[0] // TB
    cspec = pl.BlockSpec((TB, 256), lambda i: (jnp.minimum(i, nb_ctx - 1), 0))
    lspec = pl.BlockSpec((TB, 256), lambda i: (jnp.maximum(i - nb_ctx, 0), 0))
    return pl.pallas_call(
        functools.partial(_out_kernel, nb_ctx),
        grid=(nb,),
        in_specs=[cspec] * 4 + [lspec] * 4 + [
                  pl.BlockSpec((TB, D), lambda i: (i, 0)),
                  pl.BlockSpec((None, 1, 6 * D), lambda i: (i * TB // tok_per_mod, 0, 0)),
                  pl.BlockSpec((1, D), lambda i: (0, 0)),
                  pl.BlockSpec((D, D), lambda i: (0, 0)),
                  pl.BlockSpec((N_EXP, D), lambda i: (0, 0)),
                  pl.BlockSpec((N_EXP, LANES), lambda i: (0, 0))],
        out_specs=[pl.BlockSpec((TB, D), lambda i: (i, 0)),
                   pl.BlockSpec((TB, D), lambda i: (i, 0)),
                   pl.BlockSpec((None, TOP_K, TB), lambda i: (i, 0, 0)),
                   pl.BlockSpec((None, TOP_K, TB), lambda i: (i, 0, 0)),
                   pl.BlockSpec((None, N_EXP, LANES), lambda i: (i, 0, 0))],
        out_shape=[jax.ShapeDtypeStruct((t, D), F32),
                   jax.ShapeDtypeStruct((t, D), BF16),
                   jax.ShapeDtypeStruct((nb, TOP_K, TB), jnp.int32),
                   jax.ShapeDtypeStruct((nb, TOP_K, TB), F32),
                   jax.ShapeDtypeStruct((nb, N_EXP, LANES), F32)],
        compiler_params=_cparams(("arbitrary",)),
        name="out_proj_route",
    )(*ys_ctx, *ys_lat, x, mods, g, w_bf16, rw_t, rb)


def moe_tables(cnt, n_tiles):
    nb = cnt.shape[0]
    cnt8 = (cnt + CH - 1) // CH * CH
    ends = jnp.cumsum(cnt8, axis=1)
    off = ends - cnt8
    nchunks = ends[:, -1] // CH
    tot = jnp.sum(cnt8, axis=0)
    totp = (tot + TM - 1) // TM * TM
    eend = jnp.cumsum(totp)
    estart = eend - totp
    gdst = estart[None, :] + jnp.cumsum(cnt8, axis=0) - cnt8
    r = jnp.arange(NCH, dtype=jnp.int32) * CH
    e_of_c = jnp.minimum(jnp.sum((ends[:, None, :] <= r[None, :, None]).astype(jnp.int32), axis=-1), N_EXP - 1)
    pick = e_of_c[:, :, None] == jnp.arange(N_EXP, dtype=jnp.int32)[None, None, :]
    g_of_c = jnp.sum(jnp.where(pick, gdst[:, None, :], 0), axis=-1)
    o_of_c = jnp.sum(jnp.where(pick, off[:, None, :], 0), axis=-1)
    gchunk = (g_of_c + r[None, :] - o_of_c) // CH
    nused = eend[-1] // TM
    ti = jnp.arange(n_tiles, dtype=jnp.int32)
    tile_e = jnp.sum((eend[None, :] // TM <= jnp.minimum(ti, nused - 1)[:, None]).astype(jnp.int32), axis=-1)
    tile_e = jnp.minimum(tile_e, N_EXP - 1)
    has = totp > 0
    ei = jnp.arange(N_EXP, dtype=jnp.int32)
    later = has[None, :] & (ei[None, :] > ei[:, None])
    next_of = jnp.min(jnp.where(later, ei[None, :], N_EXP), axis=1)
    next_of = jnp.where(next_of == N_EXP, -1, next_of)
    ordinal = jnp.cumsum(has.astype(jnp.int32)) - 1
    tile_next = jnp.sum(jnp.where(tile_e[:, None] == ei[None, :], next_of[None, :], 0), axis=1)
    tile_slot = jnp.sum(jnp.where(tile_e[:, None] == ei[None, :], ordinal[None, :], 0), axis=1) % 2
    i32 = lambda a: a.astype(jnp.int32)
    return dict(off=off, gchunk=i32(gchunk.reshape(-1)), nchunks=i32(nchunks), tile_e=i32(tile_e),
                tile_next=i32(tile_next), tile_slot=i32(tile_slot), nused=i32(nused.reshape(1)),
                pad_lo=i32((estart + tot) // CH), pad_hi=i32(eend // CH))


def _dest_rows(eidx, off_col):
    e_iota = lax.broadcasted_iota(jnp.int32, (N_EXP, TB), 0)
    ohs = [e_iota == eidx[k:k + 1, :] for k in range(TOP_K)]
    ind = jnp.zeros((N_EXP, TB), F32)
    for oh in ohs:
        ind += oh.astype(F32)
    ti = lax.broadcasted_iota(jnp.int32, (TB, TB), 0)
    tj = lax.broadcasted_iota(jnp.int32, (TB, TB), 1)
    upper = (ti <= tj).astype(BF16)
    rank_incl = jnp.dot(ind.astype(BF16), upper, preferred_element_type=F32)
    base = off_col + rank_incl - ind
    return [jnp.sum(jnp.where(oh, base, 0.0), axis=0, keepdims=True).astype(jnp.int32) for oh in ohs]


def _dispatch_kernel(gchunk_ref, nch_ref, plo_ref, phi_ref, nused_ref, h2_ref, eidx_ref, off_ref, xs_ref,
                     xbs_ref, zbuf_ref, sem, zsem):
    blk = pl.program_id(0)
    last = pl.num_programs(0) - 1
    n_tiles = xs_ref.shape[0] // TM

    def zero_fill(start):
        def pad_chunk(c, carry):
            cp = pltpu.make_async_copy(zbuf_ref.at[pl.ds(0, CH), :],
                                       xs_ref.at[pl.ds(pl.multiple_of(c * CH, CH), CH), :], zsem)
            cp.start() if start else cp.wait()
            return carry

        def per_expert(e, carry):
            lax.fori_loop(plo_ref[e], phi_ref[e], pad_chunk, 0)
            return carry

        def tail_tile(t, carry):
            cp = pltpu.make_async_copy(zbuf_ref, xs_ref.at[pl.ds(pl.multiple_of(t * TM, TM), TM), :], zsem)
            cp.start() if start else cp.wait()
            return carry

        lax.fori_loop(0, N_EXP, per_expert, 0)
        lax.fori_loop(nused_ref[0], n_tiles, tail_tile, 0)

    @pl.when(blk == 0)
    def _():
        zbuf_ref[...] = jnp.zeros(zbuf_ref.shape, F32)
        zero_fill(True)

    dests = _dest_rows(eidx_ref[...], off_ref[:, 0:1])
    p_iota = lax.broadcasted_iota(jnp.int32, (ROWS_BS, TB), 0)
    perm = jnp.zeros((ROWS_BS, TB), F32)
    for d in dests:
        perm = jnp.where(p_iota == d, 1.0, perm)
    slot = blk % 2
    xbs_ref[slot] = jnp.dot(perm.astype(BF16), h2_ref[...], preferred_element_type=F32)

    def copy(b, c):
        dst = pl.multiple_of(gchunk_ref[b * NCH + c] * CH, CH)
        src = pl.multiple_of(c * CH, CH)
        return pltpu.make_async_copy(xbs_ref.at[b % 2, pl.ds(src, CH), :], xs_ref.at[pl.ds(dst, CH), :],
                                     sem.at[b % 2])

    def start(c, carry):
        copy(blk, c).start()
        return carry

    lax.fori_loop(0, nch_ref[blk], start, 0)

    def wait_block(b):
        rows = nch_ref[b] * CH
        pltpu.make_async_copy(xbs_ref.at[b % 2, pl.ds(0, rows), :], xs_ref.at[pl.ds(0, rows), :],
                              sem.at[b % 2]).wait()

    @pl.when(blk > 0)
    def _():
        wait_block(blk - 1)

    @pl.when(blk == last)
    def _():
        wait_block(blk)
        zero_fill(False)


def moe_dispatch(h2, eidx, off_b, tb, rows_alloc):
    nb = h2.shape[0] // TB
    return pl.pallas_call(
        _dispatch_kernel,
        grid_spec=pltpu.PrefetchScalarGridSpec(
            num_scalar_prefetch=5,
            grid=(nb,),
            in_specs=[pl.BlockSpec((TB, D), lambda i, *_: (i, 0)),
                      pl.BlockSpec((None, TOP_K, TB), lambda i, *_: (i, 0, 0)),
                      pl.BlockSpec((None, N_EXP, LANES), lambda i, *_: (i, 0, 0))],
            out_specs=pl.BlockSpec(memory_space=pl.ANY),
            scratch_shapes=[pltpu.VMEM((2, ROWS_BS, D), F32), pltpu.VMEM((TM, D), F32),
                            pltpu.SemaphoreType.DMA((2,)), pltpu.SemaphoreType.DMA(())]),
        out_shape=jax.ShapeDtypeStruct((rows_alloc, D), F32),
        compiler_params=_cparams(("arbitrary",)),
        name="moe_dispatch",
    )(tb["gchunk"], tb["nchunks"], tb["pad_lo"], tb["pad_hi"], tb["nused"], h2, eidx, off_b)


def _expert_kernel(layer, te_ref, tnext_ref, tslot_ref, nused_ref, x_ref, bgu_ref, bdn_ref, wgu_hbm, wdn_hbm,
                   y_ref, wgu_f32, wdn_f32, wgu_bf, wdn_bf, sem):
    i = pl.program_id(0)

    def fetch(e, slot, start):
        for k, (src, dst) in enumerate(((wgu_hbm, wgu_f32), (wdn_hbm, wdn_f32))):
            cp = pltpu.make_async_copy(src.at[layer, e], dst.at[slot], sem.at[slot, k])
            cp.start() if start else cp.wait()

    @pl.when(i >= nused_ref[0])
    def _():
        y_ref[...] = jnp.zeros(y_ref.shape, F32)

    @pl.when(i < nused_ref[0])
    def _():
        e = te_ref[i]
        slot = tslot_ref[i]
        first = jnp.logical_or(i == 0, e != te_ref[jnp.maximum(i - 1, 0)])

        @pl.when(i == 0)
        def _():
            fetch(e, slot, True)

        @pl.when(first)
        def _():
            fetch(e, slot, False)
            nxt = tnext_ref[i]

            @pl.when(nxt >= 0)
            def _():
                fetch(nxt, 1 - slot, True)

            wgu_bf[...] = wgu_f32[slot].astype(BF16)
            wdn_bf[...] = wdn_f32[slot].astype(BF16)

        gu = jnp.dot(x_ref[...].astype(BF16), wgu_bf[...], preferred_element_type=F32) + bgu_ref[...]
        g = jnp.minimum(gu[:, 0:D_FF], SWIGLU_LIMIT)
        u = jnp.clip(gu[:, D_FF:2 * D_FF], -SWIGLU_LIMIT, SWIGLU_LIMIT)
        act = g * jax.nn.sigmoid(SWIGLU_ALPHA * g) * (u + 1.0)
        y_ref[...] = jnp.dot(act.astype(BF16), wdn_bf[...], preferred_element_type=F32) + bdn_ref[...]


def moe_experts(xs, tb, layer, w_gu, b_gu, w_dn, b_dn):
    n_tiles = xs.shape[0] // TM
    depth = w_gu.shape[0]

    def xmap(i, te, tn, ts, nu):
        return (jnp.minimum(i, jnp.maximum(nu[0] - 1, 0)), 0)

    def bmap(i, te, tn, ts, nu):
        return (layer, te[i], 0, 0)

    return pl.pallas_call(
        functools.partial(_expert_kernel, layer),
        grid_spec=pltpu.PrefetchScalarGridSpec(
            num_scalar_prefetch=4,
            grid=(n_tiles,),
            in_specs=[pl.BlockSpec((TM, D), xmap),
                      pl.BlockSpec((None, None, 1, 2 * D_FF), bmap),
                      pl.BlockSpec((None, None, 1, D), bmap),
                      pl.BlockSpec(memory_space=pl.ANY),
                      pl.BlockSpec(memory_space=pl.ANY)],
            out_specs=pl.BlockSpec((TM, D), lambda i, *_: (i, 0)),
            scratch_shapes=[pltpu.VMEM((2, D, 2 * D_FF), F32), pltpu.VMEM((2, D_FF, D), F32),
                            pltpu.VMEM((D, 2 * D_FF), BF16), pltpu.VMEM((D_FF, D), BF16),
                            pltpu.SemaphoreType.DMA((2, 2))]),
        out_shape=jax.ShapeDtypeStruct((xs.shape[0], D), F32),
        compiler_params=_cparams(("arbitrary",)),
        name="moe_experts",
    )(tb["tile_e"], tb["tile_next"], tb["tile_slot"], tb["nused"], xs,
      b_gu.reshape(depth, N_EXP, 1, 2 * D_FF), b_dn.reshape(depth, N_EXP, 1, D), w_gu, w_dn)


def _combine_kernel(final, gchunk_ref, nch_ref, eidx_ref, ew_ref, off_ref, x1_ref, mod_ref, fg_ref, ys_ref,
                    o_ref, ybs_ref, sem):
    blk = pl.program_id(0)
    n = nch_ref[blk]
    slot = blk % 2

    def copy(b, c):
        src = pl.multiple_of(gchunk_ref[b * NCH + c] * CH, CH)
        dst = pl.multiple_of(c * CH, CH)
        return pltpu.make_async_copy(ys_ref.at[pl.ds(src, CH), :], ybs_ref.at[b % 2, pl.ds(dst, CH), :],
                                     sem.at[b % 2])

    def fetch_block(b):
        def start(c, carry):
            copy(b, c).start()
            return carry
        lax.fori_loop(0, nch_ref[b], start, 0)

    @pl.when(blk == 0)
    def _():
        fetch_block(blk)

    @pl.when(blk + 1 < pl.num_programs(0))
    def _():
        fetch_block(blk + 1)

    dests = _dest_rows(eidx_ref[...], off_ref[:, 0:1])
    ew = ew_ref[...]
    p_iota = lax.broadcasted_iota(jnp.int32, (ROWS_BS, TB), 0)
    gsel = jnp.zeros((ROWS_BS, TB), F32)
    for k, d in enumerate(dests):
        gsel = jnp.where(p_iota == d, ew[k:k + 1, :], gsel)
    perm = jnp.where(gsel > 0.0, 1.0, 0.0)
    gate_col = jnp.sum(gsel, axis=1, keepdims=True)
    pltpu.make_async_copy(ys_ref.at[pl.ds(0, n * CH), :], ybs_ref.at[slot, pl.ds(0, n * CH), :], sem.at[slot]).wait()
    row_iota = lax.broadcasted_iota(jnp.int32, (ROWS_BS, 1), 0)
    yb = jnp.where(row_iota < n * CH, ybs_ref[slot], 0.0) * gate_col
    moe = lax.dot_general(perm.astype(BF16), yb.astype(BF16), (((0,), (0,)), ((), ())),
                          preferred_element_type=F32)
    x2 = x1_ref[...] + mod_ref[:, 5 * D:6 * D] * moe
    if final:
        ms = jnp.mean(x2 * x2, axis=-1, keepdims=True)
        x2 = x2 * lax.rsqrt(ms + EPS) * fg_ref[...]
    o_ref[...] = x2


def moe_combine(ys, eidx, ew, off_b, x1, mods, fg, gchunk, nchunks, tok_per_mod, final):
    t = x1.shape[0]
    nb = t // TB
    return pl.pallas_call(
        functools.partial(_combine_kernel, final),
        grid_spec=pltpu.PrefetchScalarGridSpec(
            num_scalar_prefetch=2,
            grid=(nb,),
            in_specs=[pl.BlockSpec((None, TOP_K, TB), lambda i, *_: (i, 0, 0)),
                      pl.BlockSpec((None, TOP_K, TB), lambda i, *_: (i, 0, 0)),
                      pl.BlockSpec((None, N_EXP, LANES), lambda i, *_: (i, 0, 0)),
                      pl.BlockSpec((TB, D), lambda i, *_: (i, 0)),
                      pl.BlockSpec((None, 1, 6 * D), lambda i, *_: (i * TB // tok_per_mod, 0, 0)),
                      pl.BlockSpec((1, D), lambda i, *_: (0, 0)),
                      pl.BlockSpec(memory_space=pl.ANY)],
            out_specs=pl.BlockSpec((TB, D), lambda i, *_: (i, 0)),
            scratch_shapes=[pltpu.VMEM((2, ROWS_BS, D), F32), pltpu.SemaphoreType.DMA((2,))]),
        out_shape=jax.ShapeDtypeStruct((t, D), F32),
        compiler_params=_cparams(("arbitrary",)),
        name="moe_combine",
    )(gchunk, nchunks, eidx, ew, off_b, x1, mods, fg, ys)


def moe_layer(h2, eidx, ew, cnt_b, x1, mods, fg, layer, w_gu, b_gu, w_dn, b_dn, tok_per_mod, final):
    t = h2.shape[0]
    nb = t // TB
    max_rows = t * TOP_K + nb * N_EXP * (CH - 1) + N_EXP * (TM - CH)
    n_tiles = (max_rows + TM - 1) // TM
    cnt = cnt_b[:, :, 0].astype(jnp.int32)
    tb = moe_tables(cnt, n_tiles)
    off_b = jnp.broadcast_to(tb["off"].astype(F32)[:, :, None], (nb, N_EXP, LANES))
    xs = moe_dispatch(h2, eidx, off_b, tb, n_tiles * TM)
    ys = moe_experts(xs, tb, layer, w_gu, b_gu, w_dn, b_dn)
    return moe_combine(ys, eidx, ew, off_b, x1, mods, fg, tb["gchunk"], tb["nchunks"], tok_per_mod, final)


N_HEADS = 4


def _stack_heads(x, head_w):
    lane_h = lax.broadcasted_iota(jnp.int32, x.shape, 1) // head_w
    return jnp.concatenate([jnp.where(lane_h == h, x, 0.0) for h in range(N_HEADS)], axis=0)


def _unstack_heads(xs, head_w):
    r = xs.shape[0] // N_HEADS
    lane_h = lax.broadcasted_iota(jnp.int32, (r, xs.shape[1]), 1) // head_w
    out = jnp.zeros((r, xs.shape[1]), F32)
    for h in range(N_HEADS):
        out = jnp.where(lane_h == h, xs[h * r:(h + 1) * r, :], out)
    return out


def _block_diag_mask(rows, cols, rw, cw):
    ri = lax.broadcasted_iota(jnp.int32, (rows, cols), 0) // rw
    ci = lax.broadcasted_iota(jnp.int32, (rows, cols), 1) // cw
    return ri == ci


def _head_rmsnorm(o, head_w):
    n = o.shape[1]
    bd = _block_diag_mask(n, n, head_w, head_w).astype(F32)
    ms = jnp.dot(o * o, bd, precision=HI, preferred_element_type=F32) * (1.0 / head_w)
    return o * lax.rsqrt(ms + EPS)


def _nt(a, b, **kw):
    return lax.dot_general(a, b, (((1,), (1,)), ((), ())), preferred_element_type=F32, **kw)


def _tn(a, b, **kw):
    return lax.dot_general(a, b, (((0,), (0,)), ((), ())), preferred_element_type=F32, **kw)


NA_DH = 64
NA_WIN_R = 8
NA_WIN_C = 16
NA_ROWS = 64


NA_RPS = 8


def _na_window_start(r):
    return jnp.clip(r - NA_WIN_R // 2, 0, NA_ROWS - NA_WIN_R)


def _na_lat_kernel(q_ref, k_ref, v_ref, kc_ref, vc_ref, *refs):
    bias_refs, o_ref = refs[:NA_RPS], refs[NA_RPS]
    kc = kc_ref[...].astype(BF16)
    vc = vc_ref[...].astype(BF16)
    for t in range(NA_RPS):
        start = _na_window_start(pl.program_id(1) * NA_RPS + t)
        rows = pl.ds(pl.multiple_of(start * GRID_W, GRID_W), NA_WIN_R * GRID_W)
        tok = slice(t * GRID_W, (t + 1) * GRID_W)
        qs = _stack_heads(q_ref[tok, :] * (NA_DH ** -0.5), NA_DH).astype(BF16)
        k_all = jnp.concatenate([k_ref[rows, :].astype(BF16), kc], axis=0)
        v_all = jnp.concatenate([v_ref[rows, :].astype(BF16), vc], axis=0)
        s = _nt(qs, k_all) + bias_refs[t][...]
        m = jnp.max(s, axis=1, keepdims=True)
        p = jnp.exp(s - m)
        den = jnp.sum(p, axis=1, keepdims=True)
        o = jnp.dot(p.astype(BF16), v_all, preferred_element_type=F32)
        o_ref[tok, :] = _unstack_heads(o / den, NA_DH)


def _na_bias_table(rpb):
    col = np.arange(GRID_W)
    c_start = np.clip(col - NA_WIN_C // 2, 0, GRID_W - NA_WIN_C)
    col_mask = (col[None, :] >= c_start[:, None]) & (col[None, :] < c_start[:, None] + NA_WIN_C)
    c_idx = np.clip(col[None, :] - col[:, None], -(NA_WIN_C - 1), NA_WIN_C - 1) + (NA_WIN_C - 1)
    onehot = jnp.asarray(c_idx[None, :, :] == np.arange(2 * NA_WIN_C - 1)[:, None, None], F32)
    tb = jnp.einsum('hrc,cqk->hrqk', rpb, onehot, precision=HI)
    tb = jnp.where(col_mask[None, None], tb, -jnp.inf)
    out = []
    for ri0 in range(NA_WIN_R):
        blk = tb[:, ri0:ri0 + NA_WIN_R]
        out.append(blk.transpose(0, 2, 1, 3).reshape(N_HEADS * GRID_W, NA_WIN_R * GRID_W))
    return jnp.stack(out, 0)


def na_latent(proj, tok0, bs, ls, kc, vc, rpb):
    lc = kc.shape[1]
    bias = _na_bias_table(rpb)
    bias = jnp.concatenate([bias, jnp.zeros(bias.shape[:2] + (lc,), F32)], axis=-1)
    nstep = ls // GRID_W // NA_RPS
    tq = NA_RPS * GRID_W
    rb0 = tok0 // tq
    sb0 = tok0 // ls

    def bias_spec(t):
        def imap(b, s):
            r = s * NA_RPS + t
            return (_na_window_start(r) - r + NA_WIN_R - 1, 0, 0)
        return pl.BlockSpec((None, N_HEADS * GRID_W, NA_WIN_R * GRID_W + lc), imap)

    return pl.pallas_call(
        _na_lat_kernel,
        grid=(bs, nstep),
        in_specs=[pl.BlockSpec((tq, 256), lambda b, s: (rb0 + b * nstep + s, P_NQ // 256)),
                  pl.BlockSpec((ls, 256), lambda b, s: (sb0 + b, P_NK // 256)),
                  pl.BlockSpec((ls, 256), lambda b, s: (sb0 + b, P_NV // 256)),
                  pl.BlockSpec((None, lc, 256), lambda b, s: (b, 0, 0)),
                  pl.BlockSpec((None, lc, 256), lambda b, s: (b, 0, 0))] + [bias_spec(t) for t in range(NA_RPS)],
        out_specs=pl.BlockSpec((tq, 256), lambda b, s: (b * nstep + s, 0)),
        out_shape=jax.ShapeDtypeStruct((bs * ls, 256), F32),
        compiler_params=_cparams(("arbitrary", "arbitrary")),
        name="na_latent",
    )(proj, proj, proj, kc, vc, *([bias] * NA_RPS))


def _na_ctx_kernel(q_ref, k_ref, v_ref, o_ref, kc_ref, vc_ref):
    qs = _stack_heads(q_ref[...] * (NA_DH ** -0.5), NA_DH).astype(BF16)
    k = k_ref[...]
    v = v_ref[...]
    s = _nt(qs, k.astype(BF16))
    m = jnp.max(s, axis=1, keepdims=True)
    p = jnp.exp(s - m)
    den = jnp.sum(p, axis=1, keepdims=True)
    o = jnp.dot(p.astype(BF16), v.astype(BF16), preferred_element_type=F32)
    o_ref[...] = _unstack_heads(o / den, NA_DH)
    for h in range(N_HEADS):
        kc_ref[h] = k[:, h * NA_DH:(h + 1) * NA_DH]
        vc_ref[h] = v[:, h * NA_DH:(h + 1) * NA_DH]


def na_context(proj, bp, lp):
    cache_spec = pl.BlockSpec((None, N_HEADS, lp, NA_DH), lambda b: (b, 0, 0, 0))
    cache_shape = jax.ShapeDtypeStruct((bp, N_HEADS, lp, NA_DH), F32)
    return pl.pallas_call(
        _na_ctx_kernel,
        grid=(bp,),
        in_specs=[pl.BlockSpec((lp, 256), lambda b: (b, P_NQ // 256)),
                  pl.BlockSpec((lp, 256), lambda b: (b, P_NK // 256)),
                  pl.BlockSpec((lp, 256), lambda b: (b, P_NV // 256))],
        out_specs=[pl.BlockSpec((lp, 256), lambda b: (b, 0)), cache_spec, cache_spec],
        out_shape=[jax.ShapeDtypeStruct((bp * lp, 256), F32), cache_shape, cache_shape],
        compiler_params=_cparams(("arbitrary",)),
        name="na_context",
    )(proj, proj, proj)


GLA_DK = 32
GLA_DV = 64
GLA_C = 128
GLA_NORMALIZER = 16.0


def _gla_kernel(has_state, seq, *refs):
    if has_state:
        q_ref, k_ref, v_ref, g_ref, aux_ref, wg_ref, bg_ref, gn_ref, s0_ref, y_ref, acc_ref, bcum_ref = refs
    else:
        q_ref, k_ref, v_ref, g_ref, aux_ref, wg_ref, bg_ref, gn_ref, y_ref, sfin_ref, acc_ref, bcum_ref = refs
    c_sz = min(GLA_C, seq)
    n_chunks = seq // c_sz
    mid = c_sz // 2
    hk = N_HEADS * GLA_DK
    ti = lax.broadcasted_iota(jnp.int32, (c_sz, c_sz), 0)
    tj = lax.broadcasted_iota(jnp.int32, (c_sz, c_sz), 1)
    ai = lax.broadcasted_iota(jnp.int32, (N_HEADS * c_sz, c_sz), 0) % c_sz
    aj = lax.broadcasted_iota(jnp.int32, (N_HEADS * c_sz, c_sz), 1)
    bd = _block_diag_mask(N_HEADS * GLA_DV, hk, GLA_DV, GLA_DK)
    tri2 = jnp.concatenate([(tj <= ti).astype(BF16), (tj >= ti).astype(BF16)], axis=0)
    n_lr = wg_ref.shape[0]

    def pre(c, carry):
        rows = pl.ds(pl.multiple_of(c * c_sz, c_sz), c_sz)
        la = jax.nn.log_sigmoid(jnp.dot(aux_ref[rows, 0:n_lr], wg_ref[...], precision=HI, preferred_element_type=F32)
                                + bg_ref[...]) * (1.0 / GLA_NORMALIZER)
        l1 = la.astype(BF16)
        r1 = la - l1.astype(F32)
        l2 = r1.astype(BF16)
        l3 = (r1 - l2.astype(F32)).astype(BF16)
        bb = (jnp.dot(tri2, l1, preferred_element_type=F32) + jnp.dot(tri2, l2, preferred_element_type=F32)
              + jnp.dot(tri2, l3, preferred_element_type=F32))
        bcum_ref[rows, 0:hk] = bb[0:c_sz, 0:hk]
        bcum_ref[rows, hk:2 * hk] = bb[c_sz:2 * c_sz, hk:2 * hk]
        return carry

    lax.fori_loop(0, n_chunks, pre, 0)

    def step(d, c, st):
        amask = (aj <= ai) if d == 0 else (aj >= ai)
        rows = pl.ds(pl.multiple_of(c * c_sz, c_sz), c_sz)
        q = q_ref[rows, :] * (GLA_DK ** -0.5)
        k = k_ref[rows, :]
        v = v_ref[rows, :].astype(BF16)
        b = bcum_ref[rows, d * hk:(d + 1) * hk]
        btot = b[c_sz - 1:c_sz, :] if d == 0 else b[0:1, :]
        ref = b[mid - 1:mid, :] if d == 0 else b[mid:mid + 1, :]
        qt = q * jnp.exp(b - ref)
        kt = (k * jnp.exp(ref - b)).astype(BF16)
        ke = (k * jnp.exp(btot - b)).astype(BF16)
        a = _nt(_stack_heads(qt, GLA_DK).astype(BF16), kt)
        a = jnp.where(amask, a, 0.0).astype(BF16)
        o = _unstack_heads(jnp.dot(a, v, preferred_element_type=F32), GLA_DV)
        o += _nt((qt * jnp.exp(ref)).astype(BF16), st.astype(BF16))
        acc_ref[rows, :] += o
        upd = _tn(v, ke)
        return st * jnp.exp(btot) + jnp.where(bd, upd, 0.0)

    def body(n, sts):
        return step(0, n, sts[0]), step(1, n_chunks - 1 - n, sts[1])

    acc_ref[...] = jnp.zeros(acc_ref.shape, F32)
    if has_state:
        st0 = (s0_ref[0], s0_ref[1])
    else:
        st0 = (jnp.zeros((N_HEADS * GLA_DV, hk), F32),) * 2
    st_f, st_b = lax.fori_loop(0, n_chunks, body, st0)
    if not has_state:
        sfin_ref[0] = st_f
        sfin_ref[1] = st_b
    gn = gn_ref[...]

    def epi(i, carry):
        rows = pl.ds(pl.multiple_of(i * 256, 256), 256)
        g = g_ref[rows, :]
        y_ref[rows, :] = _head_rmsnorm(acc_ref[rows, :], GLA_DV) * gn * (g * jax.nn.sigmoid(g))
        return carry

    lax.fori_loop(0, seq // 256, epi, 0)


def gla_mixer(proj, tok0, nbatch, seq, w_gate, b_gate, norm_g, s0t):
    has_state = s0t is not None
    sb = tok0 // seq
    hk = N_HEADS * GLA_DK
    wg = jnp.concatenate([w_gate[0], w_gate[1]], axis=1)
    n_lr = wg.shape[0]
    col = lambda w, c0: pl.BlockSpec((seq, w), lambda b: (sb + b, c0 // w), pipeline_mode=pl.Buffered(1))
    in_specs = [col(128, P_GQ), col(128, P_GK), col(256, P_GV), col(256, P_GG), col(128, P_AUX),
                pl.BlockSpec((n_lr, 2 * hk), lambda b: (0, 0)),
                pl.BlockSpec((1, 2 * hk), lambda b: (0, 0)),
                pl.BlockSpec((1, 256), lambda b: (0, 0))]
    args = [proj, proj, proj, proj, proj, wg, b_gate.reshape(1, 2 * hk), jnp.tile(norm_g, N_HEADS).reshape(1, 256)]
    y_spec = pl.BlockSpec((seq, 256), lambda b: (b, 0))
    y_shape = jax.ShapeDtypeStruct((nbatch * seq, 256), F32)
    st_spec = pl.BlockSpec((None, 2, 256, 128), lambda b: (b, 0, 0, 0))
    if has_state:
        in_specs.append(st_spec)
        args.append(s0t)
        out_specs, out_shape = y_spec, y_shape
    else:
        out_specs = [y_spec, st_spec]
        out_shape = [y_shape, jax.ShapeDtypeStruct((nbatch, 2, 256, 128), F32)]
    return pl.pallas_call(
        functools.partial(_gla_kernel, has_state, seq),
        grid=(nbatch,),
        in_specs=in_specs, out_specs=out_specs, out_shape=out_shape,
        scratch_shapes=[pltpu.VMEM((seq, 256), F32), pltpu.VMEM((seq, 2 * hk), F32)],
        compiler_params=_cparams(("arbitrary",)),
        name="gla_latent" if has_state else "gla_context",
    )(*args)


ML_DH = 64
ML_C = 256
ROPE_BASE = 10000.0
ML_GATE_LANE0 = 16


def _ml_gate_selectors():
    rep = np.zeros((2, 2, LANES, N_HEADS * ML_DH), np.float32)
    sel = np.zeros((2, 8, LANES), np.float32)
    for d in range(2):
        for g in range(2):
            for h in range(N_HEADS):
                lane = ML_GATE_LANE0 + d * 8 + g * 4 + h
                rep[d, g, lane, h * ML_DH:(h + 1) * ML_DH] = 1.0
                sel[d, g * 4 + h, lane] = 1.0
    return jnp.asarray(rep), jnp.asarray(sel)


def _rope_tables(seq):
    nf = ML_DH // 4
    inv = ROPE_BASE ** (-jnp.arange(nf, dtype=F32) / nf)
    t = np.arange(seq)
    j = np.arange(ML_DH)
    pos = np.where(j[None, :] < ML_DH // 2, (t // GRID_W)[:, None], (t % GRID_W)[:, None]).astype(np.float32)
    ang = jnp.asarray(pos) * inv[j % nf][None, :]
    first = (j % (ML_DH // 2)) < nf
    cos = jnp.tile(jnp.cos(ang), (1, N_HEADS))
    sin = jnp.tile(jnp.where(first[None, :], -jnp.sin(ang), jnp.sin(ang)), (1, N_HEADS))
    return cos, sin


def _rope(x, cos, sin_signed):
    nf = ML_DH // 4
    first = (lax.broadcasted_iota(jnp.int32, x.shape, 1) % (ML_DH // 2)) < nf
    partner = jnp.where(first, pltpu.roll(x, x.shape[1] - nf, 1), pltpu.roll(x, nf, 1))
    return x * cos + partner * sin_signed


def _mlstm_kernel(latent, seq, *refs):
    if latent:
        (q_ref, k_ref, v_ref, og_ref, aux_ref, rep_ref, sel_ref, brep_ref, bsel_ref, gn_ref, cos_ref, sin_ref,
         c0_ref, n0_ref, m0_ref, y_ref, acc_ref) = refs
    else:
        (q_ref, k_ref, v_ref, og_ref, aux_ref, rep_ref, sel_ref, brep_ref, bsel_ref, gn_ref,
         y_ref, cf_ref, nf_ref, mf_ref, acc_ref) = refs
    c_sz = min(ML_C, seq)
    n_chunks = seq // c_sz
    hw = N_HEADS * ML_DH
    ti = lax.broadcasted_iota(jnp.int32, (c_sz, c_sz), 0)
    tj = lax.broadcasted_iota(jnp.int32, (c_sz, c_sz), 1)
    bd = _block_diag_mask(hw, hw, ML_DH, ML_DH)
    for d in range(2):
        causal = (tj <= ti) if d == 0 else (tj >= ti)
        tri = causal.astype(F32)
        tri_t = ((ti <= tj) if d == 0 else (ti >= tj)).astype(F32)

        def body(n, carry, d=d, causal=causal, tri=tri, tri_t=tri_t):
            cm, nrow, mrow = carry
            c = n if d == 0 else n_chunks - 1 - n
            rows = pl.ds(pl.multiple_of(c * c_sz, c_sz), c_sz)
            q = q_ref[rows, :]
            k = k_ref[rows, :] * (ML_DH ** -0.5)
            if latent:
                q = _rope(q, cos_ref[rows, :], sin_ref[rows, :])
                k = _rope(k, cos_ref[rows, :], sin_ref[rows, :])
            v = v_ref[rows, :].astype(BF16)
            aux = aux_ref[rows, :]
            li_m = jnp.dot(aux, rep_ref[d, 0], precision=HI, preferred_element_type=F32) + brep_ref[d, 0]
            lf_m = jax.nn.log_sigmoid(jnp.dot(aux, rep_ref[d, 1], precision=HI, preferred_element_type=F32)
                                      + brep_ref[d, 1])
            f_m = jnp.dot(tri, lf_m, precision=HI, preferred_element_type=F32)
            g_t = _nt(sel_ref[d], aux, precision=HI) + bsel_ref[d][:, 0:1]
            li_t = g_t[0:N_HEADS, :]
            f_t = jnp.dot(jax.nn.log_sigmoid(g_t[N_HEADS:2 * N_HEADS, :]), tri_t, precision=HI,
                          preferred_element_type=F32)
            dms, fcols, mcols = [], [], []
            for h in range(N_HEADS):
                fcol = f_m[:, h * ML_DH:h * ML_DH + 1]
                dms.append(jnp.where(causal, fcol - f_t[h:h + 1, :] + li_t[h:h + 1, :], -jnp.inf))
                fcols.append(fcol)
                mcols.append(jnp.broadcast_to(mrow[:, h * ML_DH:h * ML_DH + 1], (c_sz, 1)))
            dm = jnp.concatenate(dms, axis=0)
            log_inter = jnp.concatenate(fcols, axis=0) + jnp.concatenate(mcols, axis=0)
            m_t = jnp.maximum(log_inter, jnp.max(dm, axis=1, keepdims=True))
            qs = _stack_heads(q, ML_DH)
            qsb = qs.astype(BF16)
            s = _nt(qsb, k.astype(BF16)) * jnp.exp(dm - m_t)
            a_t = jnp.exp(log_inter - m_t)
            inter = jnp.dot(qsb, cm.astype(BF16), preferred_element_type=F32)
            num = a_t * inter + jnp.dot(s.astype(BF16), v, preferred_element_type=F32)
            den = a_t * jnp.sum(qs * nrow, axis=1, keepdims=True) + jnp.sum(s, axis=1, keepdims=True)
            hst = num / jnp.maximum(jnp.abs(den), jnp.exp(-m_t))
            hout = _unstack_heads(hst, ML_DH)
            if d == 0:
                acc_ref[rows, :] = hout
            else:
                acc_ref[rows, :] += hout
            f_tot = f_m[c_sz - 1:c_sz, :] if d == 0 else f_m[0:1, :]
            w_end = f_tot - f_m + li_m
            m_new = jnp.maximum(f_tot + mrow, jnp.max(w_end, axis=0, keepdims=True))
            a = jnp.exp(f_tot + mrow - m_new)
            kw = k * jnp.exp(w_end - m_new)
            cm_new = cm * a + jnp.where(bd, _tn(kw.astype(BF16), v), 0.0)
            n_new = nrow * a + jnp.sum(kw, axis=0, keepdims=True)
            return cm_new, n_new, m_new

        if latent:
            init = (c0_ref[d], n0_ref[d], m0_ref[d])
        else:
            init = (jnp.zeros((hw, hw), F32), jnp.zeros((1, hw), F32), jnp.zeros((1, hw), F32))
        cm, nrow, mrow = lax.fori_loop(0, n_chunks, body, init)
        if not latent:
            cf_ref[d] = cm
            nf_ref[d] = nrow
            mf_ref[d] = mrow
    gn = gn_ref[...]

    def epi(i, carry):
        rows = pl.ds(pl.multiple_of(i * 256, 256), 256)
        y_ref[rows, :] = _head_rmsnorm(acc_ref[rows, :], ML_DH) * gn * jax.nn.sigmoid(og_ref[rows, :])
        return carry

    lax.fori_loop(0, seq // 256, epi, 0)


def mlstm_mixer(proj, tok0, nbatch, seq, b_gate, norm_g, state):
    latent = state is not None
    sb = tok0 // seq
    hw = N_HEADS * ML_DH
    rep, sel = _ml_gate_selectors()
    brep = jnp.repeat(b_gate.reshape(2, 2, N_HEADS), ML_DH, axis=-1).reshape(2, 2, 1, hw)
    bsel = jnp.broadcast_to(b_gate.reshape(2, 8, 1), (2, 8, LANES))
    col = lambda c0: pl.BlockSpec((seq, 256), lambda b: (sb + b, c0 // 256), pipeline_mode=pl.Buffered(1))
    full = lambda shape: pl.BlockSpec(shape, lambda b: (0,) * len(shape), pipeline_mode=pl.Buffered(1))
    in_specs = [col(P_MQ), col(P_MK), col(P_MV), col(P_MO),
                pl.BlockSpec((seq, 128), lambda b: (sb + b, P_AUX // 128), pipeline_mode=pl.Buffered(1)),
                full((2, 2, LANES, hw)), full((2, 8, LANES)), full((2, 2, 1, hw)), full((2, 8, LANES)), full((1, hw))]
    args = [proj, proj, proj, proj, proj, rep, sel, brep, bsel, norm_g.reshape(1, hw)]
    y_spec = pl.BlockSpec((seq, 256), lambda b: (b, 0))
    y_shape = jax.ShapeDtypeStruct((nbatch * seq, 256), F32)
    c_spec = pl.BlockSpec((None, 2, hw, hw), lambda b: (b, 0, 0, 0))
    r_spec = pl.BlockSpec((None, 2, 1, hw), lambda b: (b, 0, 0, 0))
    if latent:
        cos, sin = _rope_tables(seq)
        in_specs += [full((seq, hw)), full((seq, hw)), c_spec, r_spec, r_spec]
        args += [cos, sin, *state]
        out_specs, out_shape = y_spec, y_shape
    else:
        out_specs = [y_spec, c_spec, r_spec, r_spec]
        out_shape = [y_shape, jax.ShapeDtypeStruct((nbatch, 2, hw, hw), F32),
                     jax.ShapeDtypeStruct((nbatch, 2, 1, hw), F32), jax.ShapeDtypeStruct((nbatch, 2, 1, hw), F32)]
    return pl.pallas_call(
        functools.partial(_mlstm_kernel, latent, seq),
        grid=(nbatch,),
        in_specs=in_specs, out_specs=out_specs, out_shape=out_shape,
        scratch_shapes=[pltpu.VMEM((seq, 256), F32)],
        compiler_params=_cparams(("arbitrary",)),
        name="mlstm_latent" if latent else "mlstm_context",
    )(*args)


HY_CH = 256
HY_BANDS = 16
HY_EMB = 1 + 2 * HY_BANDS
HY_FFN = 64
FFT_N1 = 64
FFT_N2 = 128


HY_LANE_FWD = HY_EMB
HY_LANE_BWD = HY_EMB + 1


def _hy_filter_kernel(slab, feat_ref, w1_ref, b1_ref, w2_ref, b2_ref, w3_ref, b3_ref, fr_ref, dl_ref, o_ref):
    feats = feat_ref[...]
    a = jnp.sin(fr_ref[0:1, :] * (jnp.dot(feats, w1_ref[...], precision=HI, preferred_element_type=F32) + b1_ref[...]))
    a = jnp.sin(fr_ref[1:2, :] * (jnp.dot(a, w2_ref[...], precision=HI, preferred_element_type=F32) + b2_ref[...]))
    a = jnp.dot(a, w3_ref[...], precision=HI, preferred_element_type=F32) + b3_ref[...]
    a = a * jnp.exp(-feats[:, 0:1] * dl_ref[...])
    fwd = feats[:, HY_LANE_FWD:HY_LANE_FWD + 1]
    bwd = feats[:, HY_LANE_BWD:HY_LANE_BWD + 1]
    for order in range(2):
        h = (fwd * a[:, (2 * order) * HY_CH:(2 * order + 1) * HY_CH]
             + bwd * a[:, (2 * order + 1) * HY_CH:(2 * order + 2) * HY_CH])
        if slab:
            for j in range(h.shape[0] // slab):
                o_ref[order, :, j, :] = h[j * slab:(j + 1) * slab, :]
        else:
            o_ref[order] = h


def hyena_filters(seq, w1, b1, w2, b2, w3, b3, freq, slab=0):
    n = jnp.arange(2 * seq)
    t = jnp.where(n < seq, n, 2 * seq - n).astype(F32)
    t = jnp.where(n == seq, 0.0, t)
    t_unit = t / (seq - 1)
    bands = jnp.linspace(1e-4, HY_BANDS - 1, HY_BANDS, dtype=F32)
    ang = (2.0 * math.pi / seq) * t[:, None] * bands[None, :]
    feats = jnp.concatenate([t_unit[:, None], jnp.cos(ang), -jnp.sin(ang),
                             (n < seq).astype(F32)[:, None], (n > seq).astype(F32)[:, None],
                             jnp.zeros((2 * seq, LANES - HY_EMB - 2), F32)], axis=-1)
    w1p = jnp.zeros((LANES, HY_FFN), F32).at[0:HY_EMB].set(w1)
    deltas = jnp.abs(jnp.linspace(math.log(1e-2) / 1.5, math.log(1e-2) / 0.3, HY_CH, dtype=F32))
    rb = min(2 * seq, 1024 if slab else 512)
    full = lambda shape: pl.BlockSpec(shape, lambda i: (0,) * len(shape))
    if slab:
        out_spec = pl.BlockSpec((2, slab, rb // slab, HY_CH), lambda i: (0, 0, i, 0))
        out_shape = jax.ShapeDtypeStruct((2, slab, 2 * seq // slab, HY_CH), F32)
    else:
        out_spec = pl.BlockSpec((2, rb, HY_CH), lambda i: (0, i, 0))
        out_shape = jax.ShapeDtypeStruct((2, 2 * seq, HY_CH), F32)
    return pl.pallas_call(
        functools.partial(_hy_filter_kernel, slab),
        grid=(2 * seq // rb,),
        in_specs=[pl.BlockSpec((rb, LANES), lambda i: (i, 0)), full((LANES, HY_FFN)), full((1, HY_FFN)),
                  full((HY_FFN, HY_FFN)), full((1, HY_FFN)), full((HY_FFN, 4 * HY_CH)), full((1, 4 * HY_CH)),
                  full((2, HY_FFN)), full((1, 4 * HY_CH))],
        out_specs=out_spec, out_shape=out_shape,
        compiler_params=_cparams(("arbitrary",)),
        name="hyena_filters",
    )(feats, w1p, b1.reshape(1, -1), w2, b2.reshape(1, -1), w3, b3.reshape(1, -1), freq,
      jnp.tile(deltas, 4).reshape(1, -1))


def _hy_short_kernel(nblk, slab, u_ref, prev_ref, next_ref, w_ref, b_ref, x1_ref, x2_ref, z_ref):
    i = pl.program_id(1)
    u = u_ref[...]
    rb = u.shape[0]
    row = lax.broadcasted_iota(jnp.int32, u.shape, 0)
    prev_row = jnp.where(i > 0, prev_ref[SUBLANES - 1:SUBLANES, :], 0.0)
    next_row = jnp.where(i < nblk - 1, next_ref[0:1, :], 0.0)
    up = jnp.where(row == 0, prev_row, pltpu.roll(u, 1, 0))
    un = jnp.where(row == rb - 1, next_row, pltpu.roll(u, rb - 1, 0))
    y = up * w_ref[0:1, :] + u * w_ref[1:2, :] + un * w_ref[2:3, :] + b_ref[...]
    for k, o_ref in enumerate((x1_ref, x2_ref, z_ref)):
        if slab:
            for a in range(rb // slab):
                o_ref[:, a, :] = y[a * slab:(a + 1) * slab, k * HY_CH:(k + 1) * HY_CH]
        else:
            o_ref[...] = y[:, k * HY_CH:(k + 1) * HY_CH]


def hyena_short_conv(proj, tok0, nbatch, seq, w, b, slab=0):
    rb = min(seq, 1024 if slab else 512)
    nblk = seq // rb
    r0 = tok0 // rb
    h0 = tok0 // SUBLANES
    hpb = rb // SUBLANES
    last = (tok0 + nbatch * seq) // SUBLANES - 1
    if slab:
        o_spec = pl.BlockSpec((None, slab, rb // slab, HY_CH), lambda bb, i: (bb, 0, i, 0))
        o_shape = jax.ShapeDtypeStruct((nbatch, slab, seq // slab, HY_CH), F32)
    else:
        o_spec = pl.BlockSpec((rb, HY_CH), lambda bb, i: (bb * nblk + i, 0))
        o_shape = jax.ShapeDtypeStruct((nbatch * seq, HY_CH), F32)
    return pl.pallas_call(
        functools.partial(_hy_short_kernel, nblk, slab),
        grid=(nbatch, nblk),
        in_specs=[pl.BlockSpec((rb, 3 * HY_CH), lambda bb, i: (r0 + bb * nblk + i, P_HU // (3 * HY_CH))),
                  pl.BlockSpec((SUBLANES, 3 * HY_CH),
                               lambda bb, i: (jnp.maximum(h0 + (bb * nblk + i) * hpb - 1, 0), P_HU // (3 * HY_CH))),
                  pl.BlockSpec((SUBLANES, 3 * HY_CH),
                               lambda bb, i: (jnp.minimum(h0 + (bb * nblk + i + 1) * hpb, last), P_HU // (3 * HY_CH))),
                  pl.BlockSpec((3, 3 * HY_CH), lambda bb, i: (0, 0)),
                  pl.BlockSpec((1, 3 * HY_CH), lambda bb, i: (0, 0))],
        out_specs=[o_spec, o_spec, o_spec],
        out_shape=[o_shape, o_shape, o_shape],
        compiler_params=_cparams(("arbitrary", "arbitrary")),
        name="hyena_short_conv",
    )(proj, proj, proj, w, b.reshape(1, -1))


def _dft_consts_single(seq):
    n = 2 * seq
    k = np.arange(n)[:, None].astype(np.float64)
    m = np.arange(n)[None, :].astype(np.float64)
    ang = 2.0 * np.pi * k * m / n
    fwd = np.concatenate([np.cos(ang), -np.sin(ang)], axis=0)
    inv = np.concatenate([np.cos(ang.T[:seq]), -np.sin(ang.T[:seq])], axis=1) / n
    return (jnp.asarray(fwd, F32), jnp.asarray(fwd[:, :seq], F32), jnp.asarray(inv, F32))


def _cmul(zr, zi, hr, hi):
    return zr * hr - zi * hi, zr * hi + zi * hr


def _split_bf16(a):
    hi = a.astype(BF16)
    return hi, (a - hi.astype(F32)).astype(BF16)


def _dot3(a, b, dims=None):
    if dims is None:
        dims = (((a.ndim - 1,), (0,)), ((), ()))
    a_hi, a_lo = _split_bf16(a)
    b_hi, b_lo = _split_bf16(b)
    dg = functools.partial(lax.dot_general, dimension_numbers=dims, preferred_element_type=F32)
    return dg(a_hi, b_hi) + dg(a_lo, b_hi) + dg(a_hi, b_lo)


def _hy_spec_single_kernel(f_ref, g_ref, o_ref):
    o_ref[...] = _dot3(f_ref[...], g_ref[...])


def _hy_conv_single_kernel(x1_ref, x2_ref, z_ref, h_ref, bias_ref, f_ref, i_ref, o_ref):
    n = f_ref.shape[0] // 2
    z = z_ref[...]
    for order, xg_ref in enumerate((x1_ref, x2_ref)):
        zz = _dot3(f_ref[...], z)
        pr, pi = _cmul(zz[0:n], zz[n:2 * n], h_ref[order, 0:n, :], h_ref[order, n:2 * n, :])
        y = _dot3(i_ref[...], jnp.concatenate([pr, pi], axis=0))
        z = xg_ref[...] * (y + z * bias_ref[order])
    o_ref[...] = z


def hyena_context(x1, x2, z, g, bias, nbatch, seq):
    n = 2 * seq
    f_full, f_half, inv = _dft_consts_single(seq)
    spec = pl.pallas_call(
        _hy_spec_single_kernel,
        grid=(2,),
        in_specs=[pl.BlockSpec((2 * n, n), lambda o: (0, 0)), pl.BlockSpec((None, n, HY_CH), lambda o: (o, 0, 0))],
        out_specs=pl.BlockSpec((None, 2 * n, HY_CH), lambda o: (o, 0, 0)),
        out_shape=jax.ShapeDtypeStruct((2, 2 * n, HY_CH), F32),
        compiler_params=_cparams(("arbitrary",)),
        name="hyena_spec_context",
    )(f_full, g)
    blk = pl.BlockSpec((seq, HY_CH), lambda b: (b, 0))
    return pl.pallas_call(
        _hy_conv_single_kernel,
        grid=(nbatch,),
        in_specs=[blk, blk, blk, pl.BlockSpec((2, 2 * n, HY_CH), lambda b: (0, 0, 0)),
                  pl.BlockSpec((2, 1, HY_CH), lambda b: (0, 0, 0)),
                  pl.BlockSpec((2 * n, seq), lambda b: (0, 0)), pl.BlockSpec((seq, 2 * n), lambda b: (0, 0))],
        out_specs=blk,
        out_shape=jax.ShapeDtypeStruct((nbatch * seq, HY_CH), F32),
        compiler_params=_cparams(("arbitrary",)),
        name="hyena_conv_context",
    )(x1, x2, z, spec, bias.reshape(2, 1, HY_CH), f_half, inv)


def _dft_consts_two_stage():
    n1, n2 = FFT_N1, FFT_N2
    n = n1 * n2
    a2 = np.arange(n2, dtype=np.float64)[:, None, None]
    k1 = np.arange(n1, dtype=np.float64)[None, :, None]
    a1 = np.arange(n1, dtype=np.float64)[None, None, :]
    th = 2.0 * np.pi * (a1 * k1 / n1 + a2 * k1 / n)
    w1 = np.concatenate([np.cos(th), -np.sin(th)], axis=1)
    tht = np.transpose(th, (0, 2, 1))
    w3 = np.concatenate([np.cos(tht), -np.sin(tht)], axis=2) / n
    ph = 2.0 * np.pi * np.arange(n2, dtype=np.float64)[:, None] * np.arange(n2, dtype=np.float64)[None, :] / n2
    c, s = np.cos(ph), np.sin(ph)
    g2 = np.block([[c, s], [-s, c]])
    g2i = np.block([[c, -s], [s, c]])
    return (jnp.asarray(w1, F32), jnp.asarray(w3, F32), jnp.asarray(g2, F32), jnp.asarray(g2i, F32))


FFT_SB = 16
FFT_KB = 16


def _dotc(w, x):
    return jnp.dot(w, x.astype(BF16), preferred_element_type=F32)


def _outer_fwd_kernel(x_ref, w_ref, o_ref):
    for jj in range(FFT_SB):
        y = _dotc(w_ref[jj], x_ref[jj])
        o_ref[jj, 0] = y[0:FFT_N1]
        o_ref[jj, 1] = y[FFT_N1:2 * FFT_N1]


def _outer_fwd(w, xs):
    nbatch, n2, n1, ch = xs.shape
    wspec = pl.BlockSpec((FFT_SB, 2 * FFT_N1, n1), lambda b, j: (j, 0, 0))
    return pl.pallas_call(
        _outer_fwd_kernel,
        grid=(nbatch, n2 // FFT_SB),
        in_specs=[pl.BlockSpec((None, FFT_SB, n1, ch), lambda b, j: (b, j, 0, 0)), wspec],
        out_specs=pl.BlockSpec((None, FFT_SB, 2, FFT_N1, ch), lambda b, j: (b, j, 0, 0, 0)),
        out_shape=jax.ShapeDtypeStruct((nbatch, n2, 2, FFT_N1, ch), F32),
        compiler_params=_cparams(("arbitrary", "arbitrary")),
        name="hyena_dft_outer_fwd",
    )(xs, w.astype(BF16))


def _inner_kernel(conv, *refs):
    if conv:
        a_ref, g_ref, h_ref, i_ref, o_ref = refs
    else:
        a_ref, g_ref, o_ref = refs
    n = FFT_N2
    for kk in range(FFT_KB):
        x = jnp.concatenate([a_ref[:, 0, kk, :], a_ref[:, 1, kk, :]], axis=0)
        y = _dotc(g_ref[...], x)
        if conv:
            pr, pi = _cmul(y[0:n], y[n:2 * n], h_ref[kk, 0:n, :], h_ref[kk, n:2 * n, :])
            q = _dotc(i_ref[...], jnp.concatenate([pr, pi], axis=0))
            o_ref[kk, 0] = q[0:n]
            o_ref[kk, 1] = q[n:2 * n]
        else:
            o_ref[kk] = y


def _inner_stage(a5, g2, spec=None, g2i=None):
    nbatch, n2, _, n1, ch = a5.shape
    conv = spec is not None
    mat = pl.BlockSpec((2 * n2, 2 * n2), lambda b, j: (0, 0))
    in_specs = [pl.BlockSpec((None, n2, 2, FFT_KB, ch), lambda b, j: (b, 0, 0, j, 0)), mat]
    args = [a5, g2.astype(BF16)]
    if conv:
        in_specs += [pl.BlockSpec((FFT_KB, 2 * n2, ch), lambda b, j: (j, 0, 0)), mat]
        args += [spec, g2i.astype(BF16)]
        out_spec = pl.BlockSpec((None, FFT_KB, 2, n2, ch), lambda b, j: (b, j, 0, 0, 0))
        out_shape = jax.ShapeDtypeStruct((nbatch, n1, 2, n2, ch), F32)
    else:
        out_spec = pl.BlockSpec((None, FFT_KB, 2 * n2, ch), lambda b, j: (b, j, 0, 0))
        out_shape = jax.ShapeDtypeStruct((nbatch, n1, 2 * n2, ch), F32)
    return pl.pallas_call(
        functools.partial(_inner_kernel, conv),
        grid=(nbatch, n1 // FFT_KB),
        in_specs=in_specs, out_specs=out_spec, out_shape=out_shape,
        compiler_params=_cparams(("arbitrary", "arbitrary")),
        name="hyena_dft_inner_conv" if conv else "hyena_dft_inner_spec",
    )(*args)


def _outer_inv_kernel(to_time_major, q_ref, w_ref, *refs):
    if to_time_major:
        perm_ref, xg_ref, z_ref, b_ref, o_ref = refs
    else:
        xg_ref, z_ref, b_ref, o_ref = refs
    outs = []
    for jj in range(FFT_SB):
        qm = jnp.concatenate([q_ref[:, 0, jj, :], q_ref[:, 1, jj, :]], axis=0)
        g = xg_ref[jj] * (_dotc(w_ref[jj], qm) + z_ref[jj] * b_ref[...])
        if to_time_major:
            outs.append(g)
        else:
            o_ref[jj] = g
    if to_time_major:
        y = jnp.concatenate(outs, axis=0)
        h1 = y.astype(BF16)
        r1 = y - h1.astype(F32)
        h2 = r1.astype(BF16)
        h3 = (r1 - h2.astype(F32)).astype(BF16)
        p = perm_ref[...]
        yp = (jnp.dot(p, h1, preferred_element_type=F32) + jnp.dot(p, h2, preferred_element_type=F32)
              + jnp.dot(p, h3, preferred_element_type=F32))
        o_ref[...] = yp.reshape(o_ref.shape)


def _outer_inv_gate(w, q5, xg, z, bias_row, to_time_major):
    nbatch, n2, n1h, ch = z.shape
    slab = pl.BlockSpec((None, FFT_SB, n1h, ch), lambda b, j: (b, j, 0, 0))
    wspec = pl.BlockSpec((FFT_SB, n1h, 2 * FFT_N1), lambda b, j: (j, 0, 0))
    in_specs = [pl.BlockSpec((None, FFT_N1, 2, FFT_SB, ch), lambda b, j: (b, 0, 0, j, 0)), wspec]
    args = [q5, w.astype(BF16)]
    if to_time_major:
        rows = FFT_SB * n1h
        r = np.arange(rows)
        perm = np.zeros((rows, rows), np.float32)
        perm[r, (r % FFT_SB) * n1h + r // FFT_SB] = 1.0
        in_specs.append(pl.BlockSpec((rows, rows), lambda b, j: (0, 0)))
        args.append(jnp.asarray(perm, BF16))
        out_spec = pl.BlockSpec((None, n1h, FFT_SB, ch), lambda b, j: (b, 0, j, 0))
        out_shape = jax.ShapeDtypeStruct((nbatch, n1h, n2, ch), F32)
    else:
        out_spec, out_shape = slab, jax.ShapeDtypeStruct(z.shape, F32)
    in_specs += [slab, slab, pl.BlockSpec((1, ch), lambda b, j: (0, 0))]
    args += [xg, z, bias_row]
    return pl.pallas_call(
        functools.partial(_outer_inv_kernel, to_time_major),
        grid=(nbatch, n2 // FFT_SB),
        in_specs=in_specs, out_specs=out_spec, out_shape=out_shape,
        compiler_params=_cparams(("arbitrary", "arbitrary")),
        name="hyena_dft_outer_inv_gate",
    )(*args)


def hyena_latent(x1, x2, z, g, bias, nbatch, seq):
    n1, n2 = FFT_N1, FFT_N2
    assert 2 * seq == n1 * n2
    w1, w3, g2, g2i = _dft_consts_two_stage()
    half = seq // n2
    spec = _inner_stage(_outer_fwd(w1, g), g2)
    w1h = w1[:, :, 0:half]
    w3h = w3[:, 0:half, :]
    q5 = _inner_stage(_outer_fwd(w1h, z), g2, spec[0], g2i)
    z1 = _outer_inv_gate(w3h, q5, x1, z, bias[0].reshape(1, HY_CH), False)
    q5 = _inner_stage(_outer_fwd(w1h, z1), g2, spec[1], g2i)
    out = _outer_inv_gate(w3h, q5, x2, z1, bias[1].reshape(1, HY_CH), True)
    return out.reshape(nbatch * seq, HY_CH)


def _permute_w_in(w):
    sizes = (128, 128, 256, 256, 16, 256, 256, 256, 256, 16, 256, 256, 256, 768)
    offs = np.cumsum((0,) + sizes)
    seg = lambda j: w[:, offs[j]:offs[j + 1]]
    order = (13, 0, 1, 2, 3, 5, 6, 7, 8, 10, 11, 12, 4, 9)
    pad = jnp.zeros((w.shape[0], P_W - P_AUX - 32), w.dtype)
    return jnp.concatenate([seg(j) for j in order] + [pad], axis=1)


def kernel(x_prompt, x_sample, cache_na_k, cache_na_v, state_gla, state_mlstm_C, state_mlstm_n, state_mlstm_m, c, c_ctx, w_ada, b_ada, norm1_g, norm2_g, w_in, w_out, gla_w_gate, gla_b_gate, gla_norm_g, ml_b_gate, ml_norm_g, na_rpb, hy_conv_w, hy_conv_b, hy_w1, hy_b1, hy_w2, hy_b2, hy_w3, hy_b3, hy_freq, hy_bias, router_w, router_b, w_gu, b_gu, w_dn, b_dn, final_norm_g):
    depth = w_ada.shape[0]
    bp, lp, _ = x_prompt.shape
    bs, ls, _ = x_sample.shape
    tp = bp * lp
    assert tp == ls, "modulation rows are selected per block of DEC_SEQ tokens"
    x = jnp.concatenate([x_prompt.reshape(tp, D), x_sample.reshape(bs * ls, D)], axis=0)
    cond8 = jnp.concatenate([c_ctx[None, :], c, jnp.zeros((8 - 1 - bs, D), F32)], axis=0)
    mods_all = ada_mods(cond8, w_ada, b_ada)
    fg = final_norm_g.reshape(1, D)
    eye_h = jnp.eye(N_HEADS, dtype=F32)
    lc = cache_na_k.shape[3]
    new_k, new_v, new_gla, new_c, new_n, new_m = [], [], [], [], [], []
    for l in range(depth):
        mods = mods_all[l].reshape(8, 1, 6 * D)
        proj = in_proj(x, mods, norm1_g[l].reshape(1, D), _permute_w_in(w_in[l]).astype(BF16), ls)
        gla_c, gla_fin = gla_mixer(proj, 0, bp, lp, gla_w_gate[l], gla_b_gate[l], gla_norm_g[l], None)
        s0t = jnp.einsum('bdhkv,hg->bdhvgk', state_gla[:, l], eye_h).reshape(bs, 2, 256, 128)
        gla_s = gla_mixer(proj, tp, bs, ls, gla_w_gate[l], gla_b_gate[l], gla_norm_g[l], s0t)
        ml_c, cf, nf, mf = mlstm_mixer(proj, 0, bp, lp, ml_b_gate[l], ml_norm_g[l], None)
        c0 = jnp.einsum('bdhvk,hg->bdhkgv', state_mlstm_C[:, l], eye_h).reshape(bs, 2, 256, 256)
        n0 = state_mlstm_n[:, l].reshape(bs, 2, 1, 256)
        m0 = jnp.repeat(state_mlstm_m[:, l], ML_DH, axis=-1).reshape(bs, 2, 1, 256)
        ml_s = mlstm_mixer(proj, tp, bs, ls, ml_b_gate[l], ml_norm_g[l], (c0, n0, m0))
        na_c, k_l, v_l = na_context(proj, bp, lp)
        kct = cache_na_k[:, l].transpose(0, 2, 1, 3).reshape(bs, lc, 256)
        vct = cache_na_v[:, l].transpose(0, 2, 1, 3).reshape(bs, lc, 256)
        na_s = na_latent(proj, tp, bs, ls, kct, vct, na_rpb[l])
        hy_args = (hy_w1[l], hy_b1[l], hy_w2[l], hy_b2[l], hy_w3[l], hy_b3[l], hy_freq[l])
        hy_c = hyena_context(*hyena_short_conv(proj, 0, bp, lp, hy_conv_w[l], hy_conv_b[l]),
                             hyena_filters(lp, *hy_args), hy_bias[l], bp, lp)
        hy_s = hyena_latent(*hyena_short_conv(proj, tp, bs, ls, hy_conv_w[l], hy_conv_b[l], FFT_N2),
                            hyena_filters(ls, *hy_args, FFT_N2), hy_bias[l], bs, ls)
        x1, h2, eidx, ew, cnt_b = out_proj_route(
            (gla_c, ml_c, na_c, hy_c), (gla_s, ml_s, na_s, hy_s), x, mods, norm2_g[l].reshape(1, D), w_out[l].astype(BF16), router_w[l].T,
            jnp.broadcast_to(router_b[l][:, None], (N_EXP, LANES)), ls)
        x = moe_layer(h2, eidx, ew, cnt_b, x1, mods, fg, l, w_gu, b_gu, w_dn, b_dn, ls, l == depth - 1)
        new_k.append(k_l)
        new_v.append(v_l)
        gla_blocks = [gla_fin[:, :, h * GLA_DV:(h + 1) * GLA_DV, h * GLA_DK:(h + 1) * GLA_DK] for h in range(N_HEADS)]
        new_gla.append(jnp.stack(gla_blocks, 2).transpose(0, 1, 2, 4, 3))
        c_blocks = [cf[:, :, h * ML_DH:(h + 1) * ML_DH, h * ML_DH:(h + 1) * ML_DH] for h in range(N_HEADS)]
        new_c.append(jnp.stack(c_blocks, 2).transpose(0, 1, 2, 4, 3))
        new_n.append(nf.reshape(bp, 2, N_HEADS, ML_DH))
        new_m.append(mf.reshape(bp, 2, N_HEADS, ML_DH)[..., 0])
    y_prompt = x[:tp].reshape(bp, lp, D)
    y_sample = x[tp:].reshape(bs, ls, D)
    return (y_prompt, y_sample, jnp.stack(new_k, 1), jnp.stack(new_v, 1), jnp.stack(new_gla, 1),
            jnp.stack(new_c, 1), jnp.stack(new_n, 1), jnp.stack(new_m, 1))
```
